```python
import math
import jax, jax.numpy as jnp
from jax import lax
import numpy as np

D_MODEL = 1024
BATCH = 4
SEQ = 4096
DEPTH = 4

MIX_WIDTH = D_MODEL
GMLP_WIDTH = MIX_WIDTH // 2
N_GMLP_HEADS = 8
GMLP_HEAD_DIM = GMLP_WIDTH // N_GMLP_HEADS
CHUNK = 128
DIFF_WIDTH = MIX_WIDTH - GMLP_WIDTH
N_DIFF_HEADS = 4
DIFF_V_DIM = DIFF_WIDTH // N_DIFF_HEADS
DIFF_HEAD_DIM = DIFF_V_DIM // 2
BLOCK_Q = 128
IN_WIDTH = 2 * GMLP_WIDTH + 3 * DIFF_WIDTH
D_FF = ((8 * D_MODEL // 3 + 127) // 128) * 128
N_EXPERTS = 8
TOP_K = 2
D_FF_EXPERT = D_FF
N_DENSE = (DEPTH + 1) // 2
N_MOE = DEPTH // 2
N_MOD = 6
EPS = 1e-6

kernel_name = "hybrid_gmlp_diffattn_moe_adaln"


def _rms(x):
    xf = x.astype(jnp.float32)
    return (xf * lax.rsqrt(jnp.mean(xf * xf, axis=-1, keepdims=True) + EPS)).astype(x.dtype)


def rmsnorm(x, g):
    return _rms(x) * g


def swiglu(h, w_gate, w_up, w_down):
    return (jax.nn.silu(h @ w_gate) * (h @ w_up)) @ w_down


def gmlp_mixer(u, v, vnorm_g, w_s, b_s):
    B, S, _ = u.shape
    u = jax.nn.gelu(u)
    v = jax.nn.gelu(v).reshape(B, S // CHUNK, CHUNK, N_GMLP_HEADS, GMLP_HEAD_DIM)
    v = _rms(v) * vnorm_g.reshape(N_GMLP_HEADS, GMLP_HEAD_DIM)
    causal = jnp.tril(jnp.ones((CHUNK, CHUNK), dtype=bool))
    ws = jnp.where(causal[None], w_s, jnp.zeros_like(w_s))
    vm = jnp.einsum('gts,bcsgd->bctgd', ws, v) + b_s.T[:, :, None]
    return u * vm.reshape(B, S, GMLP_WIDTH)


def diff_attention(q, k, v, lam, lam_init, subln_g):
    B, S, H, _, dh = q.shape
    scale = dh ** -0.5
    nb = S // BLOCK_Q
    qb = q.reshape(B, nb, BLOCK_Q, H, 2, dh).transpose(1, 0, 2, 3, 4, 5)
    kpos = jnp.arange(S)

    def block(args):
        qi, i = args
        s = jnp.einsum('bqhmd,bkhmd->bhmqk', qi, k).astype(jnp.float32) * scale
        qpos = i * BLOCK_Q + jnp.arange(BLOCK_Q)
        allowed = kpos[None, :] <= qpos[:, None]
        s = jnp.where(allowed, s, -jnp.inf)
        p = jax.nn.softmax(s, axis=-1)
        a = p[:, :, 0] - lam * p[:, :, 1]
        return jnp.einsum('bhqk,bkhd->bqhd', a.astype(v.dtype), v)

    o = lax.map(block, (qb, jnp.arange(nb)))
    o = o.transpose(1, 0, 2, 3, 4).reshape(B, S, H, v.shape[-1])
    o = _rms(o) * subln_g * (1.0 - lam_init)
    return o.reshape(B, S, H * v.shape[-1])


def moe_swiglu(h, w_router, w_gate, w_up, w_down):
    B, S, D = h.shape
    hf = h.reshape(B * S, D)
    logits = (hf @ w_router).astype(jnp.float32)
    top_v, top_i = lax.top_k(logits, TOP_K)
    top_w = jax.nn.softmax(top_v, axis=-1)
    gates = jnp.sum(jax.nn.one_hot(top_i, N_EXPERTS, dtype=jnp.float32) * top_w[..., None], axis=1)
    gates = gates.astype(h.dtype)
    out = jnp.zeros_like(hf)
    for e in range(N_EXPERTS):
        out = out + gates[:, e:e + 1] * swiglu(hf, w_gate[e], w_up[e], w_down[e])
    return out.reshape(B, S, D)


def setup_inputs(seed: int = 0) -> dict:
    key = jax.random.key(seed)
    ks = jax.random.split(key, 32)
    f32 = jnp.float32
    nrm = lambda k, shape, s: jax.random.normal(k, shape, f32) * s
    D = D_MODEL
    return {
        "x": nrm(ks[0], (BATCH, SEQ, D), 1.0),
        "c": nrm(ks[1], (BATCH, D), 1.0),
        "w_ada": nrm(ks[2], (DEPTH, D, N_MOD * D), 0.5 * D ** -0.5),
        "b_ada": nrm(ks[3], (DEPTH, N_MOD * D), 0.02),
        "norm1_g": 1.0 + nrm(ks[4], (DEPTH, D), 0.02),
        "norm2_g": 1.0 + nrm(ks[5], (DEPTH, D), 0.02),
        "w_in": nrm(ks[6], (DEPTH, D, IN_WIDTH), D ** -0.5),
        "w_out": nrm(ks[7], (DEPTH, MIX_WIDTH, D), MIX_WIDTH ** -0.5),
        "gmlp_vnorm_g": 1.0 + nrm(ks[8], (DEPTH, GMLP_WIDTH), 0.02),
        "gmlp_ws": nrm(ks[9], (DEPTH, N_GMLP_HEADS, CHUNK, CHUNK), CHUNK ** -0.5),
        "gmlp_bs": 1.0 + nrm(ks[10], (DEPTH, N_GMLP_HEADS, CHUNK), 0.1),
        "lam_q1": nrm(ks[11], (DEPTH, DIFF_HEAD_DIM), 0.1),
        "lam_k1": nrm(ks[12], (DEPTH, DIFF_HEAD_DIM), 0.1),
        "lam_q2": nrm(ks[13], (DEPTH, DIFF_HEAD_DIM), 0.1),
        "lam_k2": nrm(ks[14], (DEPTH, DIFF_HEAD_DIM), 0.1),
        "diff_subln_g": 1.0 + nrm(ks[15], (DEPTH, DIFF_V_DIM), 0.02),
        "ffn_w_gate": nrm(ks[16], (N_DENSE, D, D_FF), D ** -0.5),
        "ffn_w_up": nrm(ks[17], (N_DENSE, D, D_FF), D ** -0.5),
        "ffn_w_down": nrm(ks[18], (N_DENSE, D_FF, D), D_FF ** -0.5),
        "w_router": nrm(ks[19], (N_MOE, D, N_EXPERTS), D ** -0.5),
        "moe_w_gate": nrm(ks[20], (N_MOE, N_EXPERTS, D, D_FF_EXPERT), D ** -0.5),
        "moe_w_up": nrm(ks[21], (N_MOE, N_EXPERTS, D, D_FF_EXPERT), D ** -0.5),
        "moe_w_down": nrm(ks[22], (N_MOE, N_EXPERTS, D_FF_EXPERT, D), D_FF_EXPERT ** -0.5),
        "final_g": 1.0 + nrm(ks[23], (D,), 0.02),
    }


def reference(x, c, w_ada, b_ada, norm1_g, norm2_g, w_in, w_out, gmlp_vnorm_g, gmlp_ws, gmlp_bs,
              lam_q1, lam_k1, lam_q2, lam_k2, diff_subln_g, ffn_w_gate, ffn_w_up, ffn_w_down,
              w_router, moe_w_gate, moe_w_up, moe_w_down, final_g):
    B, S, D = x.shape
    sc = jax.nn.silu(c)
    splits = [GMLP_WIDTH, 2 * GMLP_WIDTH, 2 * GMLP_WIDTH + DIFF_WIDTH, 2 * GMLP_WIDTH + 2 * DIFF_WIDTH]
    for l in range(DEPTH):
        mod = sc @ w_ada[l] + b_ada[l]
        sh1, sc1, g1, sh2, sc2, g2 = [m[:, None, :] for m in jnp.split(mod, N_MOD, axis=-1)]

        h = rmsnorm(x, norm1_g[l]) * (1.0 + sc1) + sh1
        z = h @ w_in[l]
        u_a, v_a, q_b, k_b, v_b = jnp.split(z, splits, axis=-1)
        out_a = gmlp_mixer(u_a, v_a, gmlp_vnorm_g[l], gmlp_ws[l], gmlp_bs[l])
        lam_init = 0.8 - 0.6 * math.exp(-0.3 * l)
        lam = (jnp.exp(jnp.sum((lam_q1[l] * lam_k1[l]).astype(jnp.float32)))
               - jnp.exp(jnp.sum((lam_q2[l] * lam_k2[l]).astype(jnp.float32))) + lam_init)
        q = q_b.reshape(B, S, N_DIFF_HEADS, 2, DIFF_HEAD_DIM)
        k = k_b.reshape(B, S, N_DIFF_HEADS, 2, DIFF_HEAD_DIM)
        v = v_b.reshape(B, S, N_DIFF_HEADS, DIFF_V_DIM)
        out_b = diff_attention(q, k, v, lam, lam_init, diff_subln_g[l])
        y = jnp.concatenate([out_a, out_b], axis=-1) @ w_out[l]
        x = x + g1 * y

        h = rmsnorm(x, norm2_g[l]) * (1.0 + sc2) + sh2
        if l % 2 == 0:
            i = l // 2
            f = swiglu(h, ffn_w_gate[i], ffn_w_up[i], ffn_w_down[i])
        else:
            i = l // 2
            f = moe_swiglu(h, w_router[i], moe_w_gate[i], moe_w_up[i], moe_w_down[i])
        x = x + g2 * f
    return rmsnorm(x, final_g)
```

```python
import functools
import math

import jax
import jax.numpy as jnp
import numpy as np
from jax import lax
from jax.experimental import pallas as pl
from jax.experimental.pallas import tpu as pltpu

F32 = jnp.float32
BF16 = jnp.bfloat16
EPS = 1e-6
N_MOD = 6
N_GMLP_HEADS = 8
GMLP_HEAD_DIM = 64
CHUNK = 128
N_DIFF_HEADS = 4
DIFF_HEAD_DIM = 64
DIFF_V_DIM = 128
N_EXPERTS = 8
LANES = 128
VMEM_LIMIT = 56 * 1024 * 1024

SH1, SC1, G1, SH2, SC2, G2 = range(N_MOD)


def _params(*sem):
    return pltpu.CompilerParams(dimension_semantics=sem, vmem_limit_bytes=VMEM_LIMIT)


def _modnorm(x, g, sc, sh):
    ms = jnp.mean(x * x, axis=-1, keepdims=True)
    return (x * lax.rsqrt(ms + EPS)) * g * (1.0 + sc) + sh


def _ada_body(c_ref, w_ref, b_ref, o_ref):
    c = c_ref[...]
    sc = c * jax.nn.sigmoid(c)
    o_ref[0] = jnp.dot(sc, w_ref[0], preferred_element_type=F32,
                       precision=lax.Precision.HIGHEST) + b_ref[0]


def _ada(c, w_ada, b_ada):
    depth, d, m = w_ada.shape
    b = c.shape[0]
    rows = 8
    cp = jnp.zeros((rows, d), F32).at[:b].set(c)
    tn = 1536
    out = pl.pallas_call(
        _ada_body,
        grid=(depth, m // tn),
        in_specs=[pl.BlockSpec((rows, d), lambda l, j: (0, 0)),
                  pl.BlockSpec((1, d, tn), lambda l, j: (l, 0, j)),
                  pl.BlockSpec((1, 1, tn), lambda l, j: (l, 0, j))],
        out_specs=pl.BlockSpec((1, rows, tn), lambda l, j: (l, 0, j)),
        out_shape=jax.ShapeDtypeStruct((depth, rows, m), F32),
        compiler_params=_params("parallel", "parallel"),
        name="ada",
    )(cp, w_ada, b_ada.reshape(depth, 1, m))
    return out[:, :b].reshape(depth, b, N_MOD, d)


def _inproj_body(x_ref, mod_ref, g_ref, w_ref, z_ref, *, nchunk):
    h = _modnorm(x_ref[...], g_ref[...], mod_ref[0, SC1:SC1 + 1, :],
                 mod_ref[0, SH1:SH1 + 1, :]).astype(BF16)
    cw = w_ref.shape[1] // nchunk
    for j in range(nchunk):
        z_ref[:, j * cw:(j + 1) * cw] = jnp.dot(
            h, w_ref[:, j * cw:(j + 1) * cw], preferred_element_type=F32).astype(z_ref.dtype)


def _inproj(x, mod, g, w, seq):
    n, d = x.shape
    width = w.shape[1]
    tm = 512
    return pl.pallas_call(
        functools.partial(_inproj_body, nchunk=width // 512),
        grid=(n // tm,),
        in_specs=[pl.BlockSpec((tm, d), lambda i: (i, 0)),
                  pl.BlockSpec((1, N_MOD, d), lambda i: (i * tm // seq, 0, 0)),
                  pl.BlockSpec((1, d), lambda i: (0, 0)),
                  pl.BlockSpec((d, width), lambda i: (0, 0))],
        out_specs=pl.BlockSpec((tm, width), lambda i: (i, 0)),
        out_shape=jax.ShapeDtypeStruct((n, width), BF16),
        compiler_params=_params("parallel"),
        name="inproj",
    )(x, mod, g, w)


def _gmlp_body(u_ref, v_ref, vg_ref, w_ref, bias_ref, gm_ref, o_ref, *, nck):
    width = u_ref.shape[1]
    npair = width // LANES
    row = lax.broadcasted_iota(jnp.int32, (CHUNK, 2 * CHUNK), 0)
    col = lax.broadcasted_iota(jnp.int32, (CHUNK, 2 * CHUNK), 1)
    causal = jnp.where(col >= CHUNK, col - CHUNK, col) <= row
    lane = lax.broadcasted_iota(jnp.int32, (CHUNK, LANES), 1)
    left = lane < GMLP_HEAD_DIM
    wp = [jnp.where(causal, w_ref[p], jnp.zeros_like(w_ref[p])) for p in range(npair)]
    gm = gm_ref[...]
    for c in range(nck):
        rs = slice(c * CHUNK, (c + 1) * CHUNK)
        gu = jax.nn.gelu(u_ref[rs, :].astype(F32))
        gv = jax.nn.gelu(v_ref[rs, :].astype(F32))
        v2 = gv * gv
        hi = v2.astype(BF16)
        lo = (v2 - hi.astype(F32)).astype(BF16)
        ms = (jnp.dot(hi, gm, preferred_element_type=F32)
              + jnp.dot(lo, gm, preferred_element_type=F32))
        vn = gv * lax.rsqrt(ms + EPS) * vg_ref[...]
        outs = []
        for p in range(npair):
            vp = vn[:, p * LANES:(p + 1) * LANES]
            zero = jnp.zeros_like(vp)
            rhs = jnp.concatenate([jnp.where(left, vp, zero), jnp.where(left, zero, vp)],
                                  axis=0).astype(BF16)
            outs.append(jnp.dot(wp[p], rhs, preferred_element_type=F32))
        vm = jnp.concatenate(outs, axis=1) + bias_ref[...]
        o_ref[rs, :] = (gu * vm).astype(o_ref.dtype)


def _gmlp(z, vnorm_g, w_s, b_s):
    n = z.shape[0]
    width = N_GMLP_HEADS * GMLP_HEAD_DIM
    nck = 4
    rows = nck * CHUNK
    npair = N_GMLP_HEADS // 2
    w_pair = w_s.reshape(npair, 2, CHUNK, CHUNK).transpose(0, 2, 1, 3).reshape(
        npair, CHUNK, 2 * CHUNK).astype(BF16)
    bias = jnp.repeat(b_s.T, GMLP_HEAD_DIM, axis=1)
    grp = np.arange(width) // GMLP_HEAD_DIM
    group_mean = jnp.asarray((grp[:, None] == grp[None, :]) / GMLP_HEAD_DIM, dtype=BF16)
    return pl.pallas_call(
        functools.partial(_gmlp_body, nck=nck),
        grid=(n // rows,),
        in_specs=[pl.BlockSpec((rows, width), lambda i: (i, 0)),
                  pl.BlockSpec((rows, width), lambda i: (i, 1)),
                  pl.BlockSpec((1, width), lambda i: (0, 0)),
                  pl.BlockSpec((npair, CHUNK, 2 * CHUNK), lambda i: (0, 0, 0)),
                  pl.BlockSpec((CHUNK, width), lambda i: (0, 0)),
                  pl.BlockSpec((width, width), lambda i: (0, 0))],
        out_specs=pl.BlockSpec((rows, width), lambda i: (i, 0)),
        out_shape=jax.ShapeDtypeStruct((n, width), BF16),
        compiler_params=_params("parallel"),
        name="gmlp",
    )(z, z, vnorm_g.reshape(1, width), w_pair, bias, group_mean)


def _attn_body(q_ref, k_ref, v_ref, lam_ref, g_ref, o_ref, qs, m_s, l_s, acc, *, tq, lam_init):
    qi = pl.program_id(2)
    ki = pl.program_id(3)

    @pl.when(ki == 0)
    def _init():
        q = q_ref[...]
        qsc = q * jnp.asarray(DIFF_HEAD_DIM ** -0.5, q.dtype)
        lane = lax.broadcasted_iota(jnp.int32, q.shape, 1)
        zero = jnp.zeros_like(qsc)
        qs[0:tq, :] = jnp.where(lane < DIFF_HEAD_DIM, qsc, zero)
        qs[tq:, :] = jnp.where(lane < DIFF_HEAD_DIM, zero, qsc)
        m_s[...] = jnp.full(m_s.shape, -jnp.inf, F32)
        l_s[...] = jnp.zeros(l_s.shape, F32)
        acc[...] = jnp.zeros(acc.shape, F32)

    def step(masked):
        s = lax.dot_general(qs[...], k_ref[...], (((1,), (1,)), ((), ())),
                            preferred_element_type=F32)
        if masked:
            row = lax.broadcasted_iota(jnp.int32, s.shape, 0)
            row = jnp.where(row >= tq, row - tq, row)
            col = lax.broadcasted_iota(jnp.int32, s.shape, 1)
            s = jnp.where(col <= row, s, -jnp.inf)
        m_prev = m_s[...]
        m_new = jnp.maximum(m_prev, jnp.max(s, axis=-1, keepdims=True))
        alpha = jnp.exp(m_prev - m_new)
        p = jnp.exp(s - m_new)
        l_s[...] = alpha * l_s[...] + jnp.sum(p, axis=-1, keepdims=True)
        acc[...] = alpha * acc[...] + jnp.dot(p.astype(BF16), v_ref[...],
                                              preferred_element_type=F32)
        m_s[...] = m_new

    @pl.when(ki < qi)
    def _full():
        step(False)

    @pl.when(ki == qi)
    def _diag():
        step(True)
        lp = lam_ref[...]
        s1 = jnp.sum(lp[0:1, :] * lp[1:2, :], axis=-1, keepdims=True)
        s2 = jnp.sum(lp[2:3, :] * lp[3:4, :], axis=-1, keepdims=True)
        lam = jnp.exp(s1) - jnp.exp(s2) + lam_init
        o1 = acc[0:tq, :] / l_s[0:tq, :]
        o2 = acc[tq:, :] / l_s[tq:, :]
        o = o1 - lam * o2
        o = o * lax.rsqrt(jnp.mean(o * o, axis=-1, keepdims=True) + EPS)
        o_ref[...] = (o * g_ref[...] * (1.0 - lam_init)).astype(o_ref.dtype)


def _attn(z, lam_p, subln_g, lam_init, batch, seq):
    n = z.shape[0]
    tq = 512
    nq = seq // tq
    qcol = 2 * N_GMLP_HEADS * GMLP_HEAD_DIM // LANES
    kcol = qcol + N_DIFF_HEADS
    vcol = kcol + N_DIFF_HEADS
    return pl.pallas_call(
        functools.partial(_attn_body, tq=tq, lam_init=lam_init),
        grid=(batch, N_DIFF_HEADS, nq, nq),
        in_specs=[
            pl.BlockSpec((tq, LANES), lambda b, h, i, j: (b * nq + i, qcol + h)),
            pl.BlockSpec((tq, LANES), lambda b, h, i, j: (b * nq + jnp.minimum(i, j), kcol + h)),
            pl.BlockSpec((tq, LANES), lambda b, h, i, j: (b * nq + jnp.minimum(i, j), vcol + h)),
            pl.BlockSpec((4, DIFF_HEAD_DIM), lambda b, h, i, j: (0, 0)),
            pl.BlockSpec((1, DIFF_V_DIM), lambda b, h, i, j: (0, 0)),
        ],
        out_specs=pl.BlockSpec((tq, LANES), lambda b, h, i, j: (b * nq + i, h)),
        out_shape=jax.ShapeDtypeStruct((n, N_DIFF_HEADS * DIFF_V_DIM), BF16),
        scratch_shapes=[pltpu.VMEM((2 * tq, LANES), BF16),
                        pltpu.VMEM((2 * tq, 1), F32),
                        pltpu.VMEM((2 * tq, 1), F32),
                        pltpu.VMEM((2 * tq, DIFF_V_DIM), F32)],
        compiler_params=_params("parallel", "parallel", "parallel", "arbitrary"),
        name="diff_attn",
    )(z, z, z, lam_p, subln_g.reshape(1, DIFF_V_DIM))


def _outproj_body(a_ref, b_ref, w_ref, x_ref, mod_ref, o_ref):
    ka = a_ref.shape[1]
    y = jnp.dot(a_ref[...], w_ref[0:ka, :], preferred_element_type=F32)
    y = y + jnp.dot(b_ref[...], w_ref[ka:, :], preferred_element_type=F32)
    o_ref[...] = x_ref[...] + mod_ref[0, G1:G1 + 1, :] * y


def _outproj(out_a, out_b, w, x, mod, seq):
    n, d = x.shape
    tm = 512
    ka, kb = out_a.shape[1], out_b.shape[1]
    return pl.pallas_call(
        _outproj_body,
        grid=(n // tm,),
        in_specs=[pl.BlockSpec((tm, ka), lambda i: (i, 0)),
                  pl.BlockSpec((tm, kb), lambda i: (i, 0)),
                  pl.BlockSpec((ka + kb, d), lambda i: (0, 0)),
                  pl.BlockSpec((tm, d), lambda i: (i, 0)),
                  pl.BlockSpec((1, N_MOD, d), lambda i: (i * tm // seq, 0, 0))],
        out_specs=pl.BlockSpec((tm, d), lambda i: (i, 0)),
        out_shape=jax.ShapeDtypeStruct((n, d), F32),
        compiler_params=_params("parallel"),
        name="outproj",
    )(out_a, out_b, w, x, mod)


def _top2_gates(logits):
    lane = lax.broadcasted_iota(jnp.int32, logits.shape, 1)
    lg = jnp.where(lane < N_EXPERTS, logits, -jnp.inf)
    m1 = jnp.max(lg, axis=-1, keepdims=True)
    i1 = jnp.min(jnp.where(lg == m1, lane, LANES), axis=-1, keepdims=True)
    lg2 = jnp.where(lane == i1, -jnp.inf, lg)
    m2 = jnp.max(lg2, axis=-1, keepdims=True)
    i2 = jnp.min(jnp.where(lg2 == m2, lane, LANES), axis=-1, keepdims=True)
    e2 = jnp.exp(m2 - m1)
    den = 1.0 + e2
    return jnp.where(lane == i1, 1.0 / den, 0.0) + jnp.where(lane == i2, e2 / den, 0.0)


def _mixer_body(*refs, nf, moe, final):
    if moe:
        (x_ref, mod_ref, g_ref, fg_ref, wr_ref, wg_ref, wu_ref, wd_ref, o_ref,
         h_s, acc, tot, gates) = refs
    else:
        x_ref, mod_ref, g_ref, fg_ref, wg_ref, wu_ref, wd_ref, o_ref, h_s, acc = refs
    j = pl.program_id(1)
    nsteps = pl.num_programs(1)
    jf = j % nf

    @pl.when(j == 0)
    def _prologue():
        h = _modnorm(x_ref[...], g_ref[...], mod_ref[0, SC2:SC2 + 1, :], mod_ref[0, SH2:SH2 + 1, :])
        h_s[...] = h.astype(BF16)
        if moe:
            logits = jnp.dot(h, wr_ref[...], preferred_element_type=F32,
                             precision=lax.Precision.HIGHEST)
            gates[...] = _top2_gates(logits)
            tot[...] = jnp.zeros(tot.shape, F32)

    @pl.when(jf == 0)
    def _zero():
        acc[...] = jnp.zeros(acc.shape, F32)

    h = h_s[...]
    g = jnp.dot(h, wg_ref[0], preferred_element_type=F32)
    u = jnp.dot(h, wu_ref[0], preferred_element_type=F32)
    a = (g * jax.nn.sigmoid(g) * u).astype(BF16)
    acc[...] += jnp.dot(a, wd_ref[0], preferred_element_type=F32)

    if moe:
        @pl.when(jf == nf - 1)
        def _expert_done():
            lane = lax.broadcasted_iota(jnp.int32, gates.shape, 1)
            ge = jnp.sum(jnp.where(lane == j // nf, gates[...], 0.0), axis=-1, keepdims=True)
            tot[...] += ge * acc[...]

    @pl.when(j == nsteps - 1)
    def _epilogue():
        f = tot[...] if moe else acc[...]
        y = x_ref[...] + mod_ref[0, G2:G2 + 1, :] * f
        if final:
            y = y * lax.rsqrt(jnp.mean(y * y, axis=-1, keepdims=True) + EPS) * fg_ref[...]
        o_ref[...] = y


def _mixer(x, mod, g, final_g, w_gate, w_up, w_down, w_router, seq, final):
    n, d = x.shape
    n_e, _, f = w_gate.shape
    moe = w_router is not None
    tm = 1024
    tf = 256
    nf = f // tf
    in_specs = [pl.BlockSpec((tm, d), lambda i, j: (i, 0)),
                pl.BlockSpec((1, N_MOD, d), lambda i, j: (i * tm // seq, 0, 0)),
                pl.BlockSpec((1, d), lambda i, j: (0, 0)),
                pl.BlockSpec((1, d), lambda i, j: (0, 0))]
    args = [x, mod, g, final_g]
    scratch = [pltpu.VMEM((tm, d), BF16), pltpu.VMEM((tm, d), F32)]
    if moe:
        in_specs.append(pl.BlockSpec((d, LANES), lambda i, j: (0, 0)))
        args.append(w_router)
        scratch += [pltpu.VMEM((tm, d), F32), pltpu.VMEM((tm, LANES), F32)]
    in_specs += [pl.BlockSpec((1, d, tf), lambda i, j: (j // nf, 0, j % nf)),
                 pl.BlockSpec((1, d, tf), lambda i, j: (j // nf, 0, j % nf)),
                 pl.BlockSpec((1, tf, d), lambda i, j: (j // nf, j % nf, 0))]
    args += [w_gate, w_up, w_down]
    return pl.pallas_call(
        functools.partial(_mixer_body, nf=nf, moe=moe, final=final),
        grid=(n // tm, n_e * nf),
        in_specs=in_specs,
        out_specs=pl.BlockSpec((tm, d), lambda i, j: (i, 0)),
        out_shape=jax.ShapeDtypeStruct((n, d), F32),
        scratch_shapes=scratch,
        compiler_params=_params("parallel", "arbitrary"),
        name="moe_mixer" if moe else "ffn_mixer",
    )(*args)


def kernel(x, c, w_ada, b_ada, norm1_g, norm2_g, w_in, w_out, gmlp_vnorm_g, gmlp_ws, gmlp_bs,
           lam_q1, lam_k1, lam_q2, lam_k2, diff_subln_g, ffn_w_gate, ffn_w_up, ffn_w_down,
           w_router, moe_w_gate, moe_w_up, moe_w_down, final_g):
    batch, seq, d = x.shape
    depth = w_ada.shape[0]
    n = batch * seq
    mods = _ada(c, w_ada, b_ada)
    xf = x.reshape(n, d)
    fg = final_g.reshape(1, d)
    for l in range(depth):
        mod = mods[l]
        z = _inproj(xf, mod, norm1_g[l].reshape(1, d), w_in[l].astype(BF16), seq)
        out_a = _gmlp(z, gmlp_vnorm_g[l], gmlp_ws[l], gmlp_bs[l])
        lam_init = 0.8 - 0.6 * math.exp(-0.3 * l)
        lam_p = jnp.stack([lam_q1[l], lam_k1[l], lam_q2[l], lam_k2[l]])
        out_b = _attn(z, lam_p, diff_subln_g[l], lam_init, batch, seq)
        xf = _outproj(out_a, out_b, w_out[l].astype(BF16), xf, mod, seq)
        i = l // 2
        final = l == depth - 1
        g2 = norm2_g[l].reshape(1, d)
        if l % 2 == 0:
            xf = _mixer(xf, mod, g2, fg, ffn_w_gate[i][None].astype(BF16),
                        ffn_w_up[i][None].astype(BF16), ffn_w_down[i][None].astype(BF16),
                        None, seq, final)
        else:
            wr = jnp.zeros((d, LANES), F32).at[:, :N_EXPERTS].set(w_router[i])
            xf = _mixer(xf, mod, g2, fg, moe_w_gate[i].astype(BF16), moe_w_up[i].astype(BF16),
                        moe_w_down[i].astype(BF16), wr, seq, final)
    return xf.reshape(batch, seq, d)
```

```python
import functools
import math

import jax
import jax.numpy as jnp
import numpy as np
from jax import lax
from jax.experimental import pallas as pl
from jax.experimental.pallas import tpu as pltpu

F32 = jnp.float32
BF16 = jnp.bfloat16
EPS = 1e-6
N_MOD = 6
N_GMLP_HEADS = 8
GMLP_HEAD_DIM = 64
CHUNK = 128
N_DIFF_HEADS = 4
DIFF_HEAD_DIM = 64
DIFF_V_DIM = 128
N_EXPERTS = 8
LANES = 128
VMEM_LIMIT = 56 * 1024 * 1024

MOE_TILE = 1024
MOE_BLOCK = 288
MOE_FSPLIT = 2

SH1, SC1, G1, SH2, SC2, G2 = range(N_MOD)


def _params(*sem):
    return pltpu.CompilerParams(dimension_semantics=sem, vmem_limit_bytes=VMEM_LIMIT)


def _modnorm(x, g, sc, sh):
    ms = jnp.mean(x * x, axis=-1, keepdims=True)
    return (x * lax.rsqrt(ms + EPS)) * g * (1.0 + sc) + sh


def _ada_body(c_ref, w_ref, b_ref, o_ref):
    c = c_ref[...]
    sc = c * jax.nn.sigmoid(c)
    o_ref[0] = jnp.dot(sc, w_ref[0], preferred_element_type=F32,
                       precision=lax.Precision.HIGHEST) + b_ref[0]


def _ada(c, w_ada, b_ada):
    depth, d, m = w_ada.shape
    b = c.shape[0]
    rows = 8
    cp = jnp.zeros((rows, d), F32).at[:b].set(c)
    tn = 1536
    out = pl.pallas_call(
        _ada_body,
        grid=(depth, m // tn),
        in_specs=[pl.BlockSpec((rows, d), lambda l, j: (0, 0)),
                  pl.BlockSpec((1, d, tn), lambda l, j: (l, 0, j)),
                  pl.BlockSpec((1, 1, tn), lambda l, j: (l, 0, j))],
        out_specs=pl.BlockSpec((1, rows, tn), lambda l, j: (l, 0, j)),
        out_shape=jax.ShapeDtypeStruct((depth, rows, m), F32),
        compiler_params=_params("parallel", "parallel"),
        name="ada",
    )(cp, w_ada, b_ada.reshape(depth, 1, m))
    return out[:, :b].reshape(depth, b, N_MOD, d)


def _inproj_body(x_ref, mod_ref, g_ref, w_ref, z_ref, *, nchunk):
    h = _modnorm(x_ref[...], g_ref[...], mod_ref[0, SC1:SC1 + 1, :],
                 mod_ref[0, SH1:SH1 + 1, :]).astype(BF16)
    cw = w_ref.shape[1] // nchunk
    for j in range(nchunk):
        z_ref[:, j * cw:(j + 1) * cw] = jnp.dot(
            h, w_ref[:, j * cw:(j + 1) * cw], preferred_element_type=F32).astype(z_ref.dtype)


def _inproj(x, mod, g, w, seq):
    n, d = x.shape
    width = w.shape[1]
    tm = 512
    return pl.pallas_call(
        functools.partial(_inproj_body, nchunk=width // 512),
        grid=(n // tm,),
        in_specs=[pl.BlockSpec((tm, d), lambda i: (i, 0)),
                  pl.BlockSpec((1, N_MOD, d), lambda i: (i * tm // seq, 0, 0)),
                  pl.BlockSpec((1, d), lambda i: (0, 0)),
                  pl.BlockSpec((d, width), lambda i: (0, 0))],
        out_specs=pl.BlockSpec((tm, width), lambda i: (i, 0)),
        out_shape=jax.ShapeDtypeStruct((n, width), BF16),
        compiler_params=_params("parallel"),
        name="inproj",
    )(x, mod, g, w)


def _gmlp_body(u_ref, v_ref, vg_ref, w_ref, bias_ref, gm_ref, o_ref, *, nck):
    width = u_ref.shape[1]
    npair = width // LANES
    row = lax.broadcasted_iota(jnp.int32, (CHUNK, 2 * CHUNK), 0)
    col = lax.broadcasted_iota(jnp.int32, (CHUNK, 2 * CHUNK), 1)
    causal = jnp.where(col >= CHUNK, col - CHUNK, col) <= row
    lane = lax.broadcasted_iota(jnp.int32, (CHUNK, LANES), 1)
    left = lane < GMLP_HEAD_DIM
    wp = [jnp.where(causal, w_ref[p], jnp.zeros_like(w_ref[p])) for p in range(npair)]
    gm = gm_ref[...]
    for c in range(nck):
        rs = slice(c * CHUNK, (c + 1) * CHUNK)
        gu = jax.nn.gelu(u_ref[rs, :].astype(F32))
        gv = jax.nn.gelu(v_ref[rs, :].astype(F32))
        v2 = gv * gv
        hi = v2.astype(BF16)
        lo = (v2 - hi.astype(F32)).astype(BF16)
        ms = (jnp.dot(hi, gm, preferred_element_type=F32)
              + jnp.dot(lo, gm, preferred_element_type=F32))
        vn = gv * lax.rsqrt(ms + EPS) * vg_ref[...]
        outs = []
        for p in range(npair):
            vp = vn[:, p * LANES:(p + 1) * LANES]
            zero = jnp.zeros_like(vp)
            rhs = jnp.concatenate([jnp.where(left, vp, zero), jnp.where(left, zero, vp)],
                                  axis=0).astype(BF16)
            outs.append(jnp.dot(wp[p], rhs, preferred_element_type=F32))
        vm = jnp.concatenate(outs, axis=1) + bias_ref[...]
        o_ref[rs, :] = (gu * vm).astype(o_ref.dtype)


def _gmlp(z, vnorm_g, w_s, b_s):
    n = z.shape[0]
    width = N_GMLP_HEADS * GMLP_HEAD_DIM
    nck = 4
    rows = nck * CHUNK
    npair = N_GMLP_HEADS // 2
    w_pair = w_s.reshape(npair, 2, CHUNK, CHUNK).transpose(0, 2, 1, 3).reshape(
        npair, CHUNK, 2 * CHUNK).astype(BF16)
    bias = jnp.repeat(b_s.T, GMLP_HEAD_DIM, axis=1)
    grp = np.arange(width) // GMLP_HEAD_DIM
    group_mean = jnp.asarray((grp[:, None] == grp[None, :]) / GMLP_HEAD_DIM, dtype=BF16)
    return pl.pallas_call(
        functools.partial(_gmlp_body, nck=nck),
        grid=(n // rows,),
        in_specs=[pl.BlockSpec((rows, width), lambda i: (i, 0)),
                  pl.BlockSpec((rows, width), lambda i: (i, 1)),
                  pl.BlockSpec((1, width), lambda i: (0, 0)),
                  pl.BlockSpec((npair, CHUNK, 2 * CHUNK), lambda i: (0, 0, 0)),
                  pl.BlockSpec((CHUNK, width), lambda i: (0, 0)),
                  pl.BlockSpec((width, width), lambda i: (0, 0))],
        out_specs=pl.BlockSpec((rows, width), lambda i: (i, 0)),
        out_shape=jax.ShapeDtypeStruct((n, width), BF16),
        compiler_params=_params("parallel"),
        name="gmlp",
    )(z, z, vnorm_g.reshape(1, width), w_pair, bias, group_mean)


def _attn_body(q_ref, k_ref, vt_ref, lam_ref, g_ref, o_ref, qs, m_s, l_s, acc, *, tq, lam_init):
    qi = pl.program_id(2)
    ki = pl.program_id(3)

    @pl.when(ki == 0)
    def _init():
        q = q_ref[...]
        qsc = q * jnp.asarray(DIFF_HEAD_DIM ** -0.5, q.dtype)
        lane = lax.broadcasted_iota(jnp.int32, q.shape, 1)
        zero = jnp.zeros_like(qsc)
        qs[0:tq, :] = jnp.where(lane < DIFF_HEAD_DIM, qsc, zero)
        qs[tq:, :] = jnp.where(lane < DIFF_HEAD_DIM, zero, qsc)
        m_s[...] = jnp.full(m_s.shape, -jnp.inf, F32)
        l_s[...] = jnp.zeros(l_s.shape, F32)
        acc[...] = jnp.zeros(acc.shape, F32)

    def step(masked):
        st = lax.dot_general(k_ref[...], qs[...], (((1,), (1,)), ((), ())),
                             preferred_element_type=F32)
        if masked:
            key = lax.broadcasted_iota(jnp.int32, st.shape, 0)
            qry = lax.broadcasted_iota(jnp.int32, st.shape, 1)
            qry = jnp.where(qry >= tq, qry - tq, qry)
            st = jnp.where(key <= qry, st, -jnp.inf)
        m_prev = m_s[...]
        m_new = jnp.maximum(m_prev, jnp.max(st, axis=0, keepdims=True))
        alpha = jnp.exp(m_prev - m_new)
        pt = jnp.exp(st - m_new)
        l_s[...] = alpha * l_s[...] + jnp.sum(pt, axis=0, keepdims=True)
        acc[...] = alpha * acc[...] + jnp.dot(vt_ref[...], pt.astype(BF16),
                                              preferred_element_type=F32)
        m_s[...] = m_new

    @pl.when(ki < qi)
    def _full():
        step(False)

    @pl.when(ki == qi)
    def _diag():
        step(True)
        lp = lam_ref[...]
        s1 = jnp.sum(lp[0:1, :] * lp[1:2, :], axis=-1, keepdims=True)
        s2 = jnp.sum(lp[2:3, :] * lp[3:4, :], axis=-1, keepdims=True)
        lam = jnp.exp(s1) - jnp.exp(s2) + lam_init
        ot = acc[:, 0:tq] / l_s[:, 0:tq] - lam * (acc[:, tq:] / l_s[:, tq:])
        ot = ot * lax.rsqrt(jnp.mean(ot * ot, axis=0, keepdims=True) + EPS)
        o_ref[...] = (ot.T * g_ref[...] * (1.0 - lam_init)).astype(o_ref.dtype)


def _attn(z, vt, lam_p, subln_g, lam_init, batch, seq):
    n = z.shape[0]
    tq = 512
    nq = seq // tq
    qcol = 2 * N_GMLP_HEADS * GMLP_HEAD_DIM // LANES
    kcol = qcol + N_DIFF_HEADS
    return pl.pallas_call(
        functools.partial(_attn_body, tq=tq, lam_init=lam_init),
        grid=(batch, N_DIFF_HEADS, nq, nq),
        in_specs=[
            pl.BlockSpec((tq, LANES), lambda b, h, i, j: (b * nq + i, qcol + h)),
            pl.BlockSpec((tq, LANES), lambda b, h, i, j: (b * nq + jnp.minimum(i, j), kcol + h)),
            pl.BlockSpec((DIFF_V_DIM, tq), lambda b, h, i, j: (h, b * nq + jnp.minimum(i, j))),
            pl.BlockSpec((4, DIFF_HEAD_DIM), lambda b, h, i, j: (0, 0)),
            pl.BlockSpec((1, DIFF_V_DIM), lambda b, h, i, j: (0, 0)),
        ],
        out_specs=pl.BlockSpec((tq, LANES), lambda b, h, i, j: (b * nq + i, h)),
        out_shape=jax.ShapeDtypeStruct((n, N_DIFF_HEADS * DIFF_V_DIM), BF16),
        scratch_shapes=[pltpu.VMEM((2 * tq, LANES), BF16),
                        pltpu.VMEM((1, 2 * tq), F32),
                        pltpu.VMEM((1, 2 * tq), F32),
                        pltpu.VMEM((DIFF_V_DIM, 2 * tq), F32)],
        compiler_params=_params("parallel", "parallel", "parallel", "arbitrary"),
        name="diff_attn",
    )(z, z, vt, lam_p, subln_g.reshape(1, DIFF_V_DIM))


def _outproj_body(a_ref, b_ref, w_ref, x_ref, mod_ref, o_ref):
    ka = a_ref.shape[1]
    y = jnp.dot(a_ref[...], w_ref[0:ka, :], preferred_element_type=F32)
    y = y + jnp.dot(b_ref[...], w_ref[ka:, :], preferred_element_type=F32)
    o_ref[...] = x_ref[...] + mod_ref[0, G1:G1 + 1, :] * y


def _outproj(out_a, out_b, w, x, mod, seq):
    n, d = x.shape
    tm = 512
    ka, kb = out_a.shape[1], out_b.shape[1]
    return pl.pallas_call(
        _outproj_body,
        grid=(n // tm,),
        in_specs=[pl.BlockSpec((tm, ka), lambda i: (i, 0)),
                  pl.BlockSpec((tm, kb), lambda i: (i, 0)),
                  pl.BlockSpec((ka + kb, d), lambda i: (0, 0)),
                  pl.BlockSpec((tm, d), lambda i: (i, 0)),
                  pl.BlockSpec((1, N_MOD, d), lambda i: (i * tm // seq, 0, 0))],
        out_specs=pl.BlockSpec((tm, d), lambda i: (i, 0)),
        out_shape=jax.ShapeDtypeStruct((n, d), F32),
        compiler_params=_params("parallel"),
        name="outproj",
    )(out_a, out_b, w, x, mod)


def _ffn_body(x_ref, mod_ref, g_ref, fg_ref, wg_ref, wu_ref, wd_ref, o_ref, h_s, acc, *, final):
    j = pl.program_id(1)

    @pl.when(j == 0)
    def _prologue():
        h = _modnorm(x_ref[...], g_ref[...], mod_ref[0, SC2:SC2 + 1, :], mod_ref[0, SH2:SH2 + 1, :])
        h_s[...] = h.astype(BF16)
        acc[...] = jnp.zeros(acc.shape, F32)

    h = h_s[...]
    g = jnp.dot(h, wg_ref[...], preferred_element_type=F32)
    u = jnp.dot(h, wu_ref[...], preferred_element_type=F32)
    a = (g * jax.nn.sigmoid(g) * u).astype(BF16)
    acc[...] += jnp.dot(a, wd_ref[...], preferred_element_type=F32)

    @pl.when(j == pl.num_programs(1) - 1)
    def _epilogue():
        y = x_ref[...] + mod_ref[0, G2:G2 + 1, :] * acc[...]
        if final:
            y = y * lax.rsqrt(jnp.mean(y * y, axis=-1, keepdims=True) + EPS) * fg_ref[...]
        o_ref[...] = y


def _ffn(x, mod, g, final_g, w_gate, w_up, w_down, seq, final):
    n, d = x.shape
    f = w_gate.shape[1]
    tm = 1024
    tf = 256
    return pl.pallas_call(
        functools.partial(_ffn_body, final=final),
        grid=(n // tm, f // tf),
        in_specs=[pl.BlockSpec((tm, d), lambda i, j: (i, 0)),
                  pl.BlockSpec((1, N_MOD, d), lambda i, j: (i * tm // seq, 0, 0)),
                  pl.BlockSpec((1, d), lambda i, j: (0, 0)),
                  pl.BlockSpec((1, d), lambda i, j: (0, 0)),
                  pl.BlockSpec((d, tf), lambda i, j: (0, j)),
                  pl.BlockSpec((d, tf), lambda i, j: (0, j)),
                  pl.BlockSpec((tf, d), lambda i, j: (j, 0))],
        out_specs=pl.BlockSpec((tm, d), lambda i, j: (i, 0)),
        out_shape=jax.ShapeDtypeStruct((n, d), F32),
        scratch_shapes=[pltpu.VMEM((tm, d), BF16), pltpu.VMEM((tm, d), F32)],
        compiler_params=_params("parallel", "arbitrary"),
        name="ffn_mixer",
    )(x, mod, g, final_g, w_gate, w_up, w_down)


def _top2(logits):
    lane = lax.broadcasted_iota(jnp.int32, logits.shape, 1)
    lg = jnp.where(lane < N_EXPERTS, logits, -jnp.inf)
    m1 = jnp.max(lg, axis=-1, keepdims=True)
    i1 = jnp.min(jnp.where(lg == m1, lane, LANES), axis=-1, keepdims=True)
    lg2 = jnp.where(lane == i1, -jnp.inf, lg)
    m2 = jnp.max(lg2, axis=-1, keepdims=True)
    i2 = jnp.min(jnp.where(lg2 == m2, lane, LANES), axis=-1, keepdims=True)
    e2 = jnp.exp(m2 - m1)
    den = 1.0 + e2
    return i1, i2, 1.0 / den, e2 / den


def _router_body(x_ref, mod_ref, g_ref, wr_ref, h_ref, pos_ref, post_ref, gate_ref, cnt_ref):
    t = x_ref.shape[0]
    h = _modnorm(x_ref[...], g_ref[...], mod_ref[0, SC2:SC2 + 1, :], mod_ref[0, SH2:SH2 + 1, :])
    h_ref[...] = h.astype(BF16)
    logits = jnp.dot(h, wr_ref[...], preferred_element_type=F32, precision=lax.Precision.HIGHEST)
    i1, i2, w1, w2 = _top2(logits)
    lane = lax.broadcasted_iota(jnp.int32, logits.shape, 1)
    sel1 = lane == i1
    sel2 = lane == i2
    gate_ref[...] = jnp.where(sel1, w1, 0.0) + jnp.where(sel2, w2, 0.0)
    picked = jnp.logical_or(sel1, sel2)
    onehot = jnp.where(picked, 1.0, 0.0)
    row = lax.broadcasted_iota(jnp.int32, (t, t), 0)
    col = lax.broadcasted_iota(jnp.int32, (t, t), 1)
    before = jnp.where(col < row, 1.0, 0.0).astype(BF16)
    rank = jnp.dot(before, onehot.astype(BF16), preferred_element_type=F32)
    pos = jnp.where(picked, rank, -1.0)
    pos_ref[...] = pos
    post_ref[...] = pos.T[0:N_EXPERTS, :]
    cnt_ref[0] = jnp.broadcast_to(jnp.sum(onehot, axis=0, keepdims=True), cnt_ref.shape[1:])


def _router(x, mod, g, w_router, seq):
    n, d = x.shape
    t = MOE_TILE
    nt = n // t
    wr = jnp.zeros((d, LANES), F32).at[:, :N_EXPERTS].set(w_router)
    return pl.pallas_call(
        _router_body,
        grid=(nt,),
        in_specs=[pl.BlockSpec((t, d), lambda i: (i, 0)),
                  pl.BlockSpec((1, N_MOD, d), lambda i: (i * t // seq, 0, 0)),
                  pl.BlockSpec((1, d), lambda i: (0, 0)),
                  pl.BlockSpec((d, LANES), lambda i: (0, 0))],
        out_specs=[pl.BlockSpec((t, d), lambda i: (i, 0)),
                   pl.BlockSpec((t, LANES), lambda i: (i, 0)),
                   pl.BlockSpec((N_EXPERTS, t), lambda i: (0, i)),
                   pl.BlockSpec((t, LANES), lambda i: (i, 0)),
                   pl.BlockSpec((1, 8, LANES), lambda i: (i, 0, 0))],
        out_shape=[jax.ShapeDtypeStruct((n, d), BF16),
                   jax.ShapeDtypeStruct((n, LANES), F32),
                   jax.ShapeDtypeStruct((N_EXPERTS, n), F32),
                   jax.ShapeDtypeStruct((n, LANES), F32),
                   jax.ShapeDtypeStruct((nt, 8, LANES), F32)],
        compiler_params=_params("parallel"),
        name="router",
    )(x, mod, g, wr)


def _moe_body(cnt_ref, h_ref, pos_ref, post_ref, gate_ref, x_ref, mod_ref, fg_ref,
              wg_ref, wu_ref, wd_ref, o_ref, xg, acc, *, final):
    ti = pl.program_id(0)
    e = pl.program_id(1)
    s = pl.program_id(2)
    ns = pl.num_programs(2)
    t = h_ref.shape[0]
    bm = MOE_BLOCK
    nblk = (cnt_ref[ti, e] + (bm - 1)) // bm

    @pl.when(jnp.logical_and(e == 0, s == 0))
    def _zero():
        o_ref[...] = jnp.zeros(o_ref.shape, F32)

    def rows(b):
        return pl.ds(pl.multiple_of(b * bm, 16), bm)

    @pl.when(s == 0)
    def _gather():
        def blk(b, carry):
            tgt = (lax.broadcasted_iota(jnp.int32, (bm, t), 0) + b * bm).astype(F32)
            onehot = jnp.where(post_ref[pl.ds(e, 1), :] == tgt, 1.0, 0.0).astype(BF16)
            xg[rows(b), :] = jnp.dot(onehot, h_ref[...], preferred_element_type=F32).astype(BF16)
            return carry
        lax.fori_loop(0, nblk, blk, 0)

    def expert(b, carry):
        xb = xg[rows(b), :]
        g = jnp.dot(xb, wg_ref[0], preferred_element_type=F32)
        u = jnp.dot(xb, wu_ref[0], preferred_element_type=F32)
        a = (g * jax.nn.sigmoid(g) * u).astype(BF16)
        y = jnp.dot(a, wd_ref[0], preferred_element_type=F32)

        @pl.when(s == 0)
        def _first():
            acc[rows(b), :] = y

        @pl.when(s != 0)
        def _rest():
            acc[rows(b), :] += y
        return carry
    lax.fori_loop(0, nblk, expert, 0)

    @pl.when(s == ns - 1)
    def _scatter():
        lane = lax.broadcasted_iota(jnp.int32, (t, LANES), 1)
        mine = lane == e
        pos_e = jnp.sum(jnp.where(mine, pos_ref[...], 0.0), axis=-1, keepdims=True)
        gate_e = jnp.sum(jnp.where(mine, gate_ref[...], 0.0), axis=-1, keepdims=True)

        def blk(b, carry):
            tgt = (lax.broadcasted_iota(jnp.int32, (t, bm), 1) + b * bm).astype(F32)
            onehot = jnp.where(pos_e == tgt, 1.0, 0.0).astype(BF16)
            y = jnp.dot(onehot, acc[rows(b), :].astype(BF16), preferred_element_type=F32)
            o_ref[...] += gate_e * y
            return carry
        lax.fori_loop(0, nblk, blk, 0)

    @pl.when(jnp.logical_and(e == pl.num_programs(1) - 1, s == ns - 1))
    def _epilogue():
        y = x_ref[...] + mod_ref[0, G2:G2 + 1, :] * o_ref[...]
        if final:
            y = y * lax.rsqrt(jnp.mean(y * y, axis=-1, keepdims=True) + EPS) * fg_ref[...]
        o_ref[...] = y


def _moe(x, mod, g, final_g, w_router, w_gate, w_up, w_down, seq, final):
    n, d = x.shape
    n_e, _, f = w_gate.shape
    t = MOE_TILE
    nt = n // t
    fs = f // MOE_FSPLIT
    h, pos, post, gate, cnt = _router(x, mod, g, w_router, seq)
    counts = cnt[:, 0, :N_EXPERTS].astype(jnp.int32)
    cap = pl.cdiv(t, MOE_BLOCK) * MOE_BLOCK
    grid_spec = pltpu.PrefetchScalarGridSpec(
        num_scalar_prefetch=1,
        grid=(nt, n_e, MOE_FSPLIT),
        in_specs=[pl.BlockSpec((t, d), lambda i, e, s, c: (i, 0)),
                  pl.BlockSpec((t, LANES), lambda i, e, s, c: (i, 0)),
                  pl.BlockSpec((N_EXPERTS, t), lambda i, e, s, c: (0, i)),
                  pl.BlockSpec((t, LANES), lambda i, e, s, c: (i, 0)),
                  pl.BlockSpec((t, d), lambda i, e, s, c: (i, 0)),
                  pl.BlockSpec((1, N_MOD, d), lambda i, e, s, c: (i * t // seq, 0, 0)),
                  pl.BlockSpec((1, d), lambda i, e, s, c: (0, 0)),
                  pl.BlockSpec((1, d, fs), lambda i, e, s, c: (e, 0, s)),
                  pl.BlockSpec((1, d, fs), lambda i, e, s, c: (e, 0, s)),
                  pl.BlockSpec((1, fs, d), lambda i, e, s, c: (e, s, 0))],
        out_specs=pl.BlockSpec((t, d), lambda i, e, s, c: (i, 0)),
        scratch_shapes=[pltpu.VMEM((cap, d), BF16), pltpu.VMEM((cap, d), F32)],
    )
    return pl.pallas_call(
        functools.partial(_moe_body, final=final),
        grid_spec=grid_spec,
        out_shape=jax.ShapeDtypeStruct((n, d), F32),
        compiler_params=_params("parallel", "arbitrary", "arbitrary"),
        name="moe_mixer",
    )(counts, h, pos, post, gate, x, mod, final_g, w_gate, w_up, w_down)


def kernel(x, c, w_ada, b_ada, norm1_g, norm2_g, w_in, w_out, gmlp_vnorm_g, gmlp_ws, gmlp_bs,
           lam_q1, lam_k1, lam_q2, lam_k2, diff_subln_g, ffn_w_gate, ffn_w_up, ffn_w_down,
           w_router, moe_w_gate, moe_w_up, moe_w_down, final_g):
    batch, seq, d = x.shape
    depth = w_ada.shape[0]
    n = batch * seq
    mods = _ada(c, w_ada, b_ada)
    xf = x.reshape(n, d)
    fg = final_g.reshape(1, d)
    vcol = 2 * N_GMLP_HEADS * GMLP_HEAD_DIM + 2 * N_DIFF_HEADS * 2 * DIFF_HEAD_DIM
    for l in range(depth):
        mod = mods[l]
        z = _inproj(xf, mod, norm1_g[l].reshape(1, d), w_in[l].astype(BF16), seq)
        out_a = _gmlp(z, gmlp_vnorm_g[l], gmlp_ws[l], gmlp_bs[l])
        lam_init = 0.8 - 0.6 * math.exp(-0.3 * l)
        lam_p = jnp.stack([lam_q1[l], lam_k1[l], lam_q2[l], lam_k2[l]])
        out_b = _attn(z, z[:, vcol:].T, lam_p, diff_subln_g[l], lam_init, batch, seq)
        xf = _outproj(out_a, out_b, w_out[l].astype(BF16), xf, mod, seq)
        i = l // 2
        final = l == depth - 1
        g2 = norm2_g[l].reshape(1, d)
        if l % 2 == 0:
            xf = _ffn(xf, mod, g2, fg, ffn_w_gate[i].astype(BF16), ffn_w_up[i].astype(BF16),
                      ffn_w_down[i].astype(BF16), seq, final)
        else:
            xf = _moe(xf, mod, g2, fg, w_router[i], moe_w_gate[i].astype(BF16),
                      moe_w_up[i].astype(BF16), moe_w_down[i].astype(BF16), seq, final)
    return xf.reshape(batch, seq, d)
```

```python
import functools
import math

import jax
import jax.numpy as jnp
import numpy as np
from jax import lax
from jax.experimental import pallas as pl
from jax.experimental.pallas import tpu as pltpu

F32 = jnp.float32
BF16 = jnp.bfloat16
EPS = 1e-6
N_MOD = 6
N_GMLP_HEADS = 8
GMLP_HEAD_DIM = 64
CHUNK = 128
N_DIFF_HEADS = 4
DIFF_HEAD_DIM = 64
DIFF_V_DIM = 128
N_EXPERTS = 8
LANES = 128
VMEM_LIMIT = 56 * 1024 * 1024

MOE_TILE = 1024
MOE_BLOCKS = (192, 256, 320, 384)
MOE_FSPLIT = 2

ATTN_BLOCK = 1024
ATTN_ONES_ROWS = 16

SH1, SC1, G1, SH2, SC2, G2 = range(N_MOD)


def _params(*sem):
    return pltpu.CompilerParams(dimension_semantics=sem, vmem_limit_bytes=VMEM_LIMIT)


def _modnorm(x, g, sc, sh):
    ms = jnp.mean(x * x, axis=-1, keepdims=True)
    return (x * lax.rsqrt(ms + EPS)) * g * (1.0 + sc) + sh


def _ada_body(c_ref, w_ref, b_ref, o_ref):
    c = c_ref[...]
    sc = c * jax.nn.sigmoid(c)
    o_ref[0] = jnp.dot(sc, w_ref[0], preferred_element_type=F32,
                       precision=lax.Precision.HIGHEST) + b_ref[0]


def _ada(c, w_ada, b_ada):
    depth, d, m = w_ada.shape
    b = c.shape[0]
    rows = 8
    cp = jnp.zeros((rows, d), F32).at[:b].set(c)
    tn = 1536
    out = pl.pallas_call(
        _ada_body,
        grid=(depth, m // tn),
        in_specs=[pl.BlockSpec((rows, d), lambda l, j: (0, 0)),
                  pl.BlockSpec((1, d, tn), lambda l, j: (l, 0, j)),
                  pl.BlockSpec((1, 1, tn), lambda l, j: (l, 0, j))],
        out_specs=pl.BlockSpec((1, rows, tn), lambda l, j: (l, 0, j)),
        out_shape=jax.ShapeDtypeStruct((depth, rows, m), F32),
        compiler_params=_params("parallel", "parallel"),
        name="ada",
    )(cp, w_ada, b_ada.reshape(depth, 1, m))
    return out[:, :b].reshape(depth, b, N_MOD, d)


def _inproj_body(x_ref, mod_ref, g_ref, w_ref, z_ref, *, nchunk):
    h = _modnorm(x_ref[...], g_ref[...], mod_ref[0, SC1:SC1 + 1, :],
                 mod_ref[0, SH1:SH1 + 1, :]).astype(BF16)
    cw = w_ref.shape[1] // nchunk
    for j in range(nchunk):
        z_ref[:, j * cw:(j + 1) * cw] = jnp.dot(
            h, w_ref[:, j * cw:(j + 1) * cw], preferred_element_type=F32).astype(z_ref.dtype)


def _inproj(x, mod, g, w, seq):
    n, d = x.shape
    width = w.shape[1]
    tm = 512
    return pl.pallas_call(
        functools.partial(_inproj_body, nchunk=width // 512),
        grid=(n // tm,),
        in_specs=[pl.BlockSpec((tm, d), lambda i: (i, 0)),
                  pl.BlockSpec((1, N_MOD, d), lambda i: (i * tm // seq, 0, 0)),
                  pl.BlockSpec((1, d), lambda i: (0, 0)),
                  pl.BlockSpec((d, width), lambda i: (0, 0))],
        out_specs=pl.BlockSpec((tm, width), lambda i: (i, 0)),
        out_shape=jax.ShapeDtypeStruct((n, width), BF16),
        compiler_params=_params("parallel"),
        name="inproj",
    )(x, mod, g, w)


def _gmlp_body(u_ref, v_ref, vg_ref, w_ref, bias_ref, gm_ref, o_ref, *, nck):
    width = u_ref.shape[1]
    npair = width // LANES
    row = lax.broadcasted_iota(jnp.int32, (CHUNK, 2 * CHUNK), 0)
    col = lax.broadcasted_iota(jnp.int32, (CHUNK, 2 * CHUNK), 1)
    causal = jnp.where(col >= CHUNK, col - CHUNK, col) <= row
    lane = lax.broadcasted_iota(jnp.int32, (CHUNK, LANES), 1)
    left = lane < GMLP_HEAD_DIM
    wp = [jnp.where(causal, w_ref[p], jnp.zeros_like(w_ref[p])) for p in range(npair)]
    gm = gm_ref[...]
    for c in range(nck):
        rs = slice(c * CHUNK, (c + 1) * CHUNK)
        gu = jax.nn.gelu(u_ref[rs, :].astype(F32))
        gv = jax.nn.gelu(v_ref[rs, :].astype(F32))
        v2 = gv * gv
        hi = v2.astype(BF16)
        lo = (v2 - hi.astype(F32)).astype(BF16)
        ms = (jnp.dot(hi, gm, preferred_element_type=F32)
              + jnp.dot(lo, gm, preferred_element_type=F32))
        vn = gv * lax.rsqrt(ms + EPS) * vg_ref[...]
        outs = []
        for p in range(npair):
            vp = vn[:, p * LANES:(p + 1) * LANES]
            zero = jnp.zeros_like(vp)
            rhs = jnp.concatenate([jnp.where(left, vp, zero), jnp.where(left, zero, vp)],
                                  axis=0).astype(BF16)
            outs.append(jnp.dot(wp[p], rhs, preferred_element_type=F32))
        vm = jnp.concatenate(outs, axis=1) + bias_ref[...]
        o_ref[rs, :] = (gu * vm).astype(o_ref.dtype)


def _gmlp(z, vnorm_g, w_s, b_s):
    n = z.shape[0]
    width = N_GMLP_HEADS * GMLP_HEAD_DIM
    nck = 4
    rows = nck * CHUNK
    npair = N_GMLP_HEADS // 2
    w_pair = w_s.reshape(npair, 2, CHUNK, CHUNK).transpose(0, 2, 1, 3).reshape(
        npair, CHUNK, 2 * CHUNK).astype(BF16)
    bias = jnp.repeat(b_s.T, GMLP_HEAD_DIM, axis=1)
    grp = np.arange(width) // GMLP_HEAD_DIM
    group_mean = jnp.asarray((grp[:, None] == grp[None, :]) / GMLP_HEAD_DIM, dtype=BF16)
    return pl.pallas_call(
        functools.partial(_gmlp_body, nck=nck),
        grid=(n // rows,),
        in_specs=[pl.BlockSpec((rows, width), lambda i: (i, 0)),
                  pl.BlockSpec((rows, width), lambda i: (i, 1)),
                  pl.BlockSpec((1, width), lambda i: (0, 0)),
                  pl.BlockSpec((npair, CHUNK, 2 * CHUNK), lambda i: (0, 0, 0)),
                  pl.BlockSpec((CHUNK, width), lambda i: (0, 0)),
                  pl.BlockSpec((width, width), lambda i: (0, 0))],
        out_specs=pl.BlockSpec((rows, width), lambda i: (i, 0)),
        out_shape=jax.ShapeDtypeStruct((n, width), BF16),
        compiler_params=_params("parallel"),
        name="gmlp",
    )(z, z, vnorm_g.reshape(1, width), w_pair, bias, group_mean)


def _attn_body(qi_ref, ki_ref, q_ref, k_ref, vt_ref, lam_ref, g_ref, o_ref,
               qs1, qs2, m1, m2, acc1, acc2, *, tq, lam_init):
    qi = qi_ref[pl.program_id(2)]
    ki = ki_ref[pl.program_id(2)]
    maps = ((qs1, m1, acc1), (qs2, m2, acc2))
    dv = DIFF_V_DIM

    @pl.when(ki == 0)
    def _init():
        qsc = (q_ref[...].astype(F32) * (DIFF_HEAD_DIM ** -0.5 * math.log2(math.e))).astype(BF16)
        lane = lax.broadcasted_iota(jnp.int32, qsc.shape, 1)
        zero = jnp.zeros_like(qsc)
        qs1[...] = jnp.where(lane < DIFF_HEAD_DIM, qsc, zero)
        qs2[...] = jnp.where(lane < DIFF_HEAD_DIM, zero, qsc)
        for _, m_s, acc in maps:
            m_s[...] = jnp.full(m_s.shape, -jnp.inf, F32)
            acc[...] = jnp.zeros(acc.shape, F32)

    def step(masked):
        k = k_ref[...]
        vt = vt_ref[...]
        sts = [lax.dot_general(k, qs[...], (((1,), (1,)), ((), ())), preferred_element_type=F32)
               for qs, _, _ in maps]
        for st, (_, m_s, acc) in zip(sts, maps):
            if masked:
                key = lax.broadcasted_iota(jnp.int32, st.shape, 0)
                qry = lax.broadcasted_iota(jnp.int32, st.shape, 1)
                st = jnp.where(key <= qry, st, -jnp.inf)
            sb = st.astype(BF16)
            m_prev = m_s[...]
            m_new = jnp.maximum(m_prev, jnp.max(sb, axis=0, keepdims=True).astype(F32))
            alpha = jnp.exp2(m_prev - m_new)
            pt = jnp.exp2(sb - m_new.astype(BF16))
            acc[...] = alpha * acc[...] + jnp.dot(vt, pt, preferred_element_type=F32)
            m_s[...] = m_new

    @pl.when(ki < qi)
    def _full():
        step(False)

    @pl.when(ki == qi)
    def _diag():
        step(True)
        lp = lam_ref[...]
        s1 = jnp.sum(lp[0:1, :] * lp[1:2, :], axis=-1, keepdims=True)
        s2 = jnp.sum(lp[2:3, :] * lp[3:4, :], axis=-1, keepdims=True)
        lam = jnp.exp(s1) - jnp.exp(s2) + lam_init
        ot = (acc1[0:dv, :] / acc1[dv:dv + 1, :]
              - lam * (acc2[0:dv, :] / acc2[dv:dv + 1, :]))
        ot = ot * lax.rsqrt(jnp.mean(ot * ot, axis=0, keepdims=True) + EPS)
        o_ref[...] = (ot.T * g_ref[...] * (1.0 - lam_init)).astype(o_ref.dtype)


def _attn(z, lam_p, subln_g, lam_init, batch, seq):
    n = z.shape[0]
    tq = ATTN_BLOCK
    dve = DIFF_V_DIM + ATTN_ONES_ROWS
    vcol = z.shape[1] - N_DIFF_HEADS * DIFF_V_DIM
    vt = z[:, vcol:].T.reshape(N_DIFF_HEADS, DIFF_V_DIM, n)
    vt = jnp.concatenate([vt, jnp.ones((N_DIFF_HEADS, ATTN_ONES_ROWS, n), BF16)], axis=1)
    vt = vt.reshape(N_DIFF_HEADS * dve, n)
    nq = seq // tq
    qcol = 2 * N_GMLP_HEADS * GMLP_HEAD_DIM // LANES
    kcol = qcol + N_DIFF_HEADS
    pairs = [(i, j) for i in range(nq) for j in range(i + 1)]
    qi_tab = jnp.asarray([p[0] for p in pairs], jnp.int32)
    ki_tab = jnp.asarray([p[1] for p in pairs], jnp.int32)
    grid_spec = pltpu.PrefetchScalarGridSpec(
        num_scalar_prefetch=2,
        grid=(batch, N_DIFF_HEADS, len(pairs)),
        in_specs=[
            pl.BlockSpec((tq, LANES), lambda b, h, p, qt, kt: (b * nq + qt[p], qcol + h)),
            pl.BlockSpec((tq, LANES), lambda b, h, p, qt, kt: (b * nq + kt[p], kcol + h)),
            pl.BlockSpec((dve, tq), lambda b, h, p, qt, kt: (h, b * nq + kt[p])),
            pl.BlockSpec((4, DIFF_HEAD_DIM), lambda b, h, p, qt, kt: (0, 0)),
            pl.BlockSpec((1, DIFF_V_DIM), lambda b, h, p, qt, kt: (0, 0)),
        ],
        out_specs=pl.BlockSpec((tq, LANES), lambda b, h, p, qt, kt: (b * nq + qt[p], h)),
        scratch_shapes=(2 * [pltpu.VMEM((tq, LANES), BF16)] + 2 * [pltpu.VMEM((1, tq), F32)]
                        + 2 * [pltpu.VMEM((dve, tq), F32)]),
    )
    return pl.pallas_call(
        functools.partial(_attn_body, tq=tq, lam_init=lam_init),
        grid_spec=grid_spec,
        out_shape=jax.ShapeDtypeStruct((n, N_DIFF_HEADS * DIFF_V_DIM), BF16),
        compiler_params=_params("parallel", "parallel", "arbitrary"),
        name="diff_attn",
    )(qi_tab, ki_tab, z, z, vt, lam_p, subln_g.reshape(1, DIFF_V_DIM))


def _outproj_body(a_ref, b_ref, w_ref, x_ref, mod_ref, o_ref):
    ka = a_ref.shape[1]
    y = jnp.dot(a_ref[...], w_ref[0:ka, :], preferred_element_type=F32)
    y = y + jnp.dot(b_ref[...], w_ref[ka:, :], preferred_element_type=F32)
    o_ref[...] = x_ref[...] + mod_ref[0, G1:G1 + 1, :] * y


def _outproj(out_a, out_b, w, x, mod, seq):
    n, d = x.shape
    tm = 512
    ka, kb = out_a.shape[1], out_b.shape[1]
    return pl.pallas_call(
        _outproj_body,
        grid=(n // tm,),
        in_specs=[pl.BlockSpec((tm, ka), lambda i: (i, 0)),
                  pl.BlockSpec((tm, kb), lambda i: (i, 0)),
                  pl.BlockSpec((ka + kb, d), lambda i: (0, 0)),
                  pl.BlockSpec((tm, d), lambda i: (i, 0)),
                  pl.BlockSpec((1, N_MOD, d), lambda i: (i * tm // seq, 0, 0))],
        out_specs=pl.BlockSpec((tm, d), lambda i: (i, 0)),
        out_shape=jax.ShapeDtypeStruct((n, d), F32),
        compiler_params=_params("parallel"),
        name="outproj",
    )(out_a, out_b, w, x, mod)


def _ffn_body(x_ref, mod_ref, g_ref, fg_ref, wg_ref, wu_ref, wd_ref, o_ref, h_s, acc, *, final):
    j = pl.program_id(1)

    @pl.when(j == 0)
    def _prologue():
        h = _modnorm(x_ref[...], g_ref[...], mod_ref[0, SC2:SC2 + 1, :], mod_ref[0, SH2:SH2 + 1, :])
        h_s[...] = h.astype(BF16)
        acc[...] = jnp.zeros(acc.shape, F32)

    h = h_s[...]
    g = jnp.dot(h, wg_ref[...].astype(BF16), preferred_element_type=F32)
    u = jnp.dot(h, wu_ref[...].astype(BF16), preferred_element_type=F32)
    a = (g * jax.nn.sigmoid(g) * u).astype(BF16)
    acc[...] += jnp.dot(a, wd_ref[...].astype(BF16), preferred_element_type=F32)

    @pl.when(j == pl.num_programs(1) - 1)
    def _epilogue():
        y = x_ref[...] + mod_ref[0, G2:G2 + 1, :] * acc[...]
        if final:
            y = y * lax.rsqrt(jnp.mean(y * y, axis=-1, keepdims=True) + EPS) * fg_ref[...]
        o_ref[...] = y


def _ffn(x, mod, g, final_g, w_gate, w_up, w_down, seq, final):
    n, d = x.shape
    f = w_gate.shape[1]
    tm = 1024
    tf = 256
    return pl.pallas_call(
        functools.partial(_ffn_body, final=final),
        grid=(n // tm, f // tf),
        in_specs=[pl.BlockSpec((tm, d), lambda i, j: (i, 0)),
                  pl.BlockSpec((1, N_MOD, d), lambda i, j: (i * tm // seq, 0, 0)),
                  pl.BlockSpec((1, d), lambda i, j: (0, 0)),
                  pl.BlockSpec((1, d), lambda i, j: (0, 0)),
                  pl.BlockSpec((d, tf), lambda i, j: (0, j)),
                  pl.BlockSpec((d, tf), lambda i, j: (0, j)),
                  pl.BlockSpec((tf, d), lambda i, j: (j, 0))],
        out_specs=pl.BlockSpec((tm, d), lambda i, j: (i, 0)),
        out_shape=jax.ShapeDtypeStruct((n, d), F32),
        scratch_shapes=[pltpu.VMEM((tm, d), BF16), pltpu.VMEM((tm, d), F32)],
        compiler_params=_params("parallel", "arbitrary"),
        name="ffn_mixer",
    )(x, mod, g, final_g, w_gate, w_up, w_down)


def _top2(logits):
    lane = lax.broadcasted_iota(jnp.int32, logits.shape, 1)
    lg = jnp.where(lane < N_EXPERTS, logits, -jnp.inf)
    m1 = jnp.max(lg, axis=-1, keepdims=True)
    i1 = jnp.min(jnp.where(lg == m1, lane, LANES), axis=-1, keepdims=True)
    lg2 = jnp.where(lane == i1, -jnp.inf, lg)
    m2 = jnp.max(lg2, axis=-1, keepdims=True)
    i2 = jnp.min(jnp.where(lg2 == m2, lane, LANES), axis=-1, keepdims=True)
    e2 = jnp.exp(m2 - m1)
    den = 1.0 + e2
    return i1, i2, 1.0 / den, e2 / den


def _router_body(x_ref, mod_ref, g_ref, wr_ref, h_ref, pos_ref, post_ref, gate_ref, cnt_ref):
    t = x_ref.shape[0]
    h = _modnorm(x_ref[...], g_ref[...], mod_ref[0, SC2:SC2 + 1, :], mod_ref[0, SH2:SH2 + 1, :])
    h_ref[...] = h.astype(BF16)
    logits = jnp.dot(h, wr_ref[...], preferred_element_type=F32, precision=lax.Precision.HIGHEST)
    i1, i2, w1, w2 = _top2(logits)
    lane = lax.broadcasted_iota(jnp.int32, logits.shape, 1)
    sel1 = lane == i1
    sel2 = lane == i2
    gate_ref[...] = jnp.where(sel1, w1, 0.0) + jnp.where(sel2, w2, 0.0)
    picked = jnp.logical_or(sel1, sel2)
    onehot = jnp.where(picked, 1.0, 0.0)
    row = lax.broadcasted_iota(jnp.int32, (t, t), 0)
    col = lax.broadcasted_iota(jnp.int32, (t, t), 1)
    before = jnp.where(col < row, 1.0, 0.0).astype(BF16)
    rank = jnp.dot(before, onehot.astype(BF16), preferred_element_type=F32)
    pos = jnp.where(picked, rank, -1.0)
    pos_ref[...] = pos
    post_ref[...] = pos.T[0:N_EXPERTS, :]
    cnt_ref[0] = jnp.broadcast_to(jnp.sum(onehot, axis=0, keepdims=True), cnt_ref.shape[1:])


def _router(x, mod, g, w_router, seq):
    n, d = x.shape
    t = MOE_TILE
    nt = n // t
    wr = jnp.zeros((d, LANES), F32).at[:, :N_EXPERTS].set(w_router)
    return pl.pallas_call(
        _router_body,
        grid=(nt,),
        in_specs=[pl.BlockSpec((t, d), lambda i: (i, 0)),
                  pl.BlockSpec((1, N_MOD, d), lambda i: (i * t // seq, 0, 0)),
                  pl.BlockSpec((1, d), lambda i: (0, 0)),
                  pl.BlockSpec((d, LANES), lambda i: (0, 0))],
        out_specs=[pl.BlockSpec((t, d), lambda i: (i, 0)),
                   pl.BlockSpec((t, LANES), lambda i: (i, 0)),
                   pl.BlockSpec((N_EXPERTS, t), lambda i: (0, i)),
                   pl.BlockSpec((t, LANES), lambda i: (i, 0)),
                   pl.BlockSpec((1, 8, LANES), lambda i: (i, 0, 0))],
        out_shape=[jax.ShapeDtypeStruct((n, d), BF16),
                   jax.ShapeDtypeStruct((n, LANES), F32),
                   jax.ShapeDtypeStruct((N_EXPERTS, n), F32),
                   jax.ShapeDtypeStruct((n, LANES), F32),
                   jax.ShapeDtypeStruct((nt, 8, LANES), F32)],
        compiler_params=_params("parallel"),
        name="router",
    )(x, mod, g, wr)


def _moe_body(cnt_ref, h_ref, pos_ref, post_ref, gate_ref, x_ref, mod_ref, fg_ref,
              wg_ref, wu_ref, wd_ref, o_ref, xg, acc, *, final):
    ti = pl.program_id(0)
    e = pl.program_id(1)
    s = pl.program_id(2)
    ns = pl.num_programs(2)
    t = h_ref.shape[0]
    cnt = cnt_ref[ti, e]

    @pl.when(jnp.logical_and(e == 0, s == 0))
    def _zero():
        o_ref[...] = jnp.zeros(o_ref.shape, F32)

    def block(b, bm):
        if isinstance(b, int):
            rows = slice(b * bm, (b + 1) * bm)
        else:
            rows = pl.ds(pl.multiple_of(b * bm, 16), bm)

        @pl.when(s == 0)
        def _gather():
            tgt = (lax.broadcasted_iota(jnp.int32, (bm, t), 0) + b * bm).astype(F32)
            onehot = jnp.where(post_ref[pl.ds(e, 1), :] == tgt, 1.0, 0.0).astype(BF16)
            xg[rows, :] = jnp.dot(onehot, h_ref[...], preferred_element_type=F32).astype(BF16)

        xb = xg[rows, :]
        g = jnp.dot(xb, wg_ref[0], preferred_element_type=F32)
        u = jnp.dot(xb, wu_ref[0], preferred_element_type=F32)
        a = (g * jax.nn.sigmoid(g) * u).astype(BF16)
        y = jnp.dot(a, wd_ref[0], preferred_element_type=F32)

        @pl.when(s == 0)
        def _first():
            acc[rows, :] = y

        @pl.when(s != 0)
        def _rest():
            acc[rows, :] += y

        @pl.when(s == ns - 1)
        def _scatter():
            lane = lax.broadcasted_iota(jnp.int32, (t, LANES), 1)
            mine = lane == e
            pos_e = jnp.sum(jnp.where(mine, pos_ref[...], 0.0), axis=-1, keepdims=True)
            gate_e = jnp.sum(jnp.where(mine, gate_ref[...], 0.0), axis=-1, keepdims=True)
            tgt = (lax.broadcasted_iota(jnp.int32, (t, bm), 1) + b * bm).astype(F32)
            onehot = jnp.where(pos_e == tgt, 1.0, 0.0).astype(BF16)
            yb = jnp.dot(onehot, acc[rows, :].astype(BF16), preferred_element_type=F32)
            o_ref[...] += gate_e * yb

    lo = 0
    for bm in MOE_BLOCKS[:-1]:
        @pl.when(jnp.logical_and(cnt > lo, cnt <= bm))
        def _single(bm=bm):
            block(0, bm)
        lo = bm

    @pl.when(cnt > lo)
    def _multi():
        bm = MOE_BLOCKS[-1]

        def body(b, carry):
            block(b, bm)
            return carry
        lax.fori_loop(0, (cnt + (bm - 1)) // bm, body, 0)

    @pl.when(jnp.logical_and(e == pl.num_programs(1) - 1, s == ns - 1))
    def _epilogue():
        y = x_ref[...] + mod_ref[0, G2:G2 + 1, :] * o_ref[...]
        if final:
            y = y * lax.rsqrt(jnp.mean(y * y, axis=-1, keepdims=True) + EPS) * fg_ref[...]
        o_ref[...] = y


def _moe(x, mod, g, final_g, w_router, w_gate, w_up, w_down, seq, final):
    n, d = x.shape
    n_e, _, f = w_gate.shape
    t = MOE_TILE
    nt = n // t
    fs = f // MOE_FSPLIT
    h, pos, post, gate, cnt = _router(x, mod, g, w_router, seq)
    counts = cnt[:, 0, :N_EXPERTS].astype(jnp.int32)
    cap = pl.cdiv(t, MOE_BLOCKS[-1]) * MOE_BLOCKS[-1]
    grid_spec = pltpu.PrefetchScalarGridSpec(
        num_scalar_prefetch=1,
        grid=(nt, n_e, MOE_FSPLIT),
        in_specs=[pl.BlockSpec((t, d), lambda i, e, s, c: (i, 0)),
                  pl.BlockSpec((t, LANES), lambda i, e, s, c: (i, 0)),
                  pl.BlockSpec((N_EXPERTS, t), lambda i, e, s, c: (0, i)),
                  pl.BlockSpec((t, LANES), lambda i, e, s, c: (i, 0)),
                  pl.BlockSpec((t, d), lambda i, e, s, c: (i, 0)),
                  pl.BlockSpec((1, N_MOD, d), lambda i, e, s, c: (i * t // seq, 0, 0)),
                  pl.BlockSpec((1, d), lambda i, e, s, c: (0, 0)),
                  pl.BlockSpec((1, d, fs), lambda i, e, s, c: (e, 0, s)),
                  pl.BlockSpec((1, d, fs), lambda i, e, s, c: (e, 0, s)),
                  pl.BlockSpec((1, fs, d), lambda i, e, s, c: (e, s, 0))],
        out_specs=pl.BlockSpec((t, d), lambda i, e, s, c: (i, 0)),
        scratch_shapes=[pltpu.VMEM((cap, d), BF16), pltpu.VMEM((cap, d), F32)],
    )
    return pl.pallas_call(
        functools.partial(_moe_body, final=final),
        grid_spec=grid_spec,
        out_shape=jax.ShapeDtypeStruct((n, d), F32),
        compiler_params=_params("parallel", "arbitrary", "arbitrary"),
        name="moe_mixer",
    )(counts, h, pos, post, gate, x, mod, final_g, w_gate, w_up, w_down)


def kernel(x, c, w_ada, b_ada, norm1_g, norm2_g, w_in, w_out, gmlp_vnorm_g, gmlp_ws, gmlp_bs,
           lam_q1, lam_k1, lam_q2, lam_k2, diff_subln_g, ffn_w_gate, ffn_w_up, ffn_w_down,
           w_router, moe_w_gate, moe_w_up, moe_w_down, final_g):
    batch, seq, d = x.shape
    depth = w_ada.shape[0]
    n = batch * seq
    mods = _ada(c, w_ada, b_ada)
    xf = x.reshape(n, d)
    fg = final_g.reshape(1, d)
    for l in range(depth):
        mod = mods[l]
        z = _inproj(xf, mod, norm1_g[l].reshape(1, d), w_in[l].astype(BF16), seq)
        out_a = _gmlp(z, gmlp_vnorm_g[l], gmlp_ws[l], gmlp_bs[l])
        lam_init = 0.8 - 0.6 * math.exp(-0.3 * l)
        lam_p = jnp.stack([lam_q1[l], lam_k1[l], lam_q2[l], lam_k2[l]])
        out_b = _attn(z, lam_p, diff_subln_g[l], lam_init, batch, seq)
        xf = _outproj(out_a, out_b, w_out[l].astype(BF16), xf, mod, seq)
        i = l // 2
        final = l == depth - 1
        g2 = norm2_g[l].reshape(1, d)
        if l % 2 == 0:
            xf = _ffn(xf, mod, g2, fg, ffn_w_gate[i], ffn_w_up[i], ffn_w_down[i], seq, final)
        else:
            xf = _moe(xf, mod, g2, fg, w_router[i], moe_w_gate[i].astype(BF16),
                      moe_w_up[i].astype(BF16), moe_w_down[i].astype(BF16), seq, final)
    return xf.reshape(batch, seq, d)
```

```python
import functools
import math

import jax
import jax.numpy as jnp
import numpy as np
from jax import lax
from jax.experimental import pallas as pl
from jax.experimental.pallas import tpu as pltpu

F32 = jnp.float32
BF16 = jnp.bfloat16
EPS = 1e-6
N_MOD = 6
N_GMLP_HEADS = 8
GMLP_HEAD_DIM = 64
CHUNK = 128
N_DIFF_HEADS = 4
DIFF_HEAD_DIM = 64
DIFF_V_DIM = 128
N_EXPERTS = 8
LANES = 128
VMEM_LIMIT = 56 * 1024 * 1024

MOE_TILE = 1024
MOE_BLOCKS = tuple(range(128, 704, 64))
MOE_FSPLIT = 2

ATTN_BLOCK = 1024
ATTN_ONES_ROWS = 16

SH1, SC1, G1, SH2, SC2, G2 = range(N_MOD)


def _params(*sem):
    return pltpu.CompilerParams(dimension_semantics=sem, vmem_limit_bytes=VMEM_LIMIT)


def _modnorm(x, g, sc, sh):
    ms = jnp.mean(x * x, axis=-1, keepdims=True)
    return (x * lax.rsqrt(ms + EPS)) * g * (1.0 + sc) + sh


def _ada_body(c_ref, w_ref, b_ref, o_ref):
    c = c_ref[...]
    sc = c * jax.nn.sigmoid(c)
    o_ref[0] = jnp.dot(sc, w_ref[0], preferred_element_type=F32,
                       precision=lax.Precision.HIGHEST) + b_ref[0]


def _ada(c, w_ada, b_ada):
    depth, d, m = w_ada.shape
    b = c.shape[0]
    rows = 8
    cp = jnp.zeros((rows, d), F32).at[:b].set(c)
    tn = 1536
    out = pl.pallas_call(
        _ada_body,
        grid=(depth, m // tn),
        in_specs=[pl.BlockSpec((rows, d), lambda l, j: (0, 0)),
                  pl.BlockSpec((1, d, tn), lambda l, j: (l, 0, j)),
                  pl.BlockSpec((1, 1, tn), lambda l, j: (l, 0, j))],
        out_specs=pl.BlockSpec((1, rows, tn), lambda l, j: (l, 0, j)),
        out_shape=jax.ShapeDtypeStruct((depth, rows, m), F32),
        compiler_params=_params("parallel", "parallel"),
        name="ada",
    )(cp, w_ada, b_ada.reshape(depth, 1, m))
    return out[:, :b].reshape(depth, b, N_MOD, d)


def _inproj_body(x_ref, mod_ref, g_ref, w_ref, z_ref, vt_ref):
    h = _modnorm(x_ref[...], g_ref[...], mod_ref[0, SC1:SC1 + 1, :],
                 mod_ref[0, SH1:SH1 + 1, :]).astype(BF16)
    tm = x_ref.shape[0]
    zw = z_ref.shape[1]
    cw = 512
    for c0 in range(0, zw, cw):
        z_ref[:, c0:c0 + cw] = jnp.dot(
            h, w_ref[:, c0:c0 + cw], preferred_element_type=F32).astype(z_ref.dtype)
    dve = DIFF_V_DIM + ATTN_ONES_ROWS
    for hd in range(N_DIFF_HEADS):
        c0 = zw + hd * DIFF_V_DIM
        v = jnp.dot(h, w_ref[:, c0:c0 + DIFF_V_DIM], preferred_element_type=F32)
        vt_ref[hd * dve:hd * dve + DIFF_V_DIM, :] = v.T.astype(vt_ref.dtype)
        vt_ref[hd * dve + DIFF_V_DIM:(hd + 1) * dve, :] = jnp.ones((ATTN_ONES_ROWS, tm), vt_ref.dtype)


def _inproj(x, mod, g, w, seq):
    n, d = x.shape
    width = w.shape[1]
    zw = width - N_DIFF_HEADS * DIFF_V_DIM
    dve = DIFF_V_DIM + ATTN_ONES_ROWS
    tm = 512
    return pl.pallas_call(
        _inproj_body,
        grid=(n // tm,),
        in_specs=[pl.BlockSpec((tm, d), lambda i: (i, 0)),
                  pl.BlockSpec((1, N_MOD, d), lambda i: (i * tm // seq, 0, 0)),
                  pl.BlockSpec((1, d), lambda i: (0, 0)),
                  pl.BlockSpec((d, width), lambda i: (0, 0))],
        out_specs=[pl.BlockSpec((tm, zw), lambda i: (i, 0)),
                   pl.BlockSpec((N_DIFF_HEADS * dve, tm), lambda i: (0, i))],
        out_shape=[jax.ShapeDtypeStruct((n, zw), BF16),
                   jax.ShapeDtypeStruct((N_DIFF_HEADS * dve, n), BF16)],
        compiler_params=_params("parallel"),
        name="inproj",
    )(x, mod, g, w)


def _gmlp_body(u_ref, v_ref, vg_ref, w_ref, bias_ref, gm_ref, o_ref, *, nck):
    width = u_ref.shape[1]
    npair = width // LANES
    row = lax.broadcasted_iota(jnp.int32, (CHUNK, 2 * CHUNK), 0)
    col = lax.broadcasted_iota(jnp.int32, (CHUNK, 2 * CHUNK), 1)
    causal = jnp.where(col >= CHUNK, col - CHUNK, col) <= row
    lane = lax.broadcasted_iota(jnp.int32, (CHUNK, LANES), 1)
    left = lane < GMLP_HEAD_DIM
    wp = [jnp.where(causal, w_ref[p], jnp.zeros_like(w_ref[p])) for p in range(npair)]
    gm = gm_ref[...]
    for c in range(nck):
        rs = slice(c * CHUNK, (c + 1) * CHUNK)
        gu = jax.nn.gelu(u_ref[rs, :].astype(F32))
        gv = jax.nn.gelu(v_ref[rs, :].astype(F32))
        v2 = gv * gv
        hi = v2.astype(BF16)
        lo = (v2 - hi.astype(F32)).astype(BF16)
        ms = (jnp.dot(hi, gm, preferred_element_type=F32)
              + jnp.dot(lo, gm, preferred_element_type=F32))
        vn = gv * lax.rsqrt(ms + EPS) * vg_ref[...]
        outs = []
        for p in range(npair):
            vp = vn[:, p * LANES:(p + 1) * LANES]
            zero = jnp.zeros_like(vp)
            rhs = jnp.concatenate([jnp.where(left, vp, zero), jnp.where(left, zero, vp)],
                                  axis=0).astype(BF16)
            outs.append(jnp.dot(wp[p], rhs, preferred_element_type=F32))
        vm = jnp.concatenate(outs, axis=1) + bias_ref[...]
        o_ref[rs, :] = (gu * vm).astype(o_ref.dtype)


def _gmlp(z, vnorm_g, w_s, b_s):
    n = z.shape[0]
    width = N_GMLP_HEADS * GMLP_HEAD_DIM
    nck = 4
    rows = nck * CHUNK
    npair = N_GMLP_HEADS // 2
    w_pair = w_s.reshape(npair, 2, CHUNK, CHUNK).transpose(0, 2, 1, 3).reshape(
        npair, CHUNK, 2 * CHUNK).astype(BF16)
    bias = jnp.repeat(b_s.T, GMLP_HEAD_DIM, axis=1)
    grp = np.arange(width) // GMLP_HEAD_DIM
    group_mean = jnp.asarray((grp[:, None] == grp[None, :]) / GMLP_HEAD_DIM, dtype=BF16)
    return pl.pallas_call(
        functools.partial(_gmlp_body, nck=nck),
        grid=(n // rows,),
        in_specs=[pl.BlockSpec((rows, width), lambda i: (i, 0)),
                  pl.BlockSpec((rows, width), lambda i: (i, 1)),
                  pl.BlockSpec((1, width), lambda i: (0, 0)),
                  pl.BlockSpec((npair, CHUNK, 2 * CHUNK), lambda i: (0, 0, 0)),
                  pl.BlockSpec((CHUNK, width), lambda i: (0, 0)),
                  pl.BlockSpec((width, width), lambda i: (0, 0))],
        out_specs=pl.BlockSpec((rows, width), lambda i: (i, 0)),
        out_shape=jax.ShapeDtypeStruct((n, width), BF16),
        compiler_params=_params("parallel"),
        name="gmlp",
    )(z, z, vnorm_g.reshape(1, width), w_pair, bias, group_mean)


def _attn_body(qi_ref, ki_ref, q_ref, k_ref, vt_ref, lam_ref, g_ref, o_ref,
               qs1, qs2, m1, m2, acc1, acc2, *, tq, lam_init):
    qi = qi_ref[pl.program_id(2)]
    ki = ki_ref[pl.program_id(2)]
    maps = ((qs1, m1, acc1), (qs2, m2, acc2))
    dv = DIFF_V_DIM

    @pl.when(ki == 0)
    def _init():
        qsc = (q_ref[...].astype(F32) * (DIFF_HEAD_DIM ** -0.5 * math.log2(math.e))).astype(BF16)
        lane = lax.broadcasted_iota(jnp.int32, qsc.shape, 1)
        zero = jnp.zeros_like(qsc)
        qs1[...] = jnp.where(lane < DIFF_HEAD_DIM, qsc, zero)
        qs2[...] = jnp.where(lane < DIFF_HEAD_DIM, zero, qsc)
        for _, m_s, acc in maps:
            m_s[...] = jnp.full(m_s.shape, -jnp.inf, F32)
            acc[...] = jnp.zeros(acc.shape, F32)

    def step(masked):
        k = k_ref[...]
        vt = vt_ref[...]
        sts = [lax.dot_general(k, qs[...], (((1,), (1,)), ((), ())), preferred_element_type=F32)
               for qs, _, _ in maps]
        for st, (_, m_s, acc) in zip(sts, maps):
            if masked:
                key = lax.broadcasted_iota(jnp.int32, st.shape, 0)
                qry = lax.broadcasted_iota(jnp.int32, st.shape, 1)
                st = jnp.where(key <= qry, st, -jnp.inf)
            sb = st.astype(BF16)
            m_prev = m_s[...]
            m_new = jnp.maximum(m_prev, jnp.max(sb, axis=0, keepdims=True).astype(F32))
            alpha = jnp.exp2(m_prev - m_new)
            pt = jnp.exp2(sb - m_new.astype(BF16))
            acc[...] = alpha * acc[...] + jnp.dot(vt, pt, preferred_element_type=F32)
            m_s[...] = m_new

    @pl.when(ki < qi)
    def _full():
        step(False)

    @pl.when(ki == qi)
    def _diag():
        step(True)
        lp = lam_ref[...]
        s1 = jnp.sum(lp[0:1, :] * lp[1:2, :], axis=-1, keepdims=True)
        s2 = jnp.sum(lp[2:3, :] * lp[3:4, :], axis=-1, keepdims=True)
        lam = jnp.exp(s1) - jnp.exp(s2) + lam_init
        ot = (acc1[0:dv, :] / acc1[dv:dv + 1, :]
              - lam * (acc2[0:dv, :] / acc2[dv:dv + 1, :]))
        ot = ot * lax.rsqrt(jnp.mean(ot * ot, axis=0, keepdims=True) + EPS)
        o_ref[...] = (ot.T * g_ref[...] * (1.0 - lam_init)).astype(o_ref.dtype)


def _attn(z, vt, lam_p, subln_g, lam_init, batch, seq):
    n = z.shape[0]
    tq = ATTN_BLOCK
    dve = DIFF_V_DIM + ATTN_ONES_ROWS
    nq = seq // tq
    qcol = 2 * N_GMLP_HEADS * GMLP_HEAD_DIM // LANES
    kcol = qcol + N_DIFF_HEADS
    pairs = [(i, j) for i in range(nq) for j in range(i + 1)]
    qi_tab = jnp.asarray([p[0] for p in pairs], jnp.int32)
    ki_tab = jnp.asarray([p[1] for p in pairs], jnp.int32)
    grid_spec = pltpu.PrefetchScalarGridSpec(
        num_scalar_prefetch=2,
        grid=(batch, N_DIFF_HEADS, len(pairs)),
        in_specs=[
            pl.BlockSpec((tq, LANES), lambda b, h, p, qt, kt: (b * nq + qt[p], qcol + h)),
            pl.BlockSpec((tq, LANES), lambda b, h, p, qt, kt: (b * nq + kt[p], kcol + h)),
            pl.BlockSpec((dve, tq), lambda b, h, p, qt, kt: (h, b * nq + kt[p])),
            pl.BlockSpec((4, DIFF_HEAD_DIM), lambda b, h, p, qt, kt: (0, 0)),
            pl.BlockSpec((1, DIFF_V_DIM), lambda b, h, p, qt, kt: (0, 0)),
        ],
        out_specs=pl.BlockSpec((tq, LANES), lambda b, h, p, qt, kt: (b * nq + qt[p], h)),
        scratch_shapes=(2 * [pltpu.VMEM((tq, LANES), BF16)] + 2 * [pltpu.VMEM((1, tq), F32)]
                        + 2 * [pltpu.VMEM((dve, tq), F32)]),
    )
    return pl.pallas_call(
        functools.partial(_attn_body, tq=tq, lam_init=lam_init),
        grid_spec=grid_spec,
        out_shape=jax.ShapeDtypeStruct((n, N_DIFF_HEADS * DIFF_V_DIM), BF16),
        compiler_params=_params("parallel", "parallel", "arbitrary"),
        name="diff_attn",
    )(qi_tab, ki_tab, z, z, vt, lam_p, subln_g.reshape(1, DIFF_V_DIM))


def _outproj_body(a_ref, b_ref, w_ref, x_ref, mod_ref, o_ref):
    ka = a_ref.shape[1]
    y = jnp.dot(a_ref[...], w_ref[0:ka, :], preferred_element_type=F32)
    y = y + jnp.dot(b_ref[...], w_ref[ka:, :], preferred_element_type=F32)
    o_ref[...] = x_ref[...] + mod_ref[0, G1:G1 + 1, :] * y


def _outproj(out_a, out_b, w, x, mod, seq):
    n, d = x.shape
    tm = 512
    ka, kb = out_a.shape[1], out_b.shape[1]
    return pl.pallas_call(
        _outproj_body,
        grid=(n // tm,),
        in_specs=[pl.BlockSpec((tm, ka), lambda i: (i, 0)),
                  pl.BlockSpec((tm, kb), lambda i: (i, 0)),
                  pl.BlockSpec((ka + kb, d), lambda i: (0, 0)),
                  pl.BlockSpec((tm, d), lambda i: (i, 0)),
                  pl.BlockSpec((1, N_MOD, d), lambda i: (i * tm // seq, 0, 0))],
        out_specs=pl.BlockSpec((tm, d), lambda i: (i, 0)),
        out_shape=jax.ShapeDtypeStruct((n, d), F32),
        compiler_params=_params("parallel"),
        name="outproj",
    )(out_a, out_b, w, x, mod)


def _ffn_body(x_ref, mod_ref, g_ref, fg_ref, wg_ref, wu_ref, wd_ref, o_ref, h_s, acc, *, final):
    j = pl.program_id(1)

    @pl.when(j == 0)
    def _prologue():
        h = _modnorm(x_ref[...], g_ref[...], mod_ref[0, SC2:SC2 + 1, :], mod_ref[0, SH2:SH2 + 1, :])
        h_s[...] = h.astype(BF16)
        acc[...] = jnp.zeros(acc.shape, F32)

    h = h_s[...]
    g = jnp.dot(h, wg_ref[...].astype(BF16), preferred_element_type=F32)
    u = jnp.dot(h, wu_ref[...].astype(BF16), preferred_element_type=F32)
    a = (g * jax.nn.sigmoid(g) * u).astype(BF16)
    acc[...] += jnp.dot(a, wd_ref[...].astype(BF16), preferred_element_type=F32)

    @pl.when(j == pl.num_programs(1) - 1)
    def _epilogue():
        y = x_ref[...] + mod_ref[0, G2:G2 + 1, :] * acc[...]
        if final:
            y = y * lax.rsqrt(jnp.mean(y * y, axis=-1, keepdims=True) + EPS) * fg_ref[...]
        o_ref[...] = y


def _ffn(x, mod, g, final_g, w_gate, w_up, w_down, seq, final):
    n, d = x.shape
    f = w_gate.shape[1]
    tm = 1024
    tf = 256
    return pl.pallas_call(
        functools.partial(_ffn_body, final=final),
        grid=(n // tm, f // tf),
        in_specs=[pl.BlockSpec((tm, d), lambda i, j: (i, 0)),
                  pl.BlockSpec((1, N_MOD, d), lambda i, j: (i * tm // seq, 0, 0)),
                  pl.BlockSpec((1, d), lambda i, j: (0, 0)),
                  pl.BlockSpec((1, d), lambda i, j: (0, 0)),
                  pl.BlockSpec((d, tf), lambda i, j: (0, j)),
                  pl.BlockSpec((d, tf), lambda i, j: (0, j)),
                  pl.BlockSpec((tf, d), lambda i, j: (j, 0))],
        out_specs=pl.BlockSpec((tm, d), lambda i, j: (i, 0)),
        out_shape=jax.ShapeDtypeStruct((n, d), F32),
        scratch_shapes=[pltpu.VMEM((tm, d), BF16), pltpu.VMEM((tm, d), F32)],
        compiler_params=_params("parallel", "arbitrary"),
        name="ffn_mixer",
    )(x, mod, g, final_g, w_gate, w_up, w_down)


def _top2(logits):
    lane = lax.broadcasted_iota(jnp.int32, logits.shape, 1)
    lg = jnp.where(lane < N_EXPERTS, logits, -jnp.inf)
    m1 = jnp.max(lg, axis=-1, keepdims=True)
    i1 = jnp.min(jnp.where(lg == m1, lane, LANES), axis=-1, keepdims=True)
    lg2 = jnp.where(lane == i1, -jnp.inf, lg)
    m2 = jnp.max(lg2, axis=-1, keepdims=True)
    i2 = jnp.min(jnp.where(lg2 == m2, lane, LANES), axis=-1, keepdims=True)
    e2 = jnp.exp(m2 - m1)
    den = 1.0 + e2
    return i1, i2, 1.0 / den, e2 / den


def _router_body(x_ref, mod_ref, g_ref, wr_ref, h_ref, pos_ref, post_ref, gate_ref, cnt_ref):
    t = x_ref.shape[0]
    h = _modnorm(x_ref[...], g_ref[...], mod_ref[0, SC2:SC2 + 1, :], mod_ref[0, SH2:SH2 + 1, :])
    h_ref[...] = h.astype(BF16)
    logits = jnp.dot(h, wr_ref[...], preferred_element_type=F32, precision=lax.Precision.HIGHEST)
    i1, i2, w1, w2 = _top2(logits)
    lane = lax.broadcasted_iota(jnp.int32, logits.shape, 1)
    sel1 = lane == i1
    sel2 = lane == i2
    gate_ref[...] = jnp.where(sel1, w1, 0.0) + jnp.where(sel2, w2, 0.0)
    picked = jnp.logical_or(sel1, sel2)
    onehot = jnp.where(picked, 1.0, 0.0)
    row = lax.broadcasted_iota(jnp.int32, (t, t), 0)
    col = lax.broadcasted_iota(jnp.int32, (t, t), 1)
    before = jnp.where(col < row, 1.0, 0.0).astype(BF16)
    rank = jnp.dot(before, onehot.astype(BF16), preferred_element_type=F32)
    pos = jnp.where(picked, rank, -1.0)
    pos_ref[...] = pos
    post_ref[...] = pos.T[0:N_EXPERTS, :]
    cnt_ref[0] = jnp.broadcast_to(jnp.sum(onehot, axis=0, keepdims=True), cnt_ref.shape[1:])


def _router(x, mod, g, w_router, seq):
    n, d = x.shape
    t = MOE_TILE
    nt = n // t
    wr = jnp.zeros((d, LANES), F32).at[:, :N_EXPERTS].set(w_router)
    return pl.pallas_call(
        _router_body,
        grid=(nt,),
        in_specs=[pl.BlockSpec((t, d), lambda i: (i, 0)),
                  pl.BlockSpec((1, N_MOD, d), lambda i: (i * t // seq, 0, 0)),
                  pl.BlockSpec((1, d), lambda i: (0, 0)),
                  pl.BlockSpec((d, LANES), lambda i: (0, 0))],
        out_specs=[pl.BlockSpec((t, d), lambda i: (i, 0)),
                   pl.BlockSpec((t, LANES), lambda i: (i, 0)),
                   pl.BlockSpec((N_EXPERTS, t), lambda i: (0, i)),
                   pl.BlockSpec((t, LANES), lambda i: (i, 0)),
                   pl.BlockSpec((1, 8, LANES), lambda i: (i, 0, 0))],
        out_shape=[jax.ShapeDtypeStruct((n, d), BF16),
                   jax.ShapeDtypeStruct((n, LANES), F32),
                   jax.ShapeDtypeStruct((N_EXPERTS, n), F32),
                   jax.ShapeDtypeStruct((n, LANES), F32),
                   jax.ShapeDtypeStruct((nt, 8, LANES), F32)],
        compiler_params=_params("parallel"),
        name="router",
    )(x, mod, g, wr)


def _moe_body(cnt_ref, h_ref, pos_ref, post_ref, gate_ref, x_ref, mod_ref, fg_ref,
              wg_ref, wu_ref, wd_ref, o_ref, xg, acc, *, final):
    ti = pl.program_id(0)
    e = pl.program_id(1)
    s = pl.program_id(2)
    ns = pl.num_programs(2)
    t = h_ref.shape[0]
    cnt = cnt_ref[ti, e]

    @pl.when(jnp.logical_and(e == 0, s == 0))
    def _zero():
        o_ref[...] = jnp.zeros(o_ref.shape, F32)

    def block(b, bm):
        if isinstance(b, int):
            rows = slice(b * bm, (b + 1) * bm)
        else:
            rows = pl.ds(pl.multiple_of(b * bm, 16), bm)

        @pl.when(s == 0)
        def _gather():
            tgt = (lax.broadcasted_iota(jnp.int32, (bm, t), 0) + b * bm).astype(F32)
            onehot = jnp.where(post_ref[pl.ds(e, 1), :] == tgt, 1.0, 0.0).astype(BF16)
            xg[rows, :] = jnp.dot(onehot, h_ref[...], preferred_element_type=F32).astype(BF16)

        xb = xg[rows, :]
        g = jnp.dot(xb, wg_ref[0, 0], preferred_element_type=F32)
        u = jnp.dot(xb, wu_ref[0, 0], preferred_element_type=F32)
        a = (g * jax.nn.sigmoid(g) * u).astype(BF16)
        y = jnp.dot(a, wd_ref[0], preferred_element_type=F32)

        @pl.when(s == 0)
        def _first():
            acc[rows, :] = y

        @pl.when(s != 0)
        def _rest():
            acc[rows, :] += y

        @pl.when(s == ns - 1)
        def _scatter():
            lane = lax.broadcasted_iota(jnp.int32, (t, LANES), 1)
            mine = lane == e
            pos_e = jnp.sum(jnp.where(mine, pos_ref[...], 0.0), axis=-1, keepdims=True)
            gate_e = jnp.sum(jnp.where(mine, gate_ref[...], 0.0), axis=-1, keepdims=True)
            tgt = (lax.broadcasted_iota(jnp.int32, (t, bm), 1) + b * bm).astype(F32)
            onehot = jnp.where(pos_e == tgt, 1.0, 0.0).astype(BF16)
            yb = jnp.dot(onehot, acc[rows, :].astype(BF16), preferred_element_type=F32)
            o_ref[...] += gate_e * yb

    lo = 0
    for bm in MOE_BLOCKS[:-1]:
        @pl.when(jnp.logical_and(cnt > lo, cnt <= bm))
        def _single(bm=bm):
            block(0, bm)
        lo = bm

    @pl.when(cnt > lo)
    def _multi():
        bm = MOE_BLOCKS[-1]

        def body(b, carry):
            block(b, bm)
            return carry
        lax.fori_loop(0, (cnt + (bm - 1)) // bm, body, 0)

    @pl.when(jnp.logical_and(e == pl.num_programs(1) - 1, s == ns - 1))
    def _epilogue():
        y = x_ref[...] + mod_ref[0, G2:G2 + 1, :] * o_ref[...]
        if final:
            y = y * lax.rsqrt(jnp.mean(y * y, axis=-1, keepdims=True) + EPS) * fg_ref[...]
        o_ref[...] = y


def _moe(x, mod, g, final_g, w_router, w_gate, w_up, w_down, seq, final):
    n, d = x.shape
    n_e, _, f = w_gate.shape
    t = MOE_TILE
    nt = n // t
    fs = f // MOE_FSPLIT

    def pieces(w):
        return w.astype(BF16).reshape(n_e, d, MOE_FSPLIT, fs).transpose(0, 2, 1, 3)
    w_gate, w_up, w_down = pieces(w_gate), pieces(w_up), w_down.astype(BF16)
    h, pos, post, gate, cnt = _router(x, mod, g, w_router, seq)
    counts = cnt[:, 0, :N_EXPERTS].astype(jnp.int32)
    cap = pl.cdiv(t, MOE_BLOCKS[-1]) * MOE_BLOCKS[-1]
    grid_spec = pltpu.PrefetchScalarGridSpec(
        num_scalar_prefetch=1,
        grid=(nt, n_e, MOE_FSPLIT),
        in_specs=[pl.BlockSpec((t, d), lambda i, e, s, c: (i, 0)),
                  pl.BlockSpec((t, LANES), lambda i, e, s, c: (i, 0)),
                  pl.BlockSpec((N_EXPERTS, t), lambda i, e, s, c: (0, i)),
                  pl.BlockSpec((t, LANES), lambda i, e, s, c: (i, 0)),
                  pl.BlockSpec((t, d), lambda i, e, s, c: (i, 0)),
                  pl.BlockSpec((1, N_MOD, d), lambda i, e, s, c: (i * t // seq, 0, 0)),
                  pl.BlockSpec((1, d), lambda i, e, s, c: (0, 0)),
                  pl.BlockSpec((1, 1, d, fs), lambda i, e, s, c: (e, s, 0, 0)),
                  pl.BlockSpec((1, 1, d, fs), lambda i, e, s, c: (e, s, 0, 0)),
                  pl.BlockSpec((1, fs, d), lambda i, e, s, c: (e, s, 0))],
        out_specs=pl.BlockSpec((t, d), lambda i, e, s, c: (i, 0)),
        scratch_shapes=[pltpu.VMEM((cap, d), BF16), pltpu.VMEM((cap, d), F32)],
    )
    return pl.pallas_call(
        functools.partial(_moe_body, final=final),
        grid_spec=grid_spec,
        out_shape=jax.ShapeDtypeStruct((n, d), F32),
        compiler_params=_params("parallel", "arbitrary", "arbitrary"),
        name="moe_mixer",
    )(counts, h, pos, post, gate, x, mod, final_g, w_gate, w_up, w_down)


def kernel(x, c, w_ada, b_ada, norm1_g, norm2_g, w_in, w_out, gmlp_vnorm_g, gmlp_ws, gmlp_bs,
           lam_q1, lam_k1, lam_q2, lam_k2, diff_subln_g, ffn_w_gate, ffn_w_up, ffn_w_down,
           w_router, moe_w_gate, moe_w_up, moe_w_down, final_g):
    batch, seq, d = x.shape
    depth = w_ada.shape[0]
    n = batch * seq
    mods = _ada(c, w_ada, b_ada)
    xf = x.reshape(n, d)
    fg = final_g.reshape(1, d)
    for l in range(depth):
        mod = mods[l]
        z, vt = _inproj(xf, mod, norm1_g[l].reshape(1, d), w_in[l].astype(BF16), seq)
        out_a = _gmlp(z, gmlp_vnorm_g[l], gmlp_ws[l], gmlp_bs[l])
        lam_init = 0.8 - 0.6 * math.exp(-0.3 * l)
        lam_p = jnp.stack([lam_q1[l], lam_k1[l], lam_q2[l], lam_k2[l]])
        out_b = _attn(z, vt, lam_p, diff_subln_g[l], lam_init, batch, seq)
        xf = _outproj(out_a, out_b, w_out[l].astype(BF16), xf, mod, seq)
        i = l // 2
        final = l == depth - 1
        g2 = norm2_g[l].reshape(1, d)
        if l % 2 == 0:
            xf = _ffn(xf, mod, g2, fg, ffn_w_gate[i], ffn_w_up[i], ffn_w_down[i], seq, final)
        else:
            xf = _moe(xf, mod, g2, fg, w_router[i], moe_w_gate[i], moe_w_up[i], moe_w_down[i],
                      seq, final)
    return xf.reshape(batch, seq, d)
```

```python
import functools
import math

import jax
import jax.numpy as jnp
import numpy as np
from jax import lax
from jax.experimental import pallas as pl
from jax.experimental.pallas import tpu as pltpu

F32 = jnp.float32
BF16 = jnp.bfloat16
EPS = 1e-6
N_MOD = 6
N_GMLP_HEADS = 8
GMLP_HEAD_DIM = 64
CHUNK = 128
N_DIFF_HEADS = 4
DIFF_HEAD_DIM = 64
DIFF_V_DIM = 128
N_EXPERTS = 8
LANES = 128
VMEM_LIMIT = 56 * 1024 * 1024

MOE_TILE = 1024
MOE_BLOCKS = (192, 256, 320)
MOE_LOOP_BLOCK = 128
MOE_FSPLIT = 2

ATTN_BLOCK = 1024
ATTN_ONES_ROWS = 16

SH1, SC1, G1, SH2, SC2, G2 = range(N_MOD)


def _params(*sem):
    return pltpu.CompilerParams(dimension_semantics=sem, vmem_limit_bytes=VMEM_LIMIT)


def _modnorm(x, g, sc, sh):
    ms = jnp.mean(x * x, axis=-1, keepdims=True)
    return (x * lax.rsqrt(ms + EPS)) * g * (1.0 + sc) + sh


def _ada_body(c_ref, w_ref, b_ref, o_ref):
    c = c_ref[...]
    sc = c * jax.nn.sigmoid(c)
    o_ref[0] = jnp.dot(sc, w_ref[0], preferred_element_type=F32,
                       precision=lax.Precision.HIGHEST) + b_ref[0]


def _ada(c, w_ada, b_ada):
    depth, d, m = w_ada.shape
    b = c.shape[0]
    rows = 8
    cp = jnp.zeros((rows, d), F32).at[:b].set(c)
    tn = 1536
    out = pl.pallas_call(
        _ada_body,
        grid=(depth, m // tn),
        in_specs=[pl.BlockSpec((rows, d), lambda l, j: (0, 0)),
                  pl.BlockSpec((1, d, tn), lambda l, j: (l, 0, j)),
                  pl.BlockSpec((1, 1, tn), lambda l, j: (l, 0, j))],
        out_specs=pl.BlockSpec((1, rows, tn), lambda l, j: (l, 0, j)),
        out_shape=jax.ShapeDtypeStruct((depth, rows, m), F32),
        compiler_params=_params("parallel", "parallel"),
        name="ada",
    )(cp, w_ada, b_ada.reshape(depth, 1, m))
    return out[:, :b].reshape(depth, b, N_MOD, d)


def _inproj_body(x_ref, mod_ref, g_ref, w_ref, z_ref, vt_ref):
    h = _modnorm(x_ref[...], g_ref[...], mod_ref[0, SC1:SC1 + 1, :],
                 mod_ref[0, SH1:SH1 + 1, :]).astype(BF16)
    tm = x_ref.shape[0]
    zw = z_ref.shape[1]
    cw = 512
    for c0 in range(0, zw, cw):
        z_ref[:, c0:c0 + cw] = jnp.dot(
            h, w_ref[:, c0:c0 + cw], preferred_element_type=F32).astype(z_ref.dtype)
    dve = DIFF_V_DIM + ATTN_ONES_ROWS
    for hd in range(N_DIFF_HEADS):
        c0 = zw + hd * DIFF_V_DIM
        v = jnp.dot(h, w_ref[:, c0:c0 + DIFF_V_DIM], preferred_element_type=F32)
        vt_ref[hd * dve:hd * dve + DIFF_V_DIM, :] = v.T.astype(vt_ref.dtype)
        vt_ref[hd * dve + DIFF_V_DIM:(hd + 1) * dve, :] = jnp.ones((ATTN_ONES_ROWS, tm), vt_ref.dtype)


def _inproj(x, mod, g, w, seq):
    n, d = x.shape
    width = w.shape[1]
    zw = width - N_DIFF_HEADS * DIFF_V_DIM
    dve = DIFF_V_DIM + ATTN_ONES_ROWS
    tm = 512
    return pl.pallas_call(
        _inproj_body,
        grid=(n // tm,),
        in_specs=[pl.BlockSpec((tm, d), lambda i: (i, 0)),
                  pl.BlockSpec((1, N_MOD, d), lambda i: (i * tm // seq, 0, 0)),
                  pl.BlockSpec((1, d), lambda i: (0, 0)),
                  pl.BlockSpec((d, width), lambda i: (0, 0))],
        out_specs=[pl.BlockSpec((tm, zw), lambda i: (i, 0)),
                   pl.BlockSpec((N_DIFF_HEADS * dve, tm), lambda i: (0, i))],
        out_shape=[jax.ShapeDtypeStruct((n, zw), BF16),
                   jax.ShapeDtypeStruct((N_DIFF_HEADS * dve, n), BF16)],
        compiler_params=_params("parallel"),
        name="inproj",
    )(x, mod, g, w)


def _gmlp_body(u_ref, v_ref, vg_ref, w_ref, bias_ref, gm_ref, o_ref, *, nck):
    width = u_ref.shape[1]
    npair = width // LANES
    row = lax.broadcasted_iota(jnp.int32, (CHUNK, 2 * CHUNK), 0)
    col = lax.broadcasted_iota(jnp.int32, (CHUNK, 2 * CHUNK), 1)
    causal = jnp.where(col >= CHUNK, col - CHUNK, col) <= row
    lane = lax.broadcasted_iota(jnp.int32, (CHUNK, LANES), 1)
    left = lane < GMLP_HEAD_DIM
    wp = [jnp.where(causal, w_ref[p], jnp.zeros_like(w_ref[p])) for p in range(npair)]
    gm = gm_ref[...]
    for c in range(nck):
        rs = slice(c * CHUNK, (c + 1) * CHUNK)
        gu = jax.nn.gelu(u_ref[rs, :].astype(F32))
        gv = jax.nn.gelu(v_ref[rs, :].astype(F32))
        v2 = gv * gv
        hi = v2.astype(BF16)
        lo = (v2 - hi.astype(F32)).astype(BF16)
        ms = (jnp.dot(hi, gm, preferred_element_type=F32)
              + jnp.dot(lo, gm, preferred_element_type=F32))
        vn = gv * lax.rsqrt(ms + EPS) * vg_ref[...]
        outs = []
        for p in range(npair):
            vp = vn[:, p * LANES:(p + 1) * LANES]
            zero = jnp.zeros_like(vp)
            rhs = jnp.concatenate([jnp.where(left, vp, zero), jnp.where(left, zero, vp)],
                                  axis=0).astype(BF16)
            outs.append(jnp.dot(wp[p], rhs, preferred_element_type=F32))
        vm = jnp.concatenate(outs, axis=1) + bias_ref[...]
        o_ref[rs, :] = (gu * vm).astype(o_ref.dtype)


def _gmlp(z, vnorm_g, w_s, b_s):
    n = z.shape[0]
    width = N_GMLP_HEADS * GMLP_HEAD_DIM
    nck = 4
    rows = nck * CHUNK
    npair = N_GMLP_HEADS // 2
    w_pair = w_s.reshape(npair, 2, CHUNK, CHUNK).transpose(0, 2, 1, 3).reshape(
        npair, CHUNK, 2 * CHUNK).astype(BF16)
    bias = jnp.repeat(b_s.T, GMLP_HEAD_DIM, axis=1)
    grp = np.arange(width) // GMLP_HEAD_DIM
    group_mean = jnp.asarray((grp[:, None] == grp[None, :]) / GMLP_HEAD_DIM, dtype=BF16)
    return pl.pallas_call(
        functools.partial(_gmlp_body, nck=nck),
        grid=(n // rows,),
        in_specs=[pl.BlockSpec((rows, width), lambda i: (i, 0)),
                  pl.BlockSpec((rows, width), lambda i: (i, 1)),
                  pl.BlockSpec((1, width), lambda i: (0, 0)),
                  pl.BlockSpec((npair, CHUNK, 2 * CHUNK), lambda i: (0, 0, 0)),
                  pl.BlockSpec((CHUNK, width), lambda i: (0, 0)),
                  pl.BlockSpec((width, width), lambda i: (0, 0))],
        out_specs=pl.BlockSpec((rows, width), lambda i: (i, 0)),
        out_shape=jax.ShapeDtypeStruct((n, width), BF16),
        compiler_params=_params("parallel"),
        name="gmlp",
    )(z, z, vnorm_g.reshape(1, width), w_pair, bias, group_mean)


def _attn_body(*refs, tq, lam_init, ncast):
    qi_ref, ki_ref, q_ref, k_ref, vt_ref, lam_ref, g_ref = refs[:7]
    cast_src = refs[7:7 + ncast]
    o_ref = refs[7 + ncast]
    cast_dst = refs[8 + ncast:8 + 2 * ncast]
    qs1, qs2, m1, m2, acc1, acc2 = refs[8 + 2 * ncast:]
    for src, dst in zip(cast_src, cast_dst):
        dst[...] = src[...].astype(dst.dtype)
    qi = qi_ref[pl.program_id(2)]
    ki = ki_ref[pl.program_id(2)]
    maps = ((qs1, m1, acc1), (qs2, m2, acc2))
    dv = DIFF_V_DIM

    @pl.when(ki == 0)
    def _init():
        qsc = (q_ref[...].astype(F32) * (DIFF_HEAD_DIM ** -0.5 * math.log2(math.e))).astype(BF16)
        lane = lax.broadcasted_iota(jnp.int32, qsc.shape, 1)
        zero = jnp.zeros_like(qsc)
        qs1[...] = jnp.where(lane < DIFF_HEAD_DIM, qsc, zero)
        qs2[...] = jnp.where(lane < DIFF_HEAD_DIM, zero, qsc)
        for _, m_s, acc in maps:
            m_s[...] = jnp.full(m_s.shape, -jnp.inf, F32)
            acc[...] = jnp.zeros(acc.shape, F32)

    def step(masked):
        k = k_ref[...]
        vt = vt_ref[...]
        sts = [lax.dot_general(k, qs[...], (((1,), (1,)), ((), ())), preferred_element_type=F32)
               for qs, _, _ in maps]
        for st, (_, m_s, acc) in zip(sts, maps):
            if masked:
                key = lax.broadcasted_iota(jnp.int32, st.shape, 0)
                qry = lax.broadcasted_iota(jnp.int32, st.shape, 1)
                st = jnp.where(key <= qry, st, -jnp.inf)
            sb = st.astype(BF16)
            m_prev = m_s[...]
            m_new = jnp.maximum(m_prev, jnp.max(sb, axis=0, keepdims=True).astype(F32))
            alpha = jnp.exp2(m_prev - m_new)
            pt = jnp.exp2(sb - m_new.astype(BF16))
            acc[...] = alpha * acc[...] + jnp.dot(vt, pt, preferred_element_type=F32)
            m_s[...] = m_new

    @pl.when(ki < qi)
    def _full():
        step(False)

    @pl.when(ki == qi)
    def _diag():
        step(True)
        lp = lam_ref[...]
        s1 = jnp.sum(lp[0:1, :] * lp[1:2, :], axis=-1, keepdims=True)
        s2 = jnp.sum(lp[2:3, :] * lp[3:4, :], axis=-1, keepdims=True)
        lam = jnp.exp(s1) - jnp.exp(s2) + lam_init
        ot = (acc1[0:dv, :] / acc1[dv:dv + 1, :]
              - lam * (acc2[0:dv, :] / acc2[dv:dv + 1, :]))
        ot = ot * lax.rsqrt(jnp.mean(ot * ot, axis=0, keepdims=True) + EPS)
        o_ref[...] = (ot.T * g_ref[...] * (1.0 - lam_init)).astype(o_ref.dtype)


def _cast_rows(rows, nsteps):
    for rb in range(16, rows + 1, 16):
        if rows % rb == 0 and rows // rb <= nsteps:
            return rb
    raise ValueError(f"no row block for {rows} rows in {nsteps} steps")


def _attn(z, vt, lam_p, subln_g, lam_init, batch, seq, casts=()):
    n = z.shape[0]
    tq = ATTN_BLOCK
    dve = DIFF_V_DIM + ATTN_ONES_ROWS
    nq = seq // tq
    qcol = 2 * N_GMLP_HEADS * GMLP_HEAD_DIM // LANES
    kcol = qcol + N_DIFF_HEADS
    pairs = [(i, j) for i in range(nq) for j in range(i + 1)]
    npair = len(pairs)
    qi_tab = jnp.asarray([p[0] for p in pairs], jnp.int32)
    ki_tab = jnp.asarray([p[1] for p in pairs], jnp.int32)
    nsteps = batch * N_DIFF_HEADS * npair
    cast2d = [w.reshape(-1, w.shape[-1]) for w in casts]
    cast_specs = []
    for w in cast2d:
        rb = _cast_rows(w.shape[0], nsteps)
        last = w.shape[0] // rb - 1
        cast_specs.append(pl.BlockSpec(
            (rb, w.shape[1]),
            lambda b, h, p, qt, kt, last=last: (
                jnp.minimum((b * N_DIFF_HEADS + h) * npair + p, last), 0)))
    grid_spec = pltpu.PrefetchScalarGridSpec(
        num_scalar_prefetch=2,
        grid=(batch, N_DIFF_HEADS, npair),
        in_specs=[
            pl.BlockSpec((tq, LANES), lambda b, h, p, qt, kt: (b * nq + qt[p], qcol + h)),
            pl.BlockSpec((tq, LANES), lambda b, h, p, qt, kt: (b * nq + kt[p], kcol + h)),
            pl.BlockSpec((dve, tq), lambda b, h, p, qt, kt: (h, b * nq + kt[p])),
            pl.BlockSpec((4, DIFF_HEAD_DIM), lambda b, h, p, qt, kt: (0, 0)),
            pl.BlockSpec((1, DIFF_V_DIM), lambda b, h, p, qt, kt: (0, 0)),
        ] + cast_specs,
        out_specs=[pl.BlockSpec((tq, LANES), lambda b, h, p, qt, kt: (b * nq + qt[p], h))]
        + cast_specs,
        scratch_shapes=(2 * [pltpu.VMEM((tq, LANES), BF16)] + 2 * [pltpu.VMEM((1, tq), F32)]
                        + 2 * [pltpu.VMEM((dve, tq), F32)]),
    )
    outs = pl.pallas_call(
        functools.partial(_attn_body, tq=tq, lam_init=lam_init, ncast=len(casts)),
        grid_spec=grid_spec,
        out_shape=[jax.ShapeDtypeStruct((n, N_DIFF_HEADS * DIFF_V_DIM), BF16)]
        + [jax.ShapeDtypeStruct(w.shape, BF16) for w in cast2d],
        compiler_params=_params("arbitrary", "arbitrary", "arbitrary"),
        name="diff_attn",
    )(qi_tab, ki_tab, z, z, vt, lam_p, subln_g.reshape(1, DIFF_V_DIM), *cast2d)
    return outs[0], [o.reshape(w.shape) for o, w in zip(outs[1:], casts)]


def _outproj_body(a_ref, b_ref, w_ref, x_ref, mod_ref, o_ref):
    ka = a_ref.shape[1]
    y = jnp.dot(a_ref[...], w_ref[0:ka, :], preferred_element_type=F32)
    y = y + jnp.dot(b_ref[...], w_ref[ka:, :], preferred_element_type=F32)
    o_ref[...] = x_ref[...] + mod_ref[0, G1:G1 + 1, :] * y


def _outproj(out_a, out_b, w, x, mod, seq):
    n, d = x.shape
    tm = 512
    ka, kb = out_a.shape[1], out_b.shape[1]
    return pl.pallas_call(
        _outproj_body,
        grid=(n // tm,),
        in_specs=[pl.BlockSpec((tm, ka), lambda i: (i, 0)),
                  pl.BlockSpec((tm, kb), lambda i: (i, 0)),
                  pl.BlockSpec((ka + kb, d), lambda i: (0, 0)),
                  pl.BlockSpec((tm, d), lambda i: (i, 0)),
                  pl.BlockSpec((1, N_MOD, d), lambda i: (i * tm // seq, 0, 0))],
        out_specs=pl.BlockSpec((tm, d), lambda i: (i, 0)),
        out_shape=jax.ShapeDtypeStruct((n, d), F32),
        compiler_params=_params("parallel"),
        name="outproj",
    )(out_a, out_b, w, x, mod)


def _ffn_body(x_ref, mod_ref, g_ref, fg_ref, wg_ref, wu_ref, wd_ref, o_ref, h_s, acc, *, final):
    j = pl.program_id(1)

    @pl.when(j == 0)
    def _prologue():
        h = _modnorm(x_ref[...], g_ref[...], mod_ref[0, SC2:SC2 + 1, :], mod_ref[0, SH2:SH2 + 1, :])
        h_s[...] = h.astype(BF16)
        acc[...] = jnp.zeros(acc.shape, F32)

    h = h_s[...]
    g = jnp.dot(h, wg_ref[...].astype(BF16), preferred_element_type=F32)
    u = jnp.dot(h, wu_ref[...].astype(BF16), preferred_element_type=F32)
    a = (g * jax.nn.sigmoid(g) * u).astype(BF16)
    acc[...] += jnp.dot(a, wd_ref[...].astype(BF16), preferred_element_type=F32)

    @pl.when(j == pl.num_programs(1) - 1)
    def _epilogue():
        y = x_ref[...] + mod_ref[0, G2:G2 + 1, :] * acc[...]
        if final:
            y = y * lax.rsqrt(jnp.mean(y * y, axis=-1, keepdims=True) + EPS) * fg_ref[...]
        o_ref[...] = y


def _ffn(x, mod, g, final_g, w_gate, w_up, w_down, seq, final):
    n, d = x.shape
    f = w_gate.shape[1]
    tm = 1024
    tf = 256
    return pl.pallas_call(
        functools.partial(_ffn_body, final=final),
        grid=(n // tm, f // tf),
        in_specs=[pl.BlockSpec((tm, d), lambda i, j: (i, 0)),
                  pl.BlockSpec((1, N_MOD, d), lambda i, j: (i * tm // seq, 0, 0)),
                  pl.BlockSpec((1, d), lambda i, j: (0, 0)),
                  pl.BlockSpec((1, d), lambda i, j: (0, 0)),
                  pl.BlockSpec((d, tf), lambda i, j: (0, j)),
                  pl.BlockSpec((d, tf), lambda i, j: (0, j)),
                  pl.BlockSpec((tf, d), lambda i, j: (j, 0))],
        out_specs=pl.BlockSpec((tm, d), lambda i, j: (i, 0)),
        out_shape=jax.ShapeDtypeStruct((n, d), F32),
        scratch_shapes=[pltpu.VMEM((tm, d), BF16), pltpu.VMEM((tm, d), F32)],
        compiler_params=_params("parallel", "arbitrary"),
        name="ffn_mixer",
    )(x, mod, g, final_g, w_gate, w_up, w_down)


def _top2(logits):
    lane = lax.broadcasted_iota(jnp.int32, logits.shape, 1)
    lg = jnp.where(lane < N_EXPERTS, logits, -jnp.inf)
    m1 = jnp.max(lg, axis=-1, keepdims=True)
    i1 = jnp.min(jnp.where(lg == m1, lane, LANES), axis=-1, keepdims=True)
    lg2 = jnp.where(lane == i1, -jnp.inf, lg)
    m2 = jnp.max(lg2, axis=-1, keepdims=True)
    i2 = jnp.min(jnp.where(lg2 == m2, lane, LANES), axis=-1, keepdims=True)
    e2 = jnp.exp(m2 - m1)
    den = 1.0 + e2
    return i1, i2, 1.0 / den, e2 / den


def _router_body(x_ref, mod_ref, g_ref, wr_ref, h_ref, pos_ref, post_ref, gate_ref, cnt_ref):
    t = x_ref.shape[0]
    h = _modnorm(x_ref[...], g_ref[...], mod_ref[0, SC2:SC2 + 1, :], mod_ref[0, SH2:SH2 + 1, :])
    h_ref[...] = h.astype(BF16)
    logits = jnp.dot(h, wr_ref[...], preferred_element_type=F32, precision=lax.Precision.HIGHEST)
    i1, i2, w1, w2 = _top2(logits)
    lane = lax.broadcasted_iota(jnp.int32, logits.shape, 1)
    sel1 = lane == i1
    sel2 = lane == i2
    gate_ref[...] = jnp.where(sel1, w1, 0.0) + jnp.where(sel2, w2, 0.0)
    picked = jnp.logical_or(sel1, sel2)
    onehot = jnp.where(picked, 1.0, 0.0)
    row = lax.broadcasted_iota(jnp.int32, (t, t), 0)
    col = lax.broadcasted_iota(jnp.int32, (t, t), 1)
    before = jnp.where(col < row, 1.0, 0.0).astype(BF16)
    rank = jnp.dot(before, onehot.astype(BF16), preferred_element_type=F32)
    pos = jnp.where(picked, rank, -1.0)
    pos_ref[...] = pos
    post_ref[...] = pos.T[0:N_EXPERTS, :]
    cnt_ref[0] = jnp.broadcast_to(jnp.sum(onehot, axis=0, keepdims=True), cnt_ref.shape[1:])


def _router(x, mod, g, w_router, seq):
    n, d = x.shape
    t = MOE_TILE
    nt = n // t
    wr = jnp.zeros((d, LANES), F32).at[:, :N_EXPERTS].set(w_router)
    return pl.pallas_call(
        _router_body,
        grid=(nt,),
        in_specs=[pl.BlockSpec((t, d), lambda i: (i, 0)),
                  pl.BlockSpec((1, N_MOD, d), lambda i: (i * t // seq, 0, 0)),
                  pl.BlockSpec((1, d), lambda i: (0, 0)),
                  pl.BlockSpec((d, LANES), lambda i: (0, 0))],
        out_specs=[pl.BlockSpec((t, d), lambda i: (i, 0)),
                   pl.BlockSpec((t, LANES), lambda i: (i, 0)),
                   pl.BlockSpec((N_EXPERTS, t), lambda i: (0, i)),
                   pl.BlockSpec((t, LANES), lambda i: (i, 0)),
                   pl.BlockSpec((1, 8, LANES), lambda i: (i, 0, 0))],
        out_shape=[jax.ShapeDtypeStruct((n, d), BF16),
                   jax.ShapeDtypeStruct((n, LANES), F32),
                   jax.ShapeDtypeStruct((N_EXPERTS, n), F32),
                   jax.ShapeDtypeStruct((n, LANES), F32),
                   jax.ShapeDtypeStruct((nt, 8, LANES), F32)],
        compiler_params=_params("parallel"),
        name="router",
    )(x, mod, g, wr)


def _moe_body(cnt_ref, h_ref, pos_ref, post_ref, gate_ref, x_ref, mod_ref, fg_ref,
              wg_ref, wu_ref, wd_ref, o_ref, xg, acc, *, final):
    ti = pl.program_id(0)
    e = pl.program_id(1)
    s = pl.program_id(2)
    ns = pl.num_programs(2)
    t = h_ref.shape[0]
    cnt = cnt_ref[ti, e]

    @pl.when(jnp.logical_and(e == 0, s == 0))
    def _zero():
        o_ref[...] = jnp.zeros(o_ref.shape, F32)

    def block(b, bm):
        if isinstance(b, int):
            rows = slice(b * bm, (b + 1) * bm)
        else:
            rows = pl.ds(pl.multiple_of(b * bm, 16), bm)

        @pl.when(s == 0)
        def _gather():
            tgt = (lax.broadcasted_iota(jnp.int32, (bm, t), 0) + b * bm).astype(F32)
            onehot = jnp.where(post_ref[pl.ds(e, 1), :] == tgt, 1.0, 0.0).astype(BF16)
            xg[rows, :] = jnp.dot(onehot, h_ref[...], preferred_element_type=F32).astype(BF16)

        xb = xg[rows, :]
        g = jnp.dot(xb, wg_ref[0], preferred_element_type=F32)
        u = jnp.dot(xb, wu_ref[0], preferred_element_type=F32)
        a = (g * jax.nn.sigmoid(g) * u).astype(BF16)
        y = jnp.dot(a, wd_ref[0], preferred_element_type=F32)

        @pl.when(s == 0)
        def _first():
            acc[rows, :] = y

        @pl.when(s != 0)
        def _rest():
            acc[rows, :] += y

        @pl.when(s == ns - 1)
        def _scatter():
            lane = lax.broadcasted_iota(jnp.int32, (t, LANES), 1)
            mine = lane == e
            pos_e = jnp.sum(jnp.where(mine, pos_ref[...], 0.0), axis=-1, keepdims=True)
            gate_e = jnp.sum(jnp.where(mine, gate_ref[...], 0.0), axis=-1, keepdims=True)
            tgt = (lax.broadcasted_iota(jnp.int32, (t, bm), 1) + b * bm).astype(F32)
            onehot = jnp.where(pos_e == tgt, 1.0, 0.0).astype(BF16)
            yb = jnp.dot(onehot, acc[rows, :].astype(BF16), preferred_element_type=F32)
            o_ref[...] += gate_e * yb

    lo = MOE_LOOP_BLOCK
    for bm in MOE_BLOCKS:
        @pl.when(jnp.logical_and(cnt > lo, cnt <= bm))
        def _single(bm=bm):
            block(0, bm)
        lo = bm

    @pl.when(jnp.logical_or(cnt <= MOE_LOOP_BLOCK, cnt > lo))
    def _multi():
        bm = MOE_LOOP_BLOCK

        def body(b, carry):
            block(b, bm)
            return carry
        lax.fori_loop(0, (cnt + (bm - 1)) // bm, body, 0)

    @pl.when(jnp.logical_and(e == pl.num_programs(1) - 1, s == ns - 1))
    def _epilogue():
        y = x_ref[...] + mod_ref[0, G2:G2 + 1, :] * o_ref[...]
        if final:
            y = y * lax.rsqrt(jnp.mean(y * y, axis=-1, keepdims=True) + EPS) * fg_ref[...]
        o_ref[...] = y


def _moe(x, mod, g, final_g, w_router, w_gate, w_up, w_down, seq, final):
    n, d = x.shape
    n_e, _, f = w_gate.shape
    t = MOE_TILE
    nt = n // t
    fs = f // MOE_FSPLIT
    h, pos, post, gate, cnt = _router(x, mod, g, w_router, seq)
    counts = cnt[:, 0, :N_EXPERTS].astype(jnp.int32)
    cap = max(pl.cdiv(t, MOE_LOOP_BLOCK) * MOE_LOOP_BLOCK, MOE_BLOCKS[-1])
    grid_spec = pltpu.PrefetchScalarGridSpec(
        num_scalar_prefetch=1,
        grid=(nt, n_e, MOE_FSPLIT),
        in_specs=[pl.BlockSpec((t, d), lambda i, e, s, c: (i, 0)),
                  pl.BlockSpec((t, LANES), lambda i, e, s, c: (i, 0)),
                  pl.BlockSpec((N_EXPERTS, t), lambda i, e, s, c: (0, i)),
                  pl.BlockSpec((t, LANES), lambda i, e, s, c: (i, 0)),
                  pl.BlockSpec((t, d), lambda i, e, s, c: (i, 0)),
                  pl.BlockSpec((1, N_MOD, d), lambda i, e, s, c: (i * t // seq, 0, 0)),
                  pl.BlockSpec((1, d), lambda i, e, s, c: (0, 0)),
                  pl.BlockSpec((1, d, fs), lambda i, e, s, c: (e, 0, s)),
                  pl.BlockSpec((1, d, fs), lambda i, e, s, c: (e, 0, s)),
                  pl.BlockSpec((1, fs, d), lambda i, e, s, c: (e, s, 0))],
        out_specs=pl.BlockSpec((t, d), lambda i, e, s, c: (i, 0)),
        scratch_shapes=[pltpu.VMEM((cap, d), BF16), pltpu.VMEM((cap, d), F32)],
    )
    return pl.pallas_call(
        functools.partial(_moe_body, final=final),
        grid_spec=grid_spec,
        out_shape=jax.ShapeDtypeStruct((n, d), F32),
        compiler_params=_params("parallel", "arbitrary", "arbitrary"),
        name="moe_mixer",
    )(counts, h, pos, post, gate, x, mod, final_g, w_gate, w_up, w_down)


def kernel(x, c, w_ada, b_ada, norm1_g, norm2_g, w_in, w_out, gmlp_vnorm_g, gmlp_ws, gmlp_bs,
           lam_q1, lam_k1, lam_q2, lam_k2, diff_subln_g, ffn_w_gate, ffn_w_up, ffn_w_down,
           w_router, moe_w_gate, moe_w_up, moe_w_down, final_g):
    batch, seq, d = x.shape
    depth = w_ada.shape[0]
    n = batch * seq
    mods = _ada(c, w_ada, b_ada)
    xf = x.reshape(n, d)
    fg = final_g.reshape(1, d)
    for l in range(depth):
        mod = mods[l]
        z, vt = _inproj(xf, mod, norm1_g[l].reshape(1, d), w_in[l].astype(BF16), seq)
        out_a = _gmlp(z, gmlp_vnorm_g[l], gmlp_ws[l], gmlp_bs[l])
        lam_init = 0.8 - 0.6 * math.exp(-0.3 * l)
        lam_p = jnp.stack([lam_q1[l], lam_k1[l], lam_q2[l], lam_k2[l]])
        i = l // 2
        if l % 2 == 0:
            casts = [moe_w_gate[i], moe_w_up[i]] if l + 1 < depth else []
        else:
            casts = [moe_w_down[i]]
        out_b, cast_out = _attn(z, vt, lam_p, diff_subln_g[l], lam_init, batch, seq, casts)
        xf = _outproj(out_a, out_b, w_out[l].astype(BF16), xf, mod, seq)
        final = l == depth - 1
        g2 = norm2_g[l].reshape(1, d)
        if l % 2 == 0:
            xf = _ffn(xf, mod, g2, fg, ffn_w_gate[i], ffn_w_up[i], ffn_w_down[i], seq, final)
            moe_gate_up = cast_out
        else:
            xf = _moe(xf, mod, g2, fg, w_router[i], moe_gate_up[0], moe_gate_up[1], cast_out[0],
                      seq, final)
    return xf.reshape(batch, seq, d)
```

```python
import functools
import math

import jax
import jax.numpy as jnp
import numpy as np
from jax import lax
from jax.experimental import pallas as pl
from jax.experimental.pallas import tpu as pltpu

F32 = jnp.float32
BF16 = jnp.bfloat16
EPS = 1e-6
N_MOD = 6
N_GMLP_HEADS = 8
GMLP_HEAD_DIM = 64
CHUNK = 128
N_DIFF_HEADS = 4
DIFF_HEAD_DIM = 64
DIFF_V_DIM = 128
N_EXPERTS = 8
LANES = 128
VMEM_LIMIT = 56 * 1024 * 1024

MOE_TILE = 1024
MOE_BLOCKS = (192, 256, 320)
MOE_LOOP_BLOCK = 128
MOE_FSPLIT = 2

ATTN_BLOCK = 1024
ATTN_ONES_ROWS = 16

SH1, SC1, G1, SH2, SC2, G2 = range(N_MOD)


def _params(*sem):
    return pltpu.CompilerParams(dimension_semantics=sem, vmem_limit_bytes=VMEM_LIMIT)


def _modnorm(x, g, sc, sh):
    ms = jnp.mean(x * x, axis=-1, keepdims=True)
    return (x * lax.rsqrt(ms + EPS)) * g * (1.0 + sc) + sh


def _ada_body(c_ref, w_ref, b_ref, o_ref):
    c = c_ref[...]
    sc = c * jax.nn.sigmoid(c)
    o_ref[0] = jnp.dot(sc, w_ref[0], preferred_element_type=F32,
                       precision=lax.Precision.HIGHEST) + b_ref[0]


def _ada(c, w_ada, b_ada):
    depth, d, m = w_ada.shape
    b = c.shape[0]
    rows = 8
    cp = jnp.zeros((rows, d), F32).at[:b].set(c)
    tn = 1536
    out = pl.pallas_call(
        _ada_body,
        grid=(depth, m // tn),
        in_specs=[pl.BlockSpec((rows, d), lambda l, j: (0, 0)),
                  pl.BlockSpec((1, d, tn), lambda l, j: (l, 0, j)),
                  pl.BlockSpec((1, 1, tn), lambda l, j: (l, 0, j))],
        out_specs=pl.BlockSpec((1, rows, tn), lambda l, j: (l, 0, j)),
        out_shape=jax.ShapeDtypeStruct((depth, rows, m), F32),
        compiler_params=_params("parallel", "parallel"),
        name="ada",
    )(cp, w_ada, b_ada.reshape(depth, 1, m))
    return out[:, :b].reshape(depth, b, N_MOD, d)


def _inproj_body(x_ref, mod_ref, g_ref, w_ref, z_ref, vt_ref):
    h = _modnorm(x_ref[...], g_ref[...], mod_ref[0, SC1:SC1 + 1, :],
                 mod_ref[0, SH1:SH1 + 1, :]).astype(BF16)
    tm = x_ref.shape[0]
    zw = z_ref.shape[1]
    cw = 512
    for c0 in range(0, zw, cw):
        z_ref[:, c0:c0 + cw] = jnp.dot(
            h, w_ref[:, c0:c0 + cw], preferred_element_type=F32).astype(z_ref.dtype)
    dve = DIFF_V_DIM + ATTN_ONES_ROWS
    for hd in range(N_DIFF_HEADS):
        c0 = zw + hd * DIFF_V_DIM
        v = jnp.dot(h, w_ref[:, c0:c0 + DIFF_V_DIM], preferred_element_type=F32)
        vt_ref[hd * dve:hd * dve + DIFF_V_DIM, :] = v.T.astype(vt_ref.dtype)
        vt_ref[hd * dve + DIFF_V_DIM:(hd + 1) * dve, :] = jnp.ones((ATTN_ONES_ROWS, tm), vt_ref.dtype)


def _inproj(x, mod, g, w, seq):
    n, d = x.shape
    width = w.shape[1]
    zw = width - N_DIFF_HEADS * DIFF_V_DIM
    dve = DIFF_V_DIM + ATTN_ONES_ROWS
    tm = 512
    return pl.pallas_call(
        _inproj_body,
        grid=(n // tm,),
        in_specs=[pl.BlockSpec((tm, d), lambda i: (i, 0)),
                  pl.BlockSpec((1, N_MOD, d), lambda i: (i * tm // seq, 0, 0)),
                  pl.BlockSpec((1, d), lambda i: (0, 0)),
                  pl.BlockSpec((d, width), lambda i: (0, 0))],
        out_specs=[pl.BlockSpec((tm, zw), lambda i: (i, 0)),
                   pl.BlockSpec((N_DIFF_HEADS * dve, tm), lambda i: (0, i))],
        out_shape=[jax.ShapeDtypeStruct((n, zw), BF16),
                   jax.ShapeDtypeStruct((N_DIFF_HEADS * dve, n), BF16)],
        compiler_params=_params("parallel"),
        name="inproj",
    )(x, mod, g, w)


def _gmlp_body(u_ref, v_ref, vg_ref, w_ref, bias_ref, gm_ref, o_ref, *, nck):
    width = u_ref.shape[1]
    npair = width // LANES
    row = lax.broadcasted_iota(jnp.int32, (CHUNK, 2 * CHUNK), 0)
    col = lax.broadcasted_iota(jnp.int32, (CHUNK, 2 * CHUNK), 1)
    causal = jnp.where(col >= CHUNK, col - CHUNK, col) <= row
    lane = lax.broadcasted_iota(jnp.int32, (CHUNK, LANES), 1)
    left = lane < GMLP_HEAD_DIM
    wp = [jnp.where(causal, w_ref[p], jnp.zeros_like(w_ref[p])) for p in range(npair)]
    gm = gm_ref[...]
    for c in range(nck):
        rs = slice(c * CHUNK, (c + 1) * CHUNK)
        gu = jax.nn.gelu(u_ref[rs, :].astype(F32))
        gv = jax.nn.gelu(v_ref[rs, :].astype(F32))
        v2 = gv * gv
        hi = v2.astype(BF16)
        lo = (v2 - hi.astype(F32)).astype(BF16)
        ms = (jnp.dot(hi, gm, preferred_element_type=F32)
              + jnp.dot(lo, gm, preferred_element_type=F32))
        vn = gv * lax.rsqrt(ms + EPS) * vg_ref[...]
        outs = []
        for p in range(npair):
            vp = vn[:, p * LANES:(p + 1) * LANES]
            zero = jnp.zeros_like(vp)
            rhs = jnp.concatenate([jnp.where(left, vp, zero), jnp.where(left, zero, vp)],
                                  axis=0).astype(BF16)
            outs.append(jnp.dot(wp[p], rhs, preferred_element_type=F32))
        vm = jnp.concatenate(outs, axis=1) + bias_ref[...]
        o_ref[rs, :] = (gu * vm).astype(o_ref.dtype)


def _gmlp(z, vnorm_g, w_s, b_s):
    n = z.shape[0]
    width = N_GMLP_HEADS * GMLP_HEAD_DIM
    nck = 4
    rows = nck * CHUNK
    npair = N_GMLP_HEADS // 2
    w_pair = w_s.reshape(npair, 2, CHUNK, CHUNK).transpose(0, 2, 1, 3).reshape(
        npair, CHUNK, 2 * CHUNK).astype(BF16)
    bias = jnp.repeat(b_s.T, GMLP_HEAD_DIM, axis=1)
    grp = np.arange(width) // GMLP_HEAD_DIM
    group_mean = jnp.asarray((grp[:, None] == grp[None, :]) / GMLP_HEAD_DIM, dtype=BF16)
    return pl.pallas_call(
        functools.partial(_gmlp_body, nck=nck),
        grid=(n // rows,),
        in_specs=[pl.BlockSpec((rows, width), lambda i: (i, 0)),
                  pl.BlockSpec((rows, width), lambda i: (i, 1)),
                  pl.BlockSpec((1, width), lambda i: (0, 0)),
                  pl.BlockSpec((npair, CHUNK, 2 * CHUNK), lambda i: (0, 0, 0)),
                  pl.BlockSpec((CHUNK, width), lambda i: (0, 0)),
                  pl.BlockSpec((width, width), lambda i: (0, 0))],
        out_specs=pl.BlockSpec((rows, width), lambda i: (i, 0)),
        out_shape=jax.ShapeDtypeStruct((n, width), BF16),
        compiler_params=_params("parallel"),
        name="gmlp",
    )(z, z, vnorm_g.reshape(1, width), w_pair, bias, group_mean)


def _attn_body(*refs, tq, lam_init, ncast):
    qi_ref, ki_ref, q_ref, k_ref, vt_ref, lam_ref, g_ref = refs[:7]
    cast_src = refs[7:7 + ncast]
    o_ref = refs[7 + ncast]
    cast_dst = refs[8 + ncast:8 + 2 * ncast]
    qs1, qs2, m1, m2, acc1, acc2 = refs[8 + 2 * ncast:]
    for src, dst in zip(cast_src, cast_dst):
        pw = dst.shape[2]
        for j in range(dst.shape[0]):
            dst[j] = src[:, j * pw:(j + 1) * pw].astype(dst.dtype)
    qi = qi_ref[pl.program_id(2)]
    ki = ki_ref[pl.program_id(2)]
    maps = ((qs1, m1, acc1), (qs2, m2, acc2))
    dv = DIFF_V_DIM

    @pl.when(ki == 0)
    def _init():
        qsc = (q_ref[...].astype(F32) * (DIFF_HEAD_DIM ** -0.5 * math.log2(math.e))).astype(BF16)
        lane = lax.broadcasted_iota(jnp.int32, qsc.shape, 1)
        zero = jnp.zeros_like(qsc)
        qs1[...] = jnp.where(lane < DIFF_HEAD_DIM, qsc, zero)
        qs2[...] = jnp.where(lane < DIFF_HEAD_DIM, zero, qsc)
        for _, m_s, acc in maps:
            m_s[...] = jnp.full(m_s.shape, -jnp.inf, F32)
            acc[...] = jnp.zeros(acc.shape, F32)

    def step(masked):
        k = k_ref[...]
        vt = vt_ref[...]
        sts = [lax.dot_general(k, qs[...], (((1,), (1,)), ((), ())), preferred_element_type=F32)
               for qs, _, _ in maps]
        for st, (_, m_s, acc) in zip(sts, maps):
            if masked:
                key = lax.broadcasted_iota(jnp.int32, st.shape, 0)
                qry = lax.broadcasted_iota(jnp.int32, st.shape, 1)
                st = jnp.where(key <= qry, st, -jnp.inf)
            sb = st.astype(BF16)
            m_prev = m_s[...]
            m_new = jnp.maximum(m_prev, jnp.max(sb, axis=0, keepdims=True).astype(F32))
            alpha = jnp.exp2(m_prev - m_new)
            pt = jnp.exp2(sb - m_new.astype(BF16))
            acc[...] = alpha * acc[...] + jnp.dot(vt, pt, preferred_element_type=F32)
            m_s[...] = m_new

    @pl.when(ki < qi)
    def _full():
        step(False)

    @pl.when(ki == qi)
    def _diag():
        step(True)
        lp = lam_ref[...]
        s1 = jnp.sum(lp[0:1, :] * lp[1:2, :], axis=-1, keepdims=True)
        s2 = jnp.sum(lp[2:3, :] * lp[3:4, :], axis=-1, keepdims=True)
        lam = jnp.exp(s1) - jnp.exp(s2) + lam_init
        ot = (acc1[0:dv, :] / acc1[dv:dv + 1, :]
              - lam * (acc2[0:dv, :] / acc2[dv:dv + 1, :]))
        ot = ot * lax.rsqrt(jnp.mean(ot * ot, axis=0, keepdims=True) + EPS)
        o_ref[...] = (ot.T * g_ref[...] * (1.0 - lam_init)).astype(o_ref.dtype)


def _cast_rows(rows, nsteps):
    for rb in range(16, rows + 1, 16):
        if rows % rb == 0 and rows // rb <= nsteps:
            return rb
    raise ValueError(f"no row block for {rows} rows in {nsteps} steps")


def _attn(z, vt, lam_p, subln_g, lam_init, batch, seq, casts=()):
    n = z.shape[0]
    tq = ATTN_BLOCK
    dve = DIFF_V_DIM + ATTN_ONES_ROWS
    nq = seq // tq
    qcol = 2 * N_GMLP_HEADS * GMLP_HEAD_DIM // LANES
    kcol = qcol + N_DIFF_HEADS
    pairs = [(i, j) for i in range(nq) for j in range(i + 1)]
    npair = len(pairs)
    qi_tab = jnp.asarray([p[0] for p in pairs], jnp.int32)
    ki_tab = jnp.asarray([p[1] for p in pairs], jnp.int32)
    nsteps = batch * N_DIFF_HEADS * npair
    cast2d = [w.reshape(-1, w.shape[-1]) for w, _, _ in casts]
    cast_in, cast_out, cast_shape = [], [], []
    for w, i, pieces in casts:
        rows, cols = w.shape[1] * w.shape[2], w.shape[3]
        rb = _cast_rows(rows, nsteps)
        nb = rows // rb

        def blk(b, h, p, nb=nb):
            return jnp.minimum((b * N_DIFF_HEADS + h) * npair + p, nb - 1)
        cast_in.append(pl.BlockSpec(
            (rb, cols), lambda b, h, p, qt, kt, blk=blk, off=i * nb: (off + blk(b, h, p), 0)))
        cast_out.append(pl.BlockSpec(
            (pieces, rb, cols // pieces), lambda b, h, p, qt, kt, blk=blk: (0, blk(b, h, p), 0)))
        cast_shape.append((pieces, rows, cols // pieces))
    grid_spec = pltpu.PrefetchScalarGridSpec(
        num_scalar_prefetch=2,
        grid=(batch, N_DIFF_HEADS, npair),
        in_specs=[
            pl.BlockSpec((tq, LANES), lambda b, h, p, qt, kt: (b * nq + qt[p], qcol + h)),
            pl.BlockSpec((tq, LANES), lambda b, h, p, qt, kt: (b * nq + kt[p], kcol + h)),
            pl.BlockSpec((dve, tq), lambda b, h, p, qt, kt: (h, b * nq + kt[p])),
            pl.BlockSpec((4, DIFF_HEAD_DIM), lambda b, h, p, qt, kt: (0, 0)),
            pl.BlockSpec((1, DIFF_V_DIM), lambda b, h, p, qt, kt: (0, 0)),
        ] + cast_in,
        out_specs=[pl.BlockSpec((tq, LANES), lambda b, h, p, qt, kt: (b * nq + qt[p], h))]
        + cast_out,
        scratch_shapes=(2 * [pltpu.VMEM((tq, LANES), BF16)] + 2 * [pltpu.VMEM((1, tq), F32)]
                        + 2 * [pltpu.VMEM((dve, tq), F32)]),
    )
    outs = pl.pallas_call(
        functools.partial(_attn_body, tq=tq, lam_init=lam_init, ncast=len(casts)),
        grid_spec=grid_spec,
        out_shape=[jax.ShapeDtypeStruct((n, N_DIFF_HEADS * DIFF_V_DIM), BF16)]
        + [jax.ShapeDtypeStruct(s, BF16) for s in cast_shape],
        compiler_params=_params("arbitrary", "arbitrary", "arbitrary"),
        name="diff_attn",
    )(qi_tab, ki_tab, z, z, vt, lam_p, subln_g.reshape(1, DIFF_V_DIM), *cast2d)
    return outs[0], [o.reshape(pieces, w.shape[1], w.shape[2], w.shape[3] // pieces)
                     for o, (w, _, pieces) in zip(outs[1:], casts)]


def _outproj_body(a_ref, b_ref, w_ref, x_ref, mod_ref, o_ref):
    ka = a_ref.shape[1]
    y = jnp.dot(a_ref[...], w_ref[0:ka, :], preferred_element_type=F32)
    y = y + jnp.dot(b_ref[...], w_ref[ka:, :], preferred_element_type=F32)
    o_ref[...] = x_ref[...] + mod_ref[0, G1:G1 + 1, :] * y


def _outproj(out_a, out_b, w, x, mod, seq):
    n, d = x.shape
    tm = 512
    ka, kb = out_a.shape[1], out_b.shape[1]
    return pl.pallas_call(
        _outproj_body,
        grid=(n // tm,),
        in_specs=[pl.BlockSpec((tm, ka), lambda i: (i, 0)),
                  pl.BlockSpec((tm, kb), lambda i: (i, 0)),
                  pl.BlockSpec((ka + kb, d), lambda i: (0, 0)),
                  pl.BlockSpec((tm, d), lambda i: (i, 0)),
                  pl.BlockSpec((1, N_MOD, d), lambda i: (i * tm // seq, 0, 0))],
        out_specs=pl.BlockSpec((tm, d), lambda i: (i, 0)),
        out_shape=jax.ShapeDtypeStruct((n, d), F32),
        compiler_params=_params("parallel"),
        name="outproj",
    )(out_a, out_b, w, x, mod)


def _ffn_body(x_ref, mod_ref, g_ref, fg_ref, wg_ref, wu_ref, wd_ref, o_ref, h_s, acc, *, final):
    j = pl.program_id(1)

    @pl.when(j == 0)
    def _prologue():
        h = _modnorm(x_ref[...], g_ref[...], mod_ref[0, SC2:SC2 + 1, :], mod_ref[0, SH2:SH2 + 1, :])
        h_s[...] = h.astype(BF16)
        acc[...] = jnp.zeros(acc.shape, F32)

    h = h_s[...]
    g = jnp.dot(h, wg_ref[0].astype(BF16), preferred_element_type=F32)
    u = jnp.dot(h, wu_ref[0].astype(BF16), preferred_element_type=F32)
    a = (g * jax.nn.sigmoid(g) * u).astype(BF16)
    acc[...] += jnp.dot(a, wd_ref[0].astype(BF16), preferred_element_type=F32)

    @pl.when(j == pl.num_programs(1) - 1)
    def _epilogue():
        y = x_ref[...] + mod_ref[0, G2:G2 + 1, :] * acc[...]
        if final:
            y = y * lax.rsqrt(jnp.mean(y * y, axis=-1, keepdims=True) + EPS) * fg_ref[...]
        o_ref[...] = y


def _ffn(x, mod, g, final_g, w_gate, w_up, w_down, li, seq, final):
    n, d = x.shape
    f = w_gate.shape[2]
    tm = 1024
    tf = 256
    return pl.pallas_call(
        functools.partial(_ffn_body, final=final),
        grid=(n // tm, f // tf),
        in_specs=[pl.BlockSpec((tm, d), lambda i, j: (i, 0)),
                  pl.BlockSpec((1, N_MOD, d), lambda i, j: (i * tm // seq, 0, 0)),
                  pl.BlockSpec((1, d), lambda i, j: (0, 0)),
                  pl.BlockSpec((1, d), lambda i, j: (0, 0)),
                  pl.BlockSpec((1, d, tf), lambda i, j: (li, 0, j)),
                  pl.BlockSpec((1, d, tf), lambda i, j: (li, 0, j)),
                  pl.BlockSpec((1, tf, d), lambda i, j: (li, j, 0))],
        out_specs=pl.BlockSpec((tm, d), lambda i, j: (i, 0)),
        out_shape=jax.ShapeDtypeStruct((n, d), F32),
        scratch_shapes=[pltpu.VMEM((tm, d), BF16), pltpu.VMEM((tm, d), F32)],
        compiler_params=_params("parallel", "arbitrary"),
        name="ffn_mixer",
    )(x, mod, g, final_g, w_gate, w_up, w_down)


def _top2(logits):
    lane = lax.broadcasted_iota(jnp.int32, logits.shape, 1)
    lg = jnp.where(lane < N_EXPERTS, logits, -jnp.inf)
    m1 = jnp.max(lg, axis=-1, keepdims=True)
    i1 = jnp.min(jnp.where(lg == m1, lane, LANES), axis=-1, keepdims=True)
    lg2 = jnp.where(lane == i1, -jnp.inf, lg)
    m2 = jnp.max(lg2, axis=-1, keepdims=True)
    i2 = jnp.min(jnp.where(lg2 == m2, lane, LANES), axis=-1, keepdims=True)
    e2 = jnp.exp(m2 - m1)
    den = 1.0 + e2
    return i1, i2, 1.0 / den, e2 / den


def _router_body(x_ref, mod_ref, g_ref, wr_ref, h_ref, pos_ref, post_ref, gate_ref, cnt_ref):
    t = x_ref.shape[0]
    h = _modnorm(x_ref[...], g_ref[...], mod_ref[0, SC2:SC2 + 1, :], mod_ref[0, SH2:SH2 + 1, :])
    h_ref[...] = h.astype(BF16)
    logits = jnp.dot(h, wr_ref[...], preferred_element_type=F32, precision=lax.Precision.HIGHEST)
    i1, i2, w1, w2 = _top2(logits)
    lane = lax.broadcasted_iota(jnp.int32, logits.shape, 1)
    sel1 = lane == i1
    sel2 = lane == i2
    gate_ref[...] = jnp.where(sel1, w1, 0.0) + jnp.where(sel2, w2, 0.0)
    picked = jnp.logical_or(sel1, sel2)
    onehot = jnp.where(picked, 1.0, 0.0)
    row = lax.broadcasted_iota(jnp.int32, (t, t), 0)
    col = lax.broadcasted_iota(jnp.int32, (t, t), 1)
    before = jnp.where(col < row, 1.0, 0.0).astype(BF16)
    rank = jnp.dot(before, onehot.astype(BF16), preferred_element_type=F32)
    pos = jnp.where(picked, rank, -1.0)
    pos_ref[...] = pos
    post_ref[...] = pos.T[0:N_EXPERTS, :]
    cnt_ref[0] = jnp.broadcast_to(jnp.sum(onehot, axis=0, keepdims=True), cnt_ref.shape[1:])


def _router(x, mod, g, w_router, seq):
    n, d = x.shape
    t = MOE_TILE
    nt = n // t
    wr = jnp.zeros((d, LANES), F32).at[:, :N_EXPERTS].set(w_router)
    return pl.pallas_call(
        _router_body,
        grid=(nt,),
        in_specs=[pl.BlockSpec((t, d), lambda i: (i, 0)),
                  pl.BlockSpec((1, N_MOD, d), lambda i: (i * t // seq, 0, 0)),
                  pl.BlockSpec((1, d), lambda i: (0, 0)),
                  pl.BlockSpec((d, LANES), lambda i: (0, 0))],
        out_specs=[pl.BlockSpec((t, d), lambda i: (i, 0)),
                   pl.BlockSpec((t, LANES), lambda i: (i, 0)),
                   pl.BlockSpec((N_EXPERTS, t), lambda i: (0, i)),
                   pl.BlockSpec((t, LANES), lambda i: (i, 0)),
                   pl.BlockSpec((1, 8, LANES), lambda i: (i, 0, 0))],
        out_shape=[jax.ShapeDtypeStruct((n, d), BF16),
                   jax.ShapeDtypeStruct((n, LANES), F32),
                   jax.ShapeDtypeStruct((N_EXPERTS, n), F32),
                   jax.ShapeDtypeStruct((n, LANES), F32),
                   jax.ShapeDtypeStruct((nt, 8, LANES), F32)],
        compiler_params=_params("parallel"),
        name="router",
    )(x, mod, g, wr)


def _moe_body(cnt_ref, h_ref, pos_ref, post_ref, gate_ref, x_ref, mod_ref, fg_ref,
              wg_ref, wu_ref, wd_ref, o_ref, xg, acc, *, final):
    ti = pl.program_id(0)
    e = pl.program_id(1)
    s = pl.program_id(2)
    ns = pl.num_programs(2)
    t = h_ref.shape[0]
    cnt = cnt_ref[ti, e]

    @pl.when(jnp.logical_and(e == 0, s == 0))
    def _zero():
        o_ref[...] = jnp.zeros(o_ref.shape, F32)

    def block(b, bm):
        if isinstance(b, int):
            rows = slice(b * bm, (b + 1) * bm)
        else:
            rows = pl.ds(pl.multiple_of(b * bm, 16), bm)

        @pl.when(s == 0)
        def _gather():
            tgt = (lax.broadcasted_iota(jnp.int32, (bm, t), 0) + b * bm).astype(F32)
            onehot = jnp.where(post_ref[pl.ds(e, 1), :] == tgt, 1.0, 0.0).astype(BF16)
            xg[rows, :] = jnp.dot(onehot, h_ref[...], preferred_element_type=F32).astype(BF16)

        xb = xg[rows, :]
        g = jnp.dot(xb, wg_ref[0, 0], preferred_element_type=F32)
        u = jnp.dot(xb, wu_ref[0, 0], preferred_element_type=F32)
        a = (g * jax.nn.sigmoid(g) * u).astype(BF16)
        y = jnp.dot(a, wd_ref[0], preferred_element_type=F32)

        @pl.when(s == 0)
        def _first():
            acc[rows, :] = y

        @pl.when(s != 0)
        def _rest():
            acc[rows, :] += y

        @pl.when(s == ns - 1)
        def _scatter():
            lane = lax.broadcasted_iota(jnp.int32, (t, LANES), 1)
            mine = lane == e
            pos_e = jnp.sum(jnp.where(mine, pos_ref[...], 0.0), axis=-1, keepdims=True)
            gate_e = jnp.sum(jnp.where(mine, gate_ref[...], 0.0), axis=-1, keepdims=True)
            tgt = (lax.broadcasted_iota(jnp.int32, (t, bm), 1) + b * bm).astype(F32)
            onehot = jnp.where(pos_e == tgt, 1.0, 0.0).astype(BF16)
            yb = jnp.dot(onehot, acc[rows, :].astype(BF16), preferred_element_type=F32)
            o_ref[...] += gate_e * yb

    lo = MOE_LOOP_BLOCK
    for bm in MOE_BLOCKS:
        @pl.when(jnp.logical_and(cnt > lo, cnt <= bm))
        def _single(bm=bm):
            block(0, bm)
        lo = bm

    @pl.when(jnp.logical_or(cnt <= MOE_LOOP_BLOCK, cnt > lo))
    def _multi():
        bm = MOE_LOOP_BLOCK

        def body(b, carry):
            block(b, bm)
            return carry
        lax.fori_loop(0, (cnt + (bm - 1)) // bm, body, 0)

    @pl.when(jnp.logical_and(e == pl.num_programs(1) - 1, s == ns - 1))
    def _epilogue():
        y = x_ref[...] + mod_ref[0, G2:G2 + 1, :] * o_ref[...]
        if final:
            y = y * lax.rsqrt(jnp.mean(y * y, axis=-1, keepdims=True) + EPS) * fg_ref[...]
        o_ref[...] = y


def _moe(x, mod, g, final_g, w_router, w_gate, w_up, w_down, seq, final):
    n, d = x.shape
    n_e, f, _ = w_down.shape
    t = MOE_TILE
    nt = n // t
    fs = f // MOE_FSPLIT
    h, pos, post, gate, cnt = _router(x, mod, g, w_router, seq)
    counts = cnt[:, 0, :N_EXPERTS].astype(jnp.int32)
    cap = max(pl.cdiv(t, MOE_LOOP_BLOCK) * MOE_LOOP_BLOCK, MOE_BLOCKS[-1])
    grid_spec = pltpu.PrefetchScalarGridSpec(
        num_scalar_prefetch=1,
        grid=(nt, n_e, MOE_FSPLIT),
        in_specs=[pl.BlockSpec((t, d), lambda i, e, s, c: (i, 0)),
                  pl.BlockSpec((t, LANES), lambda i, e, s, c: (i, 0)),
                  pl.BlockSpec((N_EXPERTS, t), lambda i, e, s, c: (0, i)),
                  pl.BlockSpec((t, LANES), lambda i, e, s, c: (i, 0)),
                  pl.BlockSpec((t, d), lambda i, e, s, c: (i, 0)),
                  pl.BlockSpec((1, N_MOD, d), lambda i, e, s, c: (i * t // seq, 0, 0)),
                  pl.BlockSpec((1, d), lambda i, e, s, c: (0, 0)),
                  pl.BlockSpec((1, 1, d, fs), lambda i, e, s, c: (s, e, 0, 0)),
                  pl.BlockSpec((1, 1, d, fs), lambda i, e, s, c: (s, e, 0, 0)),
                  pl.BlockSpec((1, fs, d), lambda i, e, s, c: (e, s, 0))],
        out_specs=pl.BlockSpec((t, d), lambda i, e, s, c: (i, 0)),
        scratch_shapes=[pltpu.VMEM((cap, d), BF16), pltpu.VMEM((cap, d), F32)],
    )
    return pl.pallas_call(
        functools.partial(_moe_body, final=final),
        grid_spec=grid_spec,
        out_shape=jax.ShapeDtypeStruct((n, d), F32),
        compiler_params=_params("parallel", "arbitrary", "arbitrary"),
        name="moe_mixer",
    )(counts, h, pos, post, gate, x, mod, final_g, w_gate, w_up, w_down)


def kernel(x, c, w_ada, b_ada, norm1_g, norm2_g, w_in, w_out, gmlp_vnorm_g, gmlp_ws, gmlp_bs,
           lam_q1, lam_k1, lam_q2, lam_k2, diff_subln_g, ffn_w_gate, ffn_w_up, ffn_w_down,
           w_router, moe_w_gate, moe_w_up, moe_w_down, final_g):
    batch, seq, d = x.shape
    depth = w_ada.shape[0]
    n = batch * seq
    mods = _ada(c, w_ada, b_ada)
    xf = x.reshape(n, d)
    fg = final_g.reshape(1, d)
    for l in range(depth):
        mod = mods[l]
        z, vt = _inproj(xf, mod, norm1_g[l].reshape(1, d), w_in[l].astype(BF16), seq)
        out_a = _gmlp(z, gmlp_vnorm_g[l], gmlp_ws[l], gmlp_bs[l])
        lam_init = 0.8 - 0.6 * math.exp(-0.3 * l)
        lam_p = jnp.stack([lam_q1[l], lam_k1[l], lam_q2[l], lam_k2[l]])
        i = l // 2
        if l % 2 == 0:
            casts = ([(moe_w_gate, i, MOE_FSPLIT), (moe_w_up, i, MOE_FSPLIT)]
                     if l + 1 < depth else [])
        else:
            casts = [(moe_w_down, i, 1)]
        out_b, cast_out = _attn(z, vt, lam_p, diff_subln_g[l], lam_init, batch, seq, casts)
        xf = _outproj(out_a, out_b, w_out[l].astype(BF16), xf, mod, seq)
        final = l == depth - 1
        g2 = norm2_g[l].reshape(1, d)
        if l % 2 == 0:
            xf = _ffn(xf, mod, g2, fg, ffn_w_gate, ffn_w_up, ffn_w_down, i, seq, final)
            moe_gate_up = cast_out
        else:
            xf = _moe(xf, mod, g2, fg, w_router[i], moe_gate_up[0], moe_gate_up[1], cast_out[0][0],
                      seq, final)
    return xf.reshape(batch, seq, d)
```

```python
import functools
import math

import jax
import jax.numpy as jnp
import numpy as np
from jax import lax
from jax.experimental import pallas as pl
from jax.experimental.pallas import tpu as pltpu

F32 = jnp.float32
BF16 = jnp.bfloat16
EPS = 1e-6
N_MOD = 6
N_GMLP_HEADS = 8
GMLP_HEAD_DIM = 64
CHUNK = 128
N_DIFF_HEADS = 4
DIFF_HEAD_DIM = 64
DIFF_V_DIM = 128
N_EXPERTS = 8
LANES = 128
VMEM_LIMIT = 56 * 1024 * 1024

MOE_TILE = 1024
MOE_BLOCKS = (192, 256, 320)
MOE_LOOP_BLOCK = 128
MOE_FSPLIT = 2

ATTN_BLOCK = 1024
ATTN_ONES_ROWS = 16

SH1, SC1, G1, SH2, SC2, G2 = range(N_MOD)


def _params(*sem):
    return pltpu.CompilerParams(dimension_semantics=sem, vmem_limit_bytes=VMEM_LIMIT)


def _modnorm(x, g, sc, sh):
    ms = jnp.mean(x * x, axis=-1, keepdims=True)
    return (x * lax.rsqrt(ms + EPS)) * g * (1.0 + sc) + sh


def _ada_body(c_ref, w_ref, b_ref, o_ref):
    c = c_ref[...]
    sc = c * jax.nn.sigmoid(c)
    o_ref[0] = jnp.dot(sc, w_ref[0], preferred_element_type=F32,
                       precision=lax.Precision.HIGHEST) + b_ref[0]


def _ada(c, w_ada, b_ada):
    depth, d, m = w_ada.shape
    b = c.shape[0]
    rows = 8
    cp = jnp.zeros((rows, d), F32).at[:b].set(c)
    tn = 3072
    out = pl.pallas_call(
        _ada_body,
        grid=(depth, m // tn),
        in_specs=[pl.BlockSpec((rows, d), lambda l, j: (0, 0)),
                  pl.BlockSpec((1, d, tn), lambda l, j: (l, 0, j)),
                  pl.BlockSpec((1, 1, tn), lambda l, j: (l, 0, j))],
        out_specs=pl.BlockSpec((1, rows, tn), lambda l, j: (l, 0, j)),
        out_shape=jax.ShapeDtypeStruct((depth, rows, m), F32),
        compiler_params=_params("parallel", "parallel"),
        name="ada",
    )(cp, w_ada, b_ada.reshape(depth, 1, m))
    return out[:, :b].reshape(depth, b, N_MOD, d)


def _inproj_body(x_ref, mod_ref, g_ref, w_ref, wvt_ref, z_ref, vt_ref):
    h = _modnorm(x_ref[...], g_ref[...], mod_ref[0, SC1:SC1 + 1, :],
                 mod_ref[0, SH1:SH1 + 1, :]).astype(BF16)
    tm = x_ref.shape[0]
    zw = z_ref.shape[1]
    cw = 512
    for c0 in range(0, zw, cw):
        z_ref[:, c0:c0 + cw] = jnp.dot(
            h, w_ref[:, c0:c0 + cw], preferred_element_type=F32).astype(z_ref.dtype)
    dve = DIFF_V_DIM + ATTN_ONES_ROWS
    for hd in range(N_DIFF_HEADS):
        vt = lax.dot_general(wvt_ref[hd * DIFF_V_DIM:(hd + 1) * DIFF_V_DIM, :], h,
                             (((1,), (1,)), ((), ())), preferred_element_type=F32)
        vt_ref[hd * dve:hd * dve + DIFF_V_DIM, :] = vt.astype(vt_ref.dtype)
        vt_ref[hd * dve + DIFF_V_DIM:(hd + 1) * dve, :] = jnp.ones((ATTN_ONES_ROWS, tm), vt_ref.dtype)


def _inproj(x, mod, g, w, seq):
    n, d = x.shape
    vw = N_DIFF_HEADS * DIFF_V_DIM
    zw = w.shape[1] - vw
    dve = DIFF_V_DIM + ATTN_ONES_ROWS
    tm = 512
    return pl.pallas_call(
        _inproj_body,
        grid=(n // tm,),
        in_specs=[pl.BlockSpec((tm, d), lambda i: (i, 0)),
                  pl.BlockSpec((1, N_MOD, d), lambda i: (i * tm // seq, 0, 0)),
                  pl.BlockSpec((1, d), lambda i: (0, 0)),
                  pl.BlockSpec((d, zw), lambda i: (0, 0)),
                  pl.BlockSpec((vw, d), lambda i: (0, 0))],
        out_specs=[pl.BlockSpec((tm, zw), lambda i: (i, 0)),
                   pl.BlockSpec((N_DIFF_HEADS * dve, tm), lambda i: (0, i))],
        out_shape=[jax.ShapeDtypeStruct((n, zw), BF16),
                   jax.ShapeDtypeStruct((N_DIFF_HEADS * dve, n), BF16)],
        compiler_params=_params("parallel"),
        name="inproj",
    )(x, mod, g, w[:, :zw], w[:, zw:].T)


def _gmlp_body(u_ref, v_ref, vg_ref, w_ref, bias_ref, gm_ref, o_ref, *, nck):
    width = u_ref.shape[1]
    npair = width // LANES
    row = lax.broadcasted_iota(jnp.int32, (CHUNK, 2 * CHUNK), 0)
    col = lax.broadcasted_iota(jnp.int32, (CHUNK, 2 * CHUNK), 1)
    causal = jnp.where(col >= CHUNK, col - CHUNK, col) <= row
    lane = lax.broadcasted_iota(jnp.int32, (CHUNK, LANES), 1)
    left = lane < GMLP_HEAD_DIM
    wp = [jnp.where(causal, w_ref[p], jnp.zeros_like(w_ref[p])) for p in range(npair)]
    gm = gm_ref[...]
    for c in range(nck):
        rs = slice(c * CHUNK, (c + 1) * CHUNK)
        gu = jax.nn.gelu(u_ref[rs, :].astype(F32))
        gv = jax.nn.gelu(v_ref[rs, :].astype(F32))
        v2 = gv * gv
        hi = v2.astype(BF16)
        lo = (v2 - hi.astype(F32)).astype(BF16)
        ms = (jnp.dot(hi, gm, preferred_element_type=F32)
              + jnp.dot(lo, gm, preferred_element_type=F32))
        vn = gv * lax.rsqrt(ms + EPS) * vg_ref[...]
        outs = []
        for p in range(npair):
            vp = vn[:, p * LANES:(p + 1) * LANES]
            zero = jnp.zeros_like(vp)
            rhs = jnp.concatenate([jnp.where(left, vp, zero), jnp.where(left, zero, vp)],
                                  axis=0).astype(BF16)
            outs.append(jnp.dot(wp[p], rhs, preferred_element_type=F32))
        vm = jnp.concatenate(outs, axis=1) + bias_ref[...]
        o_ref[rs, :] = (gu * vm).astype(o_ref.dtype)


def _gmlp(z, vnorm_g, w_s, b_s):
    n = z.shape[0]
    width = N_GMLP_HEADS * GMLP_HEAD_DIM
    nck = 4
    rows = nck * CHUNK
    npair = N_GMLP_HEADS // 2
    w_pair = w_s.reshape(npair, 2, CHUNK, CHUNK).transpose(0, 2, 1, 3).reshape(
        npair, CHUNK, 2 * CHUNK).astype(BF16)
    bias = jnp.repeat(b_s.T, GMLP_HEAD_DIM, axis=1)
    grp = np.arange(width) // GMLP_HEAD_DIM
    group_mean = jnp.asarray((grp[:, None] == grp[None, :]) / GMLP_HEAD_DIM, dtype=BF16)
    return pl.pallas_call(
        functools.partial(_gmlp_body, nck=nck),
        grid=(n // rows,),
        in_specs=[pl.BlockSpec((rows, width), lambda i: (i, 0)),
                  pl.BlockSpec((rows, width), lambda i: (i, 1)),
                  pl.BlockSpec((1, width), lambda i: (0, 0)),
                  pl.BlockSpec((npair, CHUNK, 2 * CHUNK), lambda i: (0, 0, 0)),
                  pl.BlockSpec((CHUNK, width), lambda i: (0, 0)),
                  pl.BlockSpec((width, width), lambda i: (0, 0))],
        out_specs=pl.BlockSpec((rows, width), lambda i: (i, 0)),
        out_shape=jax.ShapeDtypeStruct((n, width), BF16),
        compiler_params=_params("parallel"),
        name="gmlp",
    )(z, z, vnorm_g.reshape(1, width), w_pair, bias, group_mean)


def _attn_body(*refs, tq, lam_init, ncast):
    qi_ref, ki_ref, q_ref, k_ref, vt_ref, lam_ref, g_ref = refs[:7]
    cast_src = refs[7:7 + ncast]
    o_ref = refs[7 + ncast]
    cast_dst = refs[8 + ncast:8 + 2 * ncast]
    qs1, qs2, m1, m2, acc1, acc2 = refs[8 + 2 * ncast:]
    for src, dst in zip(cast_src, cast_dst):
        pw = dst.shape[2]
        for j in range(dst.shape[0]):
            dst[j] = src[:, j * pw:(j + 1) * pw].astype(dst.dtype)
    qi = qi_ref[pl.program_id(2)]
    ki = ki_ref[pl.program_id(2)]
    maps = ((qs1, m1, acc1), (qs2, m2, acc2))
    dv = DIFF_V_DIM

    @pl.when(ki == 0)
    def _init():
        qsc = (q_ref[...].astype(F32) * (DIFF_HEAD_DIM ** -0.5 * math.log2(math.e))).astype(BF16)
        lane = lax.broadcasted_iota(jnp.int32, qsc.shape, 1)
        zero = jnp.zeros_like(qsc)
        qs1[...] = jnp.where(lane < DIFF_HEAD_DIM, qsc, zero)
        qs2[...] = jnp.where(lane < DIFF_HEAD_DIM, zero, qsc)
        for _, m_s, acc in maps:
            m_s[...] = jnp.full(m_s.shape, -jnp.inf, F32)
            acc[...] = jnp.zeros(acc.shape, F32)

    def step(masked):
        k = k_ref[...]
        vt = vt_ref[...]
        sts = [lax.dot_general(k, qs[...], (((1,), (1,)), ((), ())), preferred_element_type=F32)
               for qs, _, _ in maps]
        for st, (_, m_s, acc) in zip(sts, maps):
            if masked:
                key = lax.broadcasted_iota(jnp.int32, st.shape, 0)
                qry = lax.broadcasted_iota(jnp.int32, st.shape, 1)
                st = jnp.where(key <= qry, st, -jnp.inf)
            sb = st.astype(BF16)
            m_prev = m_s[...]
            m_new = jnp.maximum(m_prev, jnp.max(sb, axis=0, keepdims=True).astype(F32))
            alpha = jnp.exp2(m_prev - m_new)
            pt = jnp.exp2(sb - m_new.astype(BF16))
            acc[...] = alpha * acc[...] + jnp.dot(vt, pt, preferred_element_type=F32)
            m_s[...] = m_new

    @pl.when(ki < qi)
    def _full():
        step(False)

    @pl.when(ki == qi)
    def _diag():
        step(True)
        lp = lam_ref[...]
        s1 = jnp.sum(lp[0:1, :] * lp[1:2, :], axis=-1, keepdims=True)
        s2 = jnp.sum(lp[2:3, :] * lp[3:4, :], axis=-1, keepdims=True)
        lam = jnp.exp(s1) - jnp.exp(s2) + lam_init
        ot = (acc1[0:dv, :] / acc1[dv:dv + 1, :]
              - lam * (acc2[0:dv, :] / acc2[dv:dv + 1, :]))
        ot = ot * lax.rsqrt(jnp.mean(ot * ot, axis=0, keepdims=True) + EPS)
        o_ref[...] = (ot.T * g_ref[...] * (1.0 - lam_init)).astype(o_ref.dtype)


def _cast_rows(rows, nsteps):
    for rb in range(16, rows + 1, 16):
        if rows % rb == 0 and rows // rb <= nsteps:
            return rb
    raise ValueError(f"no row block for {rows} rows in {nsteps} steps")


def _attn(z, vt, lam_p, subln_g, lam_init, batch, seq, casts=()):
    n = z.shape[0]
    tq = ATTN_BLOCK
    dve = DIFF_V_DIM + ATTN_ONES_ROWS
    nq = seq // tq
    qcol = 2 * N_GMLP_HEADS * GMLP_HEAD_DIM // LANES
    kcol = qcol + N_DIFF_HEADS
    pairs = [(i, j) for i in range(nq) for j in range(i + 1)]
    npair = len(pairs)
    qi_tab = jnp.asarray([p[0] for p in pairs], jnp.int32)
    ki_tab = jnp.asarray([p[1] for p in pairs], jnp.int32)
    nsteps = batch * N_DIFF_HEADS * npair
    cast2d = [w.reshape(-1, w.shape[-1]) for w, _, _ in casts]
    cast_in, cast_out, cast_shape = [], [], []
    for w, i, pieces in casts:
        rows, cols = w.shape[1] * w.shape[2], w.shape[3]
        rb = _cast_rows(rows, nsteps)
        nb = rows // rb

        def blk(b, h, p, nb=nb):
            return jnp.minimum((b * N_DIFF_HEADS + h) * npair + p, nb - 1)
        cast_in.append(pl.BlockSpec(
            (rb, cols), lambda b, h, p, qt, kt, blk=blk, off=i * nb: (off + blk(b, h, p), 0)))
        cast_out.append(pl.BlockSpec(
            (pieces, rb, cols // pieces), lambda b, h, p, qt, kt, blk=blk: (0, blk(b, h, p), 0)))
        cast_shape.append((pieces, rows, cols // pieces))
    grid_spec = pltpu.PrefetchScalarGridSpec(
        num_scalar_prefetch=2,
        grid=(batch, N_DIFF_HEADS, npair),
        in_specs=[
            pl.BlockSpec((tq, LANES), lambda b, h, p, qt, kt: (b * nq + qt[p], qcol + h)),
            pl.BlockSpec((tq, LANES), lambda b, h, p, qt, kt: (b * nq + kt[p], kcol + h)),
            pl.BlockSpec((dve, tq), lambda b, h, p, qt, kt: (h, b * nq + kt[p])),
            pl.BlockSpec((4, DIFF_HEAD_DIM), lambda b, h, p, qt, kt: (0, 0)),
            pl.BlockSpec((1, DIFF_V_DIM), lambda b, h, p, qt, kt: (0, 0)),
        ] + cast_in,
        out_specs=[pl.BlockSpec((tq, LANES), lambda b, h, p, qt, kt: (b * nq + qt[p], h))]
        + cast_out,
        scratch_shapes=(2 * [pltpu.VMEM((tq, LANES), BF16)] + 2 * [pltpu.VMEM((1, tq), F32)]
                        + 2 * [pltpu.VMEM((dve, tq), F32)]),
    )
    outs = pl.pallas_call(
        functools.partial(_attn_body, tq=tq, lam_init=lam_init, ncast=len(casts)),
        grid_spec=grid_spec,
        out_shape=[jax.ShapeDtypeStruct((n, N_DIFF_HEADS * DIFF_V_DIM), BF16)]
        + [jax.ShapeDtypeStruct(s, BF16) for s in cast_shape],
        compiler_params=_params("arbitrary", "arbitrary", "arbitrary"),
        name="diff_attn",
    )(qi_tab, ki_tab, z, z, vt, lam_p, subln_g.reshape(1, DIFF_V_DIM), *cast2d)
    return outs[0], [o.reshape(pieces, w.shape[1], w.shape[2], w.shape[3] // pieces)
                     for o, (w, _, pieces) in zip(outs[1:], casts)]


def _outproj_body(a_ref, b_ref, w_ref, x_ref, mod_ref, o_ref):
    ka = a_ref.shape[1]
    y = jnp.dot(a_ref[...], w_ref[0:ka, :], preferred_element_type=F32)
    y = y + jnp.dot(b_ref[...], w_ref[ka:, :], preferred_element_type=F32)
    o_ref[...] = x_ref[...] + mod_ref[0, G1:G1 + 1, :] * y


def _outproj(out_a, out_b, w, x, mod, seq):
    n, d = x.shape
    tm = 512
    ka, kb = out_a.shape[1], out_b.shape[1]
    return pl.pallas_call(
        _outproj_body,
        grid=(n // tm,),
        in_specs=[pl.BlockSpec((tm, ka), lambda i: (i, 0)),
                  pl.BlockSpec((tm, kb), lambda i: (i, 0)),
                  pl.BlockSpec((ka + kb, d), lambda i: (0, 0)),
                  pl.BlockSpec((tm, d), lambda i: (i, 0)),
                  pl.BlockSpec((1, N_MOD, d), lambda i: (i * tm // seq, 0, 0))],
        out_specs=pl.BlockSpec((tm, d), lambda i: (i, 0)),
        out_shape=jax.ShapeDtypeStruct((n, d), F32),
        compiler_params=_params("parallel"),
        name="outproj",
    )(out_a, out_b, w, x, mod)


def _ffn_body(a_ref, b_ref, wo_ref, x_ref, mod_ref, g_ref, fg_ref, wg_ref, wu_ref, wd_ref, o_ref,
              h_s, acc, *, final):
    j = pl.program_id(1)

    @pl.when(j == 0)
    def _prologue():
        ka = a_ref.shape[1]
        y = jnp.dot(a_ref[...], wo_ref[0:ka, :], preferred_element_type=F32)
        y = y + jnp.dot(b_ref[...], wo_ref[ka:, :], preferred_element_type=F32)
        x = x_ref[...] + mod_ref[0, G1:G1 + 1, :] * y
        o_ref[...] = x
        h = _modnorm(x, g_ref[...], mod_ref[0, SC2:SC2 + 1, :], mod_ref[0, SH2:SH2 + 1, :])
        h_s[...] = h.astype(BF16)
        acc[...] = jnp.zeros(acc.shape, F32)

    h = h_s[...]
    g = jnp.dot(h, wg_ref[0].astype(BF16), preferred_element_type=F32)
    u = jnp.dot(h, wu_ref[0].astype(BF16), preferred_element_type=F32)
    a = (g * jax.nn.sigmoid(g) * u).astype(BF16)
    acc[...] += jnp.dot(a, wd_ref[0].astype(BF16), preferred_element_type=F32)

    @pl.when(j == pl.num_programs(1) - 1)
    def _epilogue():
        y = o_ref[...] + mod_ref[0, G2:G2 + 1, :] * acc[...]
        if final:
            y = y * lax.rsqrt(jnp.mean(y * y, axis=-1, keepdims=True) + EPS) * fg_ref[...]
        o_ref[...] = y


def _ffn(out_a, out_b, w_out, x, mod, g, final_g, w_gate, w_up, w_down, li, seq, final):
    n, d = x.shape
    f = w_gate.shape[2]
    ka, kb = out_a.shape[1], out_b.shape[1]
    tm = 1024
    tf = 256
    return pl.pallas_call(
        functools.partial(_ffn_body, final=final),
        grid=(n // tm, f // tf),
        in_specs=[pl.BlockSpec((tm, ka), lambda i, j: (i, 0)),
                  pl.BlockSpec((tm, kb), lambda i, j: (i, 0)),
                  pl.BlockSpec((ka + kb, d), lambda i, j: (0, 0)),
                  pl.BlockSpec((tm, d), lambda i, j: (i, 0)),
                  pl.BlockSpec((1, N_MOD, d), lambda i, j: (i * tm // seq, 0, 0)),
                  pl.BlockSpec((1, d), lambda i, j: (0, 0)),
                  pl.BlockSpec((1, d), lambda i, j: (0, 0)),
                  pl.BlockSpec((1, d, tf), lambda i, j: (li, 0, j)),
                  pl.BlockSpec((1, d, tf), lambda i, j: (li, 0, j)),
                  pl.BlockSpec((1, tf, d), lambda i, j: (li, j, 0))],
        out_specs=pl.BlockSpec((tm, d), lambda i, j: (i, 0)),
        out_shape=jax.ShapeDtypeStruct((n, d), F32),
        scratch_shapes=[pltpu.VMEM((tm, d), BF16), pltpu.VMEM((tm, d), F32)],
        compiler_params=_params("parallel", "arbitrary"),
        name="ffn_mixer",
    )(out_a, out_b, w_out, x, mod, g, final_g, w_gate, w_up, w_down)


def _top2(logits):
    lane = lax.broadcasted_iota(jnp.int32, logits.shape, 1)
    lg = jnp.where(lane < N_EXPERTS, logits, -jnp.inf)
    m1 = jnp.max(lg, axis=-1, keepdims=True)
    i1 = jnp.min(jnp.where(lg == m1, lane, LANES), axis=-1, keepdims=True)
    lg2 = jnp.where(lane == i1, -jnp.inf, lg)
    m2 = jnp.max(lg2, axis=-1, keepdims=True)
    i2 = jnp.min(jnp.where(lg2 == m2, lane, LANES), axis=-1, keepdims=True)
    e2 = jnp.exp(m2 - m1)
    den = 1.0 + e2
    return i1, i2, 1.0 / den, e2 / den


def _router_body(x_ref, mod_ref, g_ref, wr_ref, h_ref, pos_ref, post_ref, gate_ref, cnt_ref):
    t = x_ref.shape[0]
    h = _modnorm(x_ref[...], g_ref[...], mod_ref[0, SC2:SC2 + 1, :], mod_ref[0, SH2:SH2 + 1, :])
    h_hi = h.astype(BF16)
    h_ref[...] = h_hi
    h_lo = (h - h_hi.astype(F32)).astype(BF16)
    wr = wr_ref[...]
    w_hi = wr.astype(BF16)
    w_lo = (wr - w_hi.astype(F32)).astype(BF16)
    logits = (jnp.dot(h_hi, w_hi, preferred_element_type=F32)
              + jnp.dot(h_lo, w_hi, preferred_element_type=F32)
              + jnp.dot(h_hi, w_lo, preferred_element_type=F32))
    i1, i2, w1, w2 = _top2(logits)
    lane = lax.broadcasted_iota(jnp.int32, logits.shape, 1)
    sel1 = lane == i1
    sel2 = lane == i2
    gate_ref[...] = jnp.where(sel1, w1, 0.0) + jnp.where(sel2, w2, 0.0)
    picked = jnp.logical_or(sel1, sel2)
    onehot = jnp.where(picked, 1.0, 0.0)
    row = lax.broadcasted_iota(jnp.int32, (t, t), 0)
    col = lax.broadcasted_iota(jnp.int32, (t, t), 1)
    before = jnp.where(col < row, 1.0, 0.0).astype(BF16)
    rank = jnp.dot(before, onehot.astype(BF16), preferred_element_type=F32)
    pos = jnp.where(picked, rank, -1.0)
    pos_ref[...] = pos
    post_ref[...] = pos.T[0:N_EXPERTS, :]
    cnt_ref[0] = jnp.broadcast_to(jnp.sum(onehot, axis=0, keepdims=True), cnt_ref.shape[1:])


def _router(x, mod, g, w_router, seq):
    n, d = x.shape
    t = MOE_TILE
    nt = n // t
    wr = jnp.zeros((d, LANES), F32).at[:, :N_EXPERTS].set(w_router)
    return pl.pallas_call(
        _router_body,
        grid=(nt,),
        in_specs=[pl.BlockSpec((t, d), lambda i: (i, 0)),
                  pl.BlockSpec((1, N_MOD, d), lambda i: (i * t // seq, 0, 0)),
                  pl.BlockSpec((1, d), lambda i: (0, 0)),
                  pl.BlockSpec((d, LANES), lambda i: (0, 0))],
        out_specs=[pl.BlockSpec((t, d), lambda i: (i, 0)),
                   pl.BlockSpec((t, LANES), lambda i: (i, 0)),
                   pl.BlockSpec((N_EXPERTS, t), lambda i: (0, i)),
                   pl.BlockSpec((t, LANES), lambda i: (i, 0)),
                   pl.BlockSpec((1, 8, LANES), lambda i: (i, 0, 0))],
        out_shape=[jax.ShapeDtypeStruct((n, d), BF16),
                   jax.ShapeDtypeStruct((n, LANES), F32),
                   jax.ShapeDtypeStruct((N_EXPERTS, n), F32),
                   jax.ShapeDtypeStruct((n, LANES), F32),
                   jax.ShapeDtypeStruct((nt, 8, LANES), F32)],
        compiler_params=_params("parallel"),
        name="router",
    )(x, mod, g, wr)


def _moe_body(cnt_ref, h_ref, pos_ref, post_ref, gate_ref, x_ref, mod_ref, fg_ref,
              wg_ref, wu_ref, wd_ref, o_ref, xg, acc, *, final):
    ti = pl.program_id(0)
    e = pl.program_id(1)
    s = pl.program_id(2)
    ns = pl.num_programs(2)
    t = h_ref.shape[0]
    cnt = cnt_ref[ti, e]

    @pl.when(jnp.logical_and(e == 0, s == 0))
    def _zero():
        o_ref[...] = jnp.zeros(o_ref.shape, F32)

    def block(b, bm):
        if isinstance(b, int):
            rows = slice(b * bm, (b + 1) * bm)
        else:
            rows = pl.ds(pl.multiple_of(b * bm, 16), bm)

        @pl.when(s == 0)
        def _gather():
            tgt = (lax.broadcasted_iota(jnp.int32, (bm, t), 0) + b * bm).astype(F32)
            onehot = jnp.where(post_ref[pl.ds(e, 1), :] == tgt, 1.0, 0.0).astype(BF16)
            xg[rows, :] = jnp.dot(onehot, h_ref[...], preferred_element_type=F32).astype(BF16)

        xb = xg[rows, :]
        g = jnp.dot(xb, wg_ref[0, 0], preferred_element_type=F32)
        u = jnp.dot(xb, wu_ref[0, 0], preferred_element_type=F32)
        a = (g * jax.nn.sigmoid(g) * u).astype(BF16)
        y = jnp.dot(a, wd_ref[0], preferred_element_type=F32)

        @pl.when(s == 0)
        def _first():
            acc[rows, :] = y

        @pl.when(s != 0)
        def _rest():
            acc[rows, :] += y

        @pl.when(s == ns - 1)
        def _scatter():
            lane = lax.broadcasted_iota(jnp.int32, (t, LANES), 1)
            mine = lane == e
            pos_e = jnp.sum(jnp.where(mine, pos_ref[...], 0.0), axis=-1, keepdims=True)
            gate_e = jnp.sum(jnp.where(mine, gate_ref[...], 0.0), axis=-1, keepdims=True)
            tgt = (lax.broadcasted_iota(jnp.int32, (t, bm), 1) + b * bm).astype(F32)
            onehot = jnp.where(pos_e == tgt, 1.0, 0.0).astype(BF16)
            yb = jnp.dot(onehot, acc[rows, :].astype(BF16), preferred_element_type=F32)
            o_ref[...] += gate_e * yb

    lo = MOE_LOOP_BLOCK
    for bm in MOE_BLOCKS:
        @pl.when(jnp.logical_and(cnt > lo, cnt <= bm))
        def _single(bm=bm):
            block(0, bm)
        lo = bm

    @pl.when(jnp.logical_or(cnt <= MOE_LOOP_BLOCK, cnt > lo))
    def _multi():
        bm = MOE_LOOP_BLOCK

        def body(b, carry):
            block(b, bm)
            return carry
        lax.fori_loop(0, (cnt + (bm - 1)) // bm, body, 0)

    @pl.when(jnp.logical_and(e == pl.num_programs(1) - 1, s == ns - 1))
    def _epilogue():
        y = x_ref[...] + mod_ref[0, G2:G2 + 1, :] * o_ref[...]
        if final:
            y = y * lax.rsqrt(jnp.mean(y * y, axis=-1, keepdims=True) + EPS) * fg_ref[...]
        o_ref[...] = y


def _moe(x, mod, g, final_g, w_router, w_gate, w_up, w_down, seq, final):
    n, d = x.shape
    n_e, f, _ = w_down.shape
    t = MOE_TILE
    nt = n // t
    fs = f // MOE_FSPLIT
    h, pos, post, gate, cnt = _router(x, mod, g, w_router, seq)
    counts = cnt[:, 0, :N_EXPERTS].astype(jnp.int32)
    cap = max(pl.cdiv(t, MOE_LOOP_BLOCK) * MOE_LOOP_BLOCK, MOE_BLOCKS[-1])
    grid_spec = pltpu.PrefetchScalarGridSpec(
        num_scalar_prefetch=1,
        grid=(nt, n_e, MOE_FSPLIT),
        in_specs=[pl.BlockSpec((t, d), lambda i, e, s, c: (i, 0)),
                  pl.BlockSpec((t, LANES), lambda i, e, s, c: (i, 0)),
                  pl.BlockSpec((N_EXPERTS, t), lambda i, e, s, c: (0, i)),
                  pl.BlockSpec((t, LANES), lambda i, e, s, c: (i, 0)),
                  pl.BlockSpec((t, d), lambda i, e, s, c: (i, 0)),
                  pl.BlockSpec((1, N_MOD, d), lambda i, e, s, c: (i * t // seq, 0, 0)),
                  pl.BlockSpec((1, d), lambda i, e, s, c: (0, 0)),
                  pl.BlockSpec((1, 1, d, fs), lambda i, e, s, c: (s, e, 0, 0)),
                  pl.BlockSpec((1, 1, d, fs), lambda i, e, s, c: (s, e, 0, 0)),
                  pl.BlockSpec((1, fs, d), lambda i, e, s, c: (e, s, 0))],
        out_specs=pl.BlockSpec((t, d), lambda i, e, s, c: (i, 0)),
        scratch_shapes=[pltpu.VMEM((cap, d), BF16), pltpu.VMEM((cap, d), F32)],
    )
    return pl.pallas_call(
        functools.partial(_moe_body, final=final),
        grid_spec=grid_spec,
        out_shape=jax.ShapeDtypeStruct((n, d), F32),
        compiler_params=_params("parallel", "arbitrary", "arbitrary"),
        name="moe_mixer",
    )(counts, h, pos, post, gate, x, mod, final_g, w_gate, w_up, w_down)


def kernel(x, c, w_ada, b_ada, norm1_g, norm2_g, w_in, w_out, gmlp_vnorm_g, gmlp_ws, gmlp_bs,
           lam_q1, lam_k1, lam_q2, lam_k2, diff_subln_g, ffn_w_gate, ffn_w_up, ffn_w_down,
           w_router, moe_w_gate, moe_w_up, moe_w_down, final_g):
    batch, seq, d = x.shape
    depth = w_ada.shape[0]
    n = batch * seq
    mods = _ada(c, w_ada, b_ada)
    xf = x.reshape(n, d)
    fg = final_g.reshape(1, d)
    for l in range(depth):
        mod = mods[l]
        z, vt = _inproj(xf, mod, norm1_g[l].reshape(1, d), w_in[l].astype(BF16), seq)
        out_a = _gmlp(z, gmlp_vnorm_g[l], gmlp_ws[l], gmlp_bs[l])
        lam_init = 0.8 - 0.6 * math.exp(-0.3 * l)
        lam_p = jnp.stack([lam_q1[l], lam_k1[l], lam_q2[l], lam_k2[l]])
        i = l // 2
        if l % 2 == 0:
            casts = ([(moe_w_gate, i, MOE_FSPLIT), (moe_w_up, i, MOE_FSPLIT)]
                     if l + 1 < depth else [])
        else:
            casts = [(moe_w_down, i, 1)]
        out_b, cast_out = _attn(z, vt, lam_p, diff_subln_g[l], lam_init, batch, seq, casts)
        final = l == depth - 1
        g2 = norm2_g[l].reshape(1, d)
        if l % 2 == 0:
            xf = _ffn(out_a, out_b, w_out[l].astype(BF16), xf, mod, g2, fg,
                      ffn_w_gate, ffn_w_up, ffn_w_down, i, seq, final)
            moe_gate_up = cast_out
        else:
            xf = _outproj(out_a, out_b, w_out[l].astype(BF16), xf, mod, seq)
            xf = _moe(xf, mod, g2, fg, w_router[i], moe_gate_up[0], moe_gate_up[1], cast_out[0][0],
                      seq, final)
    return xf.reshape(batch, seq, d)
```

```python
import functools
import math

import jax
import jax.numpy as jnp
import numpy as np
from jax import lax
from jax.experimental import pallas as pl
from jax.experimental.pallas import tpu as pltpu

F32 = jnp.float32
BF16 = jnp.bfloat16
EPS = 1e-6
N_MOD = 6
N_GMLP_HEADS = 8
GMLP_HEAD_DIM = 64
CHUNK = 128
N_DIFF_HEADS = 4
DIFF_HEAD_DIM = 64
DIFF_V_DIM = 128
N_EXPERTS = 8
LANES = 128
VMEM_LIMIT = 56 * 1024 * 1024

MOE_TILE = 1024
MOE_BLOCKS = (192, 256, 320)
MOE_LOOP_BLOCK = 128
MOE_FSPLIT = 2

ATTN_BLOCK = 1024
ATTN_ONES_ROWS = 16

SH1, SC1, G1, SH2, SC2, G2 = range(N_MOD)


def _params(*sem):
    return pltpu.CompilerParams(dimension_semantics=sem, vmem_limit_bytes=VMEM_LIMIT)


def _modnorm(x, g, sc, sh):
    ms = jnp.mean(x * x, axis=-1, keepdims=True)
    return (x * lax.rsqrt(ms + EPS)) * g * (1.0 + sc) + sh


def _ada_body(c_ref, w_ref, b_ref, o_ref):
    c = c_ref[...]
    sc = c * jax.nn.sigmoid(c)
    o_ref[0] = jnp.dot(sc, w_ref[0], preferred_element_type=F32,
                       precision=lax.Precision.HIGHEST) + b_ref[0]


def _ada(c, w_ada, b_ada):
    depth, d, m = w_ada.shape
    b = c.shape[0]
    rows = 8
    cp = jnp.zeros((rows, d), F32).at[:b].set(c)
    tn = 3072
    out = pl.pallas_call(
        _ada_body,
        grid=(depth, m // tn),
        in_specs=[pl.BlockSpec((rows, d), lambda l, j: (0, 0)),
                  pl.BlockSpec((1, d, tn), lambda l, j: (l, 0, j)),
                  pl.BlockSpec((1, 1, tn), lambda l, j: (l, 0, j))],
        out_specs=pl.BlockSpec((1, rows, tn), lambda l, j: (l, 0, j)),
        out_shape=jax.ShapeDtypeStruct((depth, rows, m), F32),
        compiler_params=_params("parallel", "parallel"),
        name="ada",
    )(cp, w_ada, b_ada.reshape(depth, 1, m))
    return out[:, :b].reshape(depth, b, N_MOD, d)


def _inproj_body(x_ref, mod_ref, g_ref, w_ref, wvt_ref, z_ref, vt_ref):
    h = _modnorm(x_ref[...], g_ref[...], mod_ref[0, SC1:SC1 + 1, :],
                 mod_ref[0, SH1:SH1 + 1, :]).astype(BF16)
    tm = x_ref.shape[0]
    zw = z_ref.shape[1]
    cw = 512
    for c0 in range(0, zw, cw):
        z_ref[:, c0:c0 + cw] = jnp.dot(
            h, w_ref[:, c0:c0 + cw], preferred_element_type=F32).astype(z_ref.dtype)
    dve = DIFF_V_DIM + ATTN_ONES_ROWS
    vt = lax.dot_general(wvt_ref[...], h, (((1,), (1,)), ((), ())),
                         preferred_element_type=F32).astype(vt_ref.dtype)
    for hd in range(N_DIFF_HEADS):
        vt_ref[hd * dve:hd * dve + DIFF_V_DIM, :] = vt[hd * DIFF_V_DIM:(hd + 1) * DIFF_V_DIM, :]
        vt_ref[hd * dve + DIFF_V_DIM:(hd + 1) * dve, :] = jnp.ones((ATTN_ONES_ROWS, tm), vt_ref.dtype)


def _inproj(x, mod, g, w, seq):
    n, d = x.shape
    vw = N_DIFF_HEADS * DIFF_V_DIM
    zw = w.shape[1] - vw
    dve = DIFF_V_DIM + ATTN_ONES_ROWS
    tm = 512
    return pl.pallas_call(
        _inproj_body,
        grid=(n // tm,),
        in_specs=[pl.BlockSpec((tm, d), lambda i: (i, 0)),
                  pl.BlockSpec((1, N_MOD, d), lambda i: (i * tm // seq, 0, 0)),
                  pl.BlockSpec((1, d), lambda i: (0, 0)),
                  pl.BlockSpec((d, zw), lambda i: (0, 0)),
                  pl.BlockSpec((vw, d), lambda i: (0, 0))],
        out_specs=[pl.BlockSpec((tm, zw), lambda i: (i, 0)),
                   pl.BlockSpec((N_DIFF_HEADS * dve, tm), lambda i: (0, i))],
        out_shape=[jax.ShapeDtypeStruct((n, zw), BF16),
                   jax.ShapeDtypeStruct((N_DIFF_HEADS * dve, n), BF16)],
        compiler_params=_params("parallel"),
        name="inproj",
    )(x, mod, g, w, w[:, zw:].T)


def _gmlp_body(u_ref, v_ref, vg_ref, w_ref, bias_ref, gm_ref, o_ref, *, nck):
    width = u_ref.shape[1]
    npair = width // LANES
    row = lax.broadcasted_iota(jnp.int32, (CHUNK, 2 * CHUNK), 0)
    col = lax.broadcasted_iota(jnp.int32, (CHUNK, 2 * CHUNK), 1)
    causal = jnp.where(col >= CHUNK, col - CHUNK, col) <= row
    lane = lax.broadcasted_iota(jnp.int32, (CHUNK, LANES), 1)
    left = lane < GMLP_HEAD_DIM
    wp = [jnp.where(causal, w_ref[p], jnp.zeros_like(w_ref[p])) for p in range(npair)]
    gm = gm_ref[...]
    for c in range(nck):
        rs = slice(c * CHUNK, (c + 1) * CHUNK)
        gu = jax.nn.gelu(u_ref[rs, :].astype(F32))
        gv = jax.nn.gelu(v_ref[rs, :].astype(F32))
        v2 = gv * gv
        hi = v2.astype(BF16)
        lo = (v2 - hi.astype(F32)).astype(BF16)
        ms = (jnp.dot(hi, gm, preferred_element_type=F32)
              + jnp.dot(lo, gm, preferred_element_type=F32))
        vn = gv * lax.rsqrt(ms + EPS) * vg_ref[...]
        outs = []
        for p in range(npair):
            vp = vn[:, p * LANES:(p + 1) * LANES]
            zero = jnp.zeros_like(vp)
            rhs = jnp.concatenate([jnp.where(left, vp, zero), jnp.where(left, zero, vp)],
                                  axis=0).astype(BF16)
            outs.append(jnp.dot(wp[p], rhs, preferred_element_type=F32))
        vm = jnp.concatenate(outs, axis=1) + bias_ref[...]
        o_ref[rs, :] = (gu * vm).astype(o_ref.dtype)


def _gmlp(z, vnorm_g, w_s, b_s):
    n = z.shape[0]
    width = N_GMLP_HEADS * GMLP_HEAD_DIM
    nck = 4
    rows = nck * CHUNK
    npair = N_GMLP_HEADS // 2
    w_pair = w_s.reshape(npair, 2, CHUNK, CHUNK).transpose(0, 2, 1, 3).reshape(
        npair, CHUNK, 2 * CHUNK).astype(BF16)
    bias = jnp.repeat(b_s.T, GMLP_HEAD_DIM, axis=1)
    grp = np.arange(width) // GMLP_HEAD_DIM
    group_mean = jnp.asarray((grp[:, None] == grp[None, :]) / GMLP_HEAD_DIM, dtype=BF16)
    return pl.pallas_call(
        functools.partial(_gmlp_body, nck=nck),
        grid=(n // rows,),
        in_specs=[pl.BlockSpec((rows, width), lambda i: (i, 0)),
                  pl.BlockSpec((rows, width), lambda i: (i, 1)),
                  pl.BlockSpec((1, width), lambda i: (0, 0)),
                  pl.BlockSpec((npair, CHUNK, 2 * CHUNK), lambda i: (0, 0, 0)),
                  pl.BlockSpec((CHUNK, width), lambda i: (0, 0)),
                  pl.BlockSpec((width, width), lambda i: (0, 0))],
        out_specs=pl.BlockSpec((rows, width), lambda i: (i, 0)),
        out_shape=jax.ShapeDtypeStruct((n, width), BF16),
        compiler_params=_params("parallel"),
        name="gmlp",
    )(z, z, vnorm_g.reshape(1, width), w_pair, bias, group_mean)


def _attn_body(*refs, tq, lam_init, ncast):
    qi_ref, ki_ref, q_ref, k_ref, vt_ref, lam_ref, g_ref = refs[:7]
    cast_src = refs[7:7 + ncast]
    o_ref = refs[7 + ncast]
    cast_dst = refs[8 + ncast:8 + 2 * ncast]
    qs1, qs2, m1, m2, acc1, acc2 = refs[8 + 2 * ncast:]
    for src, dst in zip(cast_src, cast_dst):
        pw = dst.shape[2]
        for j in range(dst.shape[0]):
            dst[j] = src[:, j * pw:(j + 1) * pw].astype(dst.dtype)
    qi = qi_ref[pl.program_id(2)]
    ki = ki_ref[pl.program_id(2)]
    maps = ((qs1, m1, acc1), (qs2, m2, acc2))
    dv = DIFF_V_DIM

    @pl.when(ki == 0)
    def _init():
        qsc = (q_ref[...].astype(F32) * (DIFF_HEAD_DIM ** -0.5 * math.log2(math.e))).astype(BF16)
        lane = lax.broadcasted_iota(jnp.int32, qsc.shape, 1)
        zero = jnp.zeros_like(qsc)
        qs1[...] = jnp.where(lane < DIFF_HEAD_DIM, qsc, zero)
        qs2[...] = jnp.where(lane < DIFF_HEAD_DIM, zero, qsc)
        for _, m_s, acc in maps:
            m_s[...] = jnp.full(m_s.shape, -jnp.inf, F32)
            acc[...] = jnp.zeros(acc.shape, F32)

    def step(masked):
        k = k_ref[...]
        vt = vt_ref[...]
        sts = [lax.dot_general(k, qs[...], (((1,), (1,)), ((), ())), preferred_element_type=F32)
               for qs, _, _ in maps]
        for st, (_, m_s, acc) in zip(sts, maps):
            if masked:
                key = lax.broadcasted_iota(jnp.int32, st.shape, 0)
                qry = lax.broadcasted_iota(jnp.int32, st.shape, 1)
                st = jnp.where(key <= qry, st, -jnp.inf)
            sb = st.astype(BF16)
            m_prev = m_s[...]
            m_new = jnp.maximum(m_prev, jnp.max(sb, axis=0, keepdims=True).astype(F32))
            alpha = jnp.exp2(m_prev - m_new)
            pt = jnp.exp2(sb - m_new.astype(BF16))
            acc[...] = alpha * acc[...] + jnp.dot(vt, pt, preferred_element_type=F32)
            m_s[...] = m_new

    @pl.when(ki < qi)
    def _full():
        step(False)

    @pl.when(ki == qi)
    def _diag():
        step(True)
        lp = lam_ref[...]
        s1 = jnp.sum(lp[0:1, :] * lp[1:2, :], axis=-1, keepdims=True)
        s2 = jnp.sum(lp[2:3, :] * lp[3:4, :], axis=-1, keepdims=True)
        lam = jnp.exp(s1) - jnp.exp(s2) + lam_init
        ot = (acc1[0:dv, :] / acc1[dv:dv + 1, :]
              - lam * (acc2[0:dv, :] / acc2[dv:dv + 1, :]))
        ot = ot * lax.rsqrt(jnp.mean(ot * ot, axis=0, keepdims=True) + EPS)
        o_ref[...] = (ot.T * g_ref[...] * (1.0 - lam_init)).astype(o_ref.dtype)


def _cast_rows(rows, nsteps):
    for rb in range(16, rows + 1, 16):
        if rows % rb == 0 and rows // rb <= nsteps:
            return rb
    raise ValueError(f"no row block for {rows} rows in {nsteps} steps")


def _attn(z, vt, lam_p, subln_g, lam_init, batch, seq, casts=()):
    n = z.shape[0]
    tq = ATTN_BLOCK
    dve = DIFF_V_DIM + ATTN_ONES_ROWS
    nq = seq // tq
    qcol = 2 * N_GMLP_HEADS * GMLP_HEAD_DIM // LANES
    kcol = qcol + N_DIFF_HEADS
    pairs = [(i, j) for i in range(nq) for j in range(i + 1)]
    npair = len(pairs)
    qi_tab = jnp.asarray([p[0] for p in pairs], jnp.int32)
    ki_tab = jnp.asarray([p[1] for p in pairs], jnp.int32)
    nsteps = batch * N_DIFF_HEADS * npair
    cast2d = [w.reshape(-1, w.shape[-1]) for w, _, _ in casts]
    cast_in, cast_out, cast_shape = [], [], []
    for w, i, pieces in casts:
        rows, cols = w.shape[1] * w.shape[2], w.shape[3]
        rb = _cast_rows(rows, nsteps)
        nb = rows // rb

        def blk(b, h, p, nb=nb):
            return jnp.minimum((b * N_DIFF_HEADS + h) * npair + p, nb - 1)
        cast_in.append(pl.BlockSpec(
            (rb, cols), lambda b, h, p, qt, kt, blk=blk, off=i * nb: (off + blk(b, h, p), 0)))
        cast_out.append(pl.BlockSpec(
            (pieces, rb, cols // pieces), lambda b, h, p, qt, kt, blk=blk: (0, blk(b, h, p), 0)))
        cast_shape.append((pieces, rows, cols // pieces))
    grid_spec = pltpu.PrefetchScalarGridSpec(
        num_scalar_prefetch=2,
        grid=(batch, N_DIFF_HEADS, npair),
        in_specs=[
            pl.BlockSpec((tq, LANES), lambda b, h, p, qt, kt: (b * nq + qt[p], qcol + h)),
            pl.BlockSpec((tq, LANES), lambda b, h, p, qt, kt: (b * nq + kt[p], kcol + h)),
            pl.BlockSpec((dve, tq), lambda b, h, p, qt, kt: (h, b * nq + kt[p])),
            pl.BlockSpec((4, DIFF_HEAD_DIM), lambda b, h, p, qt, kt: (0, 0)),
            pl.BlockSpec((1, DIFF_V_DIM), lambda b, h, p, qt, kt: (0, 0)),
        ] + cast_in,
        out_specs=[pl.BlockSpec((tq, LANES), lambda b, h, p, qt, kt: (b * nq + qt[p], h))]
        + cast_out,
        scratch_shapes=(2 * [pltpu.VMEM((tq, LANES), BF16)] + 2 * [pltpu.VMEM((1, tq), F32)]
                        + 2 * [pltpu.VMEM((dve, tq), F32)]),
    )
    outs = pl.pallas_call(
        functools.partial(_attn_body, tq=tq, lam_init=lam_init, ncast=len(casts)),
        grid_spec=grid_spec,
        out_shape=[jax.ShapeDtypeStruct((n, N_DIFF_HEADS * DIFF_V_DIM), BF16)]
        + [jax.ShapeDtypeStruct(s, BF16) for s in cast_shape],
        compiler_params=_params("arbitrary", "arbitrary", "arbitrary"),
        name="diff_attn",
    )(qi_tab, ki_tab, z, z, vt, lam_p, subln_g.reshape(1, DIFF_V_DIM), *cast2d)
    return outs[0], [o.reshape(pieces, w.shape[1], w.shape[2], w.shape[3] // pieces)
                     for o, (w, _, pieces) in zip(outs[1:], casts)]


def _outproj_body(a_ref, b_ref, w_ref, x_ref, mod_ref, o_ref):
    ka = a_ref.shape[1]
    y = jnp.dot(a_ref[...], w_ref[0:ka, :], preferred_element_type=F32)
    y = y + jnp.dot(b_ref[...], w_ref[ka:, :], preferred_element_type=F32)
    o_ref[...] = x_ref[...] + mod_ref[0, G1:G1 + 1, :] * y


def _outproj(out_a, out_b, w, x, mod, seq):
    n, d = x.shape
    tm = 512
    ka, kb = out_a.shape[1], out_b.shape[1]
    return pl.pallas_call(
        _outproj_body,
        grid=(n // tm,),
        in_specs=[pl.BlockSpec((tm, ka), lambda i: (i, 0)),
                  pl.BlockSpec((tm, kb), lambda i: (i, 0)),
                  pl.BlockSpec((ka + kb, d), lambda i: (0, 0)),
                  pl.BlockSpec((tm, d), lambda i: (i, 0)),
                  pl.BlockSpec((1, N_MOD, d), lambda i: (i * tm // seq, 0, 0))],
        out_specs=pl.BlockSpec((tm, d), lambda i: (i, 0)),
        out_shape=jax.ShapeDtypeStruct((n, d), F32),
        compiler_params=_params("parallel"),
        name="outproj",
    )(out_a, out_b, w, x, mod)


def _ffn_body(a_ref, b_ref, wo_ref, x_ref, mod_ref, g_ref, fg_ref, wg_ref, wu_ref, wd_ref, o_ref,
              h_s, acc, *, final):
    j = pl.program_id(1)

    @pl.when(j == 0)
    def _prologue():
        ka = a_ref.shape[1]
        y = jnp.dot(a_ref[...], wo_ref[0:ka, :], preferred_element_type=F32)
        y = y + jnp.dot(b_ref[...], wo_ref[ka:, :], preferred_element_type=F32)
        x = x_ref[...] + mod_ref[0, G1:G1 + 1, :] * y
        o_ref[...] = x
        h = _modnorm(x, g_ref[...], mod_ref[0, SC2:SC2 + 1, :], mod_ref[0, SH2:SH2 + 1, :])
        h_s[...] = h.astype(BF16)
        acc[...] = jnp.zeros(acc.shape, F32)

    h = h_s[...]
    g = jnp.dot(h, wg_ref[0].astype(BF16), preferred_element_type=F32)
    u = jnp.dot(h, wu_ref[0].astype(BF16), preferred_element_type=F32)
    a = (g * jax.nn.sigmoid(g) * u).astype(BF16)
    acc[...] += jnp.dot(a, wd_ref[0].astype(BF16), preferred_element_type=F32)

    @pl.when(j == pl.num_programs(1) - 1)
    def _epilogue():
        y = o_ref[...] + mod_ref[0, G2:G2 + 1, :] * acc[...]
        if final:
            y = y * lax.rsqrt(jnp.mean(y * y, axis=-1, keepdims=True) + EPS) * fg_ref[...]
        o_ref[...] = y


def _ffn(out_a, out_b, w_out, x, mod, g, final_g, w_gate, w_up, w_down, li, seq, final):
    n, d = x.shape
    f = w_gate.shape[2]
    ka, kb = out_a.shape[1], out_b.shape[1]
    tm = 1024
    tf = 256
    return pl.pallas_call(
        functools.partial(_ffn_body, final=final),
        grid=(n // tm, f // tf),
        in_specs=[pl.BlockSpec((tm, ka), lambda i, j: (i, 0)),
                  pl.BlockSpec((tm, kb), lambda i, j: (i, 0)),
                  pl.BlockSpec((ka + kb, d), lambda i, j: (0, 0)),
                  pl.BlockSpec((tm, d), lambda i, j: (i, 0)),
                  pl.BlockSpec((1, N_MOD, d), lambda i, j: (i * tm // seq, 0, 0)),
                  pl.BlockSpec((1, d), lambda i, j: (0, 0)),
                  pl.BlockSpec((1, d), lambda i, j: (0, 0)),
                  pl.BlockSpec((1, d, tf), lambda i, j: (li, 0, j)),
                  pl.BlockSpec((1, d, tf), lambda i, j: (li, 0, j)),
                  pl.BlockSpec((1, tf, d), lambda i, j: (li, j, 0))],
        out_specs=pl.BlockSpec((tm, d), lambda i, j: (i, 0)),
        out_shape=jax.ShapeDtypeStruct((n, d), F32),
        scratch_shapes=[pltpu.VMEM((tm, d), BF16), pltpu.VMEM((tm, d), F32)],
        compiler_params=_params("parallel", "arbitrary"),
        name="ffn_mixer",
    )(out_a, out_b, w_out, x, mod, g, final_g, w_gate, w_up, w_down)


def _top2(logits):
    lane = lax.broadcasted_iota(jnp.int32, logits.shape, 1)
    lg = jnp.where(lane < N_EXPERTS, logits, -jnp.inf)
    m1 = jnp.max(lg, axis=-1, keepdims=True)
    i1 = jnp.min(jnp.where(lg == m1, lane, LANES), axis=-1, keepdims=True)
    lg2 = jnp.where(lane == i1, -jnp.inf, lg)
    m2 = jnp.max(lg2, axis=-1, keepdims=True)
    i2 = jnp.min(jnp.where(lg2 == m2, lane, LANES), axis=-1, keepdims=True)
    e2 = jnp.exp(m2 - m1)
    den = 1.0 + e2
    return i1, i2, 1.0 / den, e2 / den


def _router_body(x_ref, mod_ref, g_ref, wr_ref, h_ref, post_ref, gatet_ref, cnt_ref):
    t = x_ref.shape[0]
    h = _modnorm(x_ref[...], g_ref[...], mod_ref[0, SC2:SC2 + 1, :], mod_ref[0, SH2:SH2 + 1, :])
    h_hi = h.astype(BF16)
    h_ref[...] = h_hi
    h_lo = (h - h_hi.astype(F32)).astype(BF16)
    wr = wr_ref[...]
    w_hi = wr.astype(BF16)
    w_lo = (wr - w_hi.astype(F32)).astype(BF16)
    logits = (jnp.dot(h_hi, w_hi, preferred_element_type=F32)
              + jnp.dot(h_lo, w_hi, preferred_element_type=F32)
              + jnp.dot(h_hi, w_lo, preferred_element_type=F32))
    i1, i2, w1, w2 = _top2(logits)
    lane = lax.broadcasted_iota(jnp.int32, logits.shape, 1)
    sel1 = lane == i1
    sel2 = lane == i2
    gate = jnp.where(sel1, w1, 0.0) + jnp.where(sel2, w2, 0.0)
    gatet_ref[...] = gate.T[0:N_EXPERTS, :]
    picked = jnp.logical_or(sel1, sel2)
    onehot = jnp.where(picked, 1.0, 0.0)
    row = lax.broadcasted_iota(jnp.int32, (t, t), 0)
    col = lax.broadcasted_iota(jnp.int32, (t, t), 1)
    before = jnp.where(col < row, 1.0, 0.0).astype(BF16)
    rank = jnp.dot(before, onehot.astype(BF16), preferred_element_type=F32)
    pos = jnp.where(picked, rank, -1.0)
    post_ref[...] = pos.T[0:N_EXPERTS, :]
    cnt_ref[0] = jnp.broadcast_to(jnp.sum(onehot, axis=0, keepdims=True), cnt_ref.shape[1:])


def _router(x, mod, g, w_router, seq):
    n, d = x.shape
    t = MOE_TILE
    nt = n // t
    wr = jnp.zeros((d, LANES), F32).at[:, :N_EXPERTS].set(w_router)
    return pl.pallas_call(
        _router_body,
        grid=(nt,),
        in_specs=[pl.BlockSpec((t, d), lambda i: (i, 0)),
                  pl.BlockSpec((1, N_MOD, d), lambda i: (i * t // seq, 0, 0)),
                  pl.BlockSpec((1, d), lambda i: (0, 0)),
                  pl.BlockSpec((d, LANES), lambda i: (0, 0))],
        out_specs=[pl.BlockSpec((t, d), lambda i: (i, 0)),
                   pl.BlockSpec((N_EXPERTS, t), lambda i: (0, i)),
                   pl.BlockSpec((N_EXPERTS, t), lambda i: (0, i)),
                   pl.BlockSpec((1, 8, LANES), lambda i: (i, 0, 0))],
        out_shape=[jax.ShapeDtypeStruct((n, d), BF16),
                   jax.ShapeDtypeStruct((N_EXPERTS, n), F32),
                   jax.ShapeDtypeStruct((N_EXPERTS, n), F32),
                   jax.ShapeDtypeStruct((nt, 8, LANES), F32)],
        compiler_params=_params("parallel"),
        name="router",
    )(x, mod, g, wr)


def _moe_body(cnt_ref, h_ref, post_ref, gatet_ref, x_ref, mod_ref, fg_ref,
              wg_ref, wu_ref, wd_ref, o_ref, xg, acc, *, final):
    ti = pl.program_id(0)
    e = pl.program_id(1)
    s = pl.program_id(2)
    ns = pl.num_programs(2)
    t = h_ref.shape[0]
    cnt = cnt_ref[ti, e]

    @pl.when(jnp.logical_and(e == 0, s == 0))
    def _zero():
        o_ref[...] = jnp.zeros(o_ref.shape, F32)

    def block(b, bm):
        if isinstance(b, int):
            rows = slice(b * bm, (b + 1) * bm)
        else:
            rows = pl.ds(pl.multiple_of(b * bm, 16), bm)

        @pl.when(s == 0)
        def _gather():
            tgt = (lax.broadcasted_iota(jnp.int32, (bm, t), 0) + b * bm).astype(F32)
            onehot = jnp.where(post_ref[pl.ds(e, 1), :] == tgt, 1.0, 0.0).astype(BF16)
            xg[rows, :] = jnp.dot(onehot, h_ref[...], preferred_element_type=F32).astype(BF16)

        xb = xg[rows, :]
        g = jnp.dot(xb, wg_ref[0, 0], preferred_element_type=F32)
        u = jnp.dot(xb, wu_ref[0, 0], preferred_element_type=F32)
        a = (g * jax.nn.sigmoid(g) * u).astype(BF16)
        y = jnp.dot(a, wd_ref[0], preferred_element_type=F32)

        @pl.when(s == 0)
        def _first():
            acc[rows, :] = y

        @pl.when(s != 0)
        def _rest():
            acc[rows, :] += y

        @pl.when(s == ns - 1)
        def _scatter():
            tgt = (lax.broadcasted_iota(jnp.int32, (bm, t), 0) + b * bm).astype(F32)
            hit = post_ref[pl.ds(e, 1), :] == tgt
            gate_rows = jnp.sum(jnp.where(hit, gatet_ref[pl.ds(e, 1), :], 0.0), axis=-1,
                                keepdims=True)
            yb = (acc[rows, :] * gate_rows).astype(BF16)
            o_ref[...] += lax.dot_general(jnp.where(hit, 1.0, 0.0).astype(BF16), yb,
                                          (((0,), (0,)), ((), ())), preferred_element_type=F32)

    lo = MOE_LOOP_BLOCK
    for bm in MOE_BLOCKS:
        @pl.when(jnp.logical_and(cnt > lo, cnt <= bm))
        def _single(bm=bm):
            block(0, bm)
        lo = bm

    @pl.when(jnp.logical_or(cnt <= MOE_LOOP_BLOCK, cnt > lo))
    def _multi():
        bm = MOE_LOOP_BLOCK

        def body(b, carry):
            block(b, bm)
            return carry
        lax.fori_loop(0, (cnt + (bm - 1)) // bm, body, 0)

    @pl.when(jnp.logical_and(e == pl.num_programs(1) - 1, s == ns - 1))
    def _epilogue():
        y = x_ref[...] + mod_ref[0, G2:G2 + 1, :] * o_ref[...]
        if final:
            y = y * lax.rsqrt(jnp.mean(y * y, axis=-1, keepdims=True) + EPS) * fg_ref[...]
        o_ref[...] = y


def _moe(x, mod, g, final_g, w_router, w_gate, w_up, w_down, seq, final):
    n, d = x.shape
    n_e, f, _ = w_down.shape
    t = MOE_TILE
    nt = n // t
    fs = f // MOE_FSPLIT
    h, post, gatet, cnt = _router(x, mod, g, w_router, seq)
    counts = cnt[:, 0, :N_EXPERTS].astype(jnp.int32)
    cap = max(pl.cdiv(t, MOE_LOOP_BLOCK) * MOE_LOOP_BLOCK, MOE_BLOCKS[-1])
    grid_spec = pltpu.PrefetchScalarGridSpec(
        num_scalar_prefetch=1,
        grid=(nt, n_e, MOE_FSPLIT),
        in_specs=[pl.BlockSpec((t, d), lambda i, e, s, c: (i, 0)),
                  pl.BlockSpec((N_EXPERTS, t), lambda i, e, s, c: (0, i)),
                  pl.BlockSpec((N_EXPERTS, t), lambda i, e, s, c: (0, i)),
                  pl.BlockSpec((t, d), lambda i, e, s, c: (i, 0)),
                  pl.BlockSpec((1, N_MOD, d), lambda i, e, s, c: (i * t // seq, 0, 0)),
                  pl.BlockSpec((1, d), lambda i, e, s, c: (0, 0)),
                  pl.BlockSpec((1, 1, d, fs), lambda i, e, s, c: (s, e, 0, 0)),
                  pl.BlockSpec((1, 1, d, fs), lambda i, e, s, c: (s, e, 0, 0)),
                  pl.BlockSpec((1, fs, d), lambda i, e, s, c: (e, s, 0))],
        out_specs=pl.BlockSpec((t, d), lambda i, e, s, c: (i, 0)),
        scratch_shapes=[pltpu.VMEM((cap, d), BF16), pltpu.VMEM((cap, d), F32)],
    )
    return pl.pallas_call(
        functools.partial(_moe_body, final=final),
        grid_spec=grid_spec,
        out_shape=jax.ShapeDtypeStruct((n, d), F32),
        compiler_params=_params("parallel", "arbitrary", "arbitrary"),
        name="moe_mixer",
    )(counts, h, post, gatet, x, mod, final_g, w_gate, w_up, w_down)


def kernel(x, c, w_ada, b_ada, norm1_g, norm2_g, w_in, w_out, gmlp_vnorm_g, gmlp_ws, gmlp_bs,
           lam_q1, lam_k1, lam_q2, lam_k2, diff_subln_g, ffn_w_gate, ffn_w_up, ffn_w_down,
           w_router, moe_w_gate, moe_w_up, moe_w_down, final_g):
    batch, seq, d = x.shape
    depth = w_ada.shape[0]
    n = batch * seq
    mods = _ada(c, w_ada, b_ada)
    xf = x.reshape(n, d)
    fg = final_g.reshape(1, d)
    for l in range(depth):
        mod = mods[l]
        z, vt = _inproj(xf, mod, norm1_g[l].reshape(1, d), w_in[l].astype(BF16), seq)
        out_a = _gmlp(z, gmlp_vnorm_g[l], gmlp_ws[l], gmlp_bs[l])
        lam_init = 0.8 - 0.6 * math.exp(-0.3 * l)
        lam_p = jnp.stack([lam_q1[l], lam_k1[l], lam_q2[l], lam_k2[l]])
        i = l // 2
        if l % 2 == 0:
            casts = ([(moe_w_gate, i, MOE_FSPLIT), (moe_w_up, i, MOE_FSPLIT)]
                     if l + 1 < depth else [])
        else:
            casts = [(moe_w_down, i, 1)]
        out_b, cast_out = _attn(z, vt, lam_p, diff_subln_g[l], lam_init, batch, seq, casts)
        final = l == depth - 1
        g2 = norm2_g[l].reshape(1, d)
        if l % 2 == 0:
            xf = _ffn(out_a, out_b, w_out[l].astype(BF16), xf, mod, g2, fg,
                      ffn_w_gate, ffn_w_up, ffn_w_down, i, seq, final)
            moe_gate_up = cast_out
        else:
            xf = _outproj(out_a, out_b, w_out[l].astype(BF16), xf, mod, seq)
            xf = _moe(xf, mod, g2, fg, w_router[i], moe_gate_up[0], moe_gate_up[1], cast_out[0][0],
                      seq, final)
    return xf.reshape(batch, seq, d)
```

```python
import functools
import math

import jax
import jax.numpy as jnp
import numpy as np
from jax import lax
from jax.experimental import pallas as pl
from jax.experimental.pallas import tpu as pltpu

F32 = jnp.float32
BF16 = jnp.bfloat16
EPS = 1e-6
N_MOD = 6
N_GMLP_HEADS = 8
GMLP_HEAD_DIM = 64
CHUNK = 128
N_DIFF_HEADS = 4
DIFF_HEAD_DIM = 64
DIFF_V_DIM = 128
N_EXPERTS = 8
LANES = 128
VMEM_LIMIT = 56 * 1024 * 1024

MOE_TILE = 1024
MOE_BLOCKS = (256, 352, 448)
MOE_LOOP_BLOCK = 160
MOE_FSPLIT = 2

ATTN_BLOCK = 1024
ATTN_ONES_ROWS = 16

SH1, SC1, G1, SH2, SC2, G2 = range(N_MOD)


def _params(*sem):
    return pltpu.CompilerParams(dimension_semantics=sem, vmem_limit_bytes=VMEM_LIMIT)


def _modnorm(x, g, sc, sh):
    ms = jnp.mean(x * x, axis=-1, keepdims=True)
    return (x * lax.rsqrt(ms + EPS)) * g * (1.0 + sc) + sh


def _ada_body(c_ref, w_ref, b_ref, o_ref):
    c = c_ref[...]
    sc = c * jax.nn.sigmoid(c)
    o_ref[0] = jnp.dot(sc, w_ref[0], preferred_element_type=F32,
                       precision=lax.Precision.HIGHEST) + b_ref[0]


def _ada(c, w_ada, b_ada):
    depth, d, m = w_ada.shape
    b = c.shape[0]
    rows = 8
    cp = jnp.zeros((rows, d), F32).at[:b].set(c)
    tn = 3072
    out = pl.pallas_call(
        _ada_body,
        grid=(depth, m // tn),
        in_specs=[pl.BlockSpec((rows, d), lambda l, j: (0, 0)),
                  pl.BlockSpec((1, d, tn), lambda l, j: (l, 0, j)),
                  pl.BlockSpec((1, 1, tn), lambda l, j: (l, 0, j))],
        out_specs=pl.BlockSpec((1, rows, tn), lambda l, j: (l, 0, j)),
        out_shape=jax.ShapeDtypeStruct((depth, rows, m), F32),
        compiler_params=_params("parallel", "parallel"),
        name="ada",
    )(cp, w_ada, b_ada.reshape(depth, 1, m))
    return out[:, :b].reshape(depth, b, N_MOD, d)


def _inproj_body(x_ref, mod_ref, g_ref, w_ref, wvt_ref, z_ref, vt_ref):
    h = _modnorm(x_ref[...], g_ref[...], mod_ref[0, SC1:SC1 + 1, :],
                 mod_ref[0, SH1:SH1 + 1, :]).astype(BF16)
    tm = x_ref.shape[0]
    zw = z_ref.shape[1]
    cw = 512
    for c0 in range(0, zw, cw):
        z_ref[:, c0:c0 + cw] = jnp.dot(
            h, w_ref[:, c0:c0 + cw], preferred_element_type=F32).astype(z_ref.dtype)
    dve = DIFF_V_DIM + ATTN_ONES_ROWS
    vt = lax.dot_general(wvt_ref[...], h, (((1,), (1,)), ((), ())),
                         preferred_element_type=F32).astype(vt_ref.dtype)
    for hd in range(N_DIFF_HEADS):
        vt_ref[hd * dve:hd * dve + DIFF_V_DIM, :] = vt[hd * DIFF_V_DIM:(hd + 1) * DIFF_V_DIM, :]
        vt_ref[hd * dve + DIFF_V_DIM:(hd + 1) * dve, :] = jnp.ones((ATTN_ONES_ROWS, tm), vt_ref.dtype)


def _inproj(x, mod, g, w, seq):
    n, d = x.shape
    vw = N_DIFF_HEADS * DIFF_V_DIM
    zw = w.shape[1] - vw
    dve = DIFF_V_DIM + ATTN_ONES_ROWS
    tm = 512
    return pl.pallas_call(
        _inproj_body,
        grid=(n // tm,),
        in_specs=[pl.BlockSpec((tm, d), lambda i: (i, 0)),
                  pl.BlockSpec((1, N_MOD, d), lambda i: (i * tm // seq, 0, 0)),
                  pl.BlockSpec((1, d), lambda i: (0, 0)),
                  pl.BlockSpec((d, zw), lambda i: (0, 0)),
                  pl.BlockSpec((vw, d), lambda i: (0, 0))],
        out_specs=[pl.BlockSpec((tm, zw), lambda i: (i, 0)),
                   pl.BlockSpec((N_DIFF_HEADS * dve, tm), lambda i: (0, i))],
        out_shape=[jax.ShapeDtypeStruct((n, zw), BF16),
                   jax.ShapeDtypeStruct((N_DIFF_HEADS * dve, n), BF16)],
        compiler_params=_params("parallel"),
        name="inproj",
    )(x, mod, g, w, w[:, zw:].T)


def _gmlp_body(u_ref, v_ref, vg_ref, w_ref, bias_ref, gm_ref, o_ref, *, nck):
    width = u_ref.shape[1]
    npair = width // LANES
    row = lax.broadcasted_iota(jnp.int32, (CHUNK, 2 * CHUNK), 0)
    col = lax.broadcasted_iota(jnp.int32, (CHUNK, 2 * CHUNK), 1)
    causal = jnp.where(col >= CHUNK, col - CHUNK, col) <= row
    lane = lax.broadcasted_iota(jnp.int32, (CHUNK, LANES), 1)
    left = lane < GMLP_HEAD_DIM
    wp = [jnp.where(causal, w_ref[p], jnp.zeros_like(w_ref[p])) for p in range(npair)]
    gm = gm_ref[...]
    for c in range(nck):
        rs = slice(c * CHUNK, (c + 1) * CHUNK)
        gu = jax.nn.gelu(u_ref[rs, :].astype(F32))
        gv = jax.nn.gelu(v_ref[rs, :].astype(F32))
        v2 = gv * gv
        hi = v2.astype(BF16)
        lo = (v2 - hi.astype(F32)).astype(BF16)
        ms = (jnp.dot(hi, gm, preferred_element_type=F32)
              + jnp.dot(lo, gm, preferred_element_type=F32))
        vn = gv * lax.rsqrt(ms + EPS) * vg_ref[...]
        outs = []
        for p in range(npair):
            vp = vn[:, p * LANES:(p + 1) * LANES]
            zero = jnp.zeros_like(vp)
            rhs = jnp.concatenate([jnp.where(left, vp, zero), jnp.where(left, zero, vp)],
                                  axis=0).astype(BF16)
            outs.append(jnp.dot(wp[p], rhs, preferred_element_type=F32))
        vm = jnp.concatenate(outs, axis=1) + bias_ref[...]
        o_ref[rs, :] = (gu * vm).astype(o_ref.dtype)


def _gmlp(z, vnorm_g, w_s, b_s):
    n = z.shape[0]
    width = N_GMLP_HEADS * GMLP_HEAD_DIM
    nck = 4
    rows = nck * CHUNK
    npair = N_GMLP_HEADS // 2
    w_pair = w_s.reshape(npair, 2, CHUNK, CHUNK).transpose(0, 2, 1, 3).reshape(
        npair, CHUNK, 2 * CHUNK).astype(BF16)
    bias = jnp.repeat(b_s.T, GMLP_HEAD_DIM, axis=1)
    grp = np.arange(width) // GMLP_HEAD_DIM
    group_mean = jnp.asarray((grp[:, None] == grp[None, :]) / GMLP_HEAD_DIM, dtype=BF16)
    return pl.pallas_call(
        functools.partial(_gmlp_body, nck=nck),
        grid=(n // rows,),
        in_specs=[pl.BlockSpec((rows, width), lambda i: (i, 0)),
                  pl.BlockSpec((rows, width), lambda i: (i, 1)),
                  pl.BlockSpec((1, width), lambda i: (0, 0)),
                  pl.BlockSpec((npair, CHUNK, 2 * CHUNK), lambda i: (0, 0, 0)),
                  pl.BlockSpec((CHUNK, width), lambda i: (0, 0)),
                  pl.BlockSpec((width, width), lambda i: (0, 0))],
        out_specs=pl.BlockSpec((rows, width), lambda i: (i, 0)),
        out_shape=jax.ShapeDtypeStruct((n, width), BF16),
        compiler_params=_params("parallel"),
        name="gmlp",
    )(z, z, vnorm_g.reshape(1, width), w_pair, bias, group_mean)


def _attn_body(*refs, tq, lam_init, ncast):
    qi_ref, ki_ref, q_ref, k_ref, vt_ref, lam_ref, g_ref = refs[:7]
    cast_src = refs[7:7 + ncast]
    o_ref = refs[7 + ncast]
    cast_dst = refs[8 + ncast:8 + 2 * ncast]
    qs1, qs2, m1, m2, acc1, acc2 = refs[8 + 2 * ncast:]
    for src, dst in zip(cast_src, cast_dst):
        pw = dst.shape[2]
        for j in range(dst.shape[0]):
            dst[j] = src[:, j * pw:(j + 1) * pw].astype(dst.dtype)
    qi = qi_ref[pl.program_id(2)]
    ki = ki_ref[pl.program_id(2)]
    maps = ((qs1, m1, acc1), (qs2, m2, acc2))
    dv = DIFF_V_DIM

    @pl.when(ki == 0)
    def _init():
        qsc = (q_ref[...].astype(F32) * (DIFF_HEAD_DIM ** -0.5 * math.log2(math.e))).astype(BF16)
        lane = lax.broadcasted_iota(jnp.int32, qsc.shape, 1)
        zero = jnp.zeros_like(qsc)
        qs1[...] = jnp.where(lane < DIFF_HEAD_DIM, qsc, zero)
        qs2[...] = jnp.where(lane < DIFF_HEAD_DIM, zero, qsc)
        for _, m_s, acc in maps:
            m_s[...] = jnp.full(m_s.shape, -jnp.inf, F32)
            acc[...] = jnp.zeros(acc.shape, F32)

    def step(masked):
        k = k_ref[...]
        vt = vt_ref[...]
        sts = [lax.dot_general(k, qs[...], (((1,), (1,)), ((), ())), preferred_element_type=F32)
               for qs, _, _ in maps]
        for st, (_, m_s, acc) in zip(sts, maps):
            if masked:
                key = lax.broadcasted_iota(jnp.int32, st.shape, 0)
                qry = lax.broadcasted_iota(jnp.int32, st.shape, 1)
                st = jnp.where(key <= qry, st, -jnp.inf)
            sb = st.astype(BF16)
            m_prev = m_s[...]
            m_new = jnp.maximum(m_prev, jnp.max(sb, axis=0, keepdims=True).astype(F32))
            alpha = jnp.exp2(m_prev - m_new)
            pt = jnp.exp2(sb - m_new.astype(BF16))
            acc[...] = alpha * acc[...] + jnp.dot(vt, pt, preferred_element_type=F32)
            m_s[...] = m_new

    @pl.when(ki < qi)
    def _full():
        step(False)

    @pl.when(ki == qi)
    def _diag():
        step(True)
        lp = lam_ref[...]
        s1 = jnp.sum(lp[0:1, :] * lp[1:2, :], axis=-1, keepdims=True)
        s2 = jnp.sum(lp[2:3, :] * lp[3:4, :], axis=-1, keepdims=True)
        lam = jnp.exp(s1) - jnp.exp(s2) + lam_init
        ot = (acc1[0:dv, :] / acc1[dv:dv + 1, :]
              - lam * (acc2[0:dv, :] / acc2[dv:dv + 1, :]))
        ot = ot * lax.rsqrt(jnp.mean(ot * ot, axis=0, keepdims=True) + EPS)
        o_ref[...] = (ot.T * g_ref[...] * (1.0 - lam_init)).astype(o_ref.dtype)


def _cast_rows(rows, nsteps):
    for rb in range(16, rows + 1, 16):
        if rows % rb == 0 and rows // rb <= nsteps:
            return rb
    raise ValueError(f"no row block for {rows} rows in {nsteps} steps")


def _attn(z, vt, lam_p, subln_g, lam_init, batch, seq, casts=()):
    n = z.shape[0]
    tq = ATTN_BLOCK
    dve = DIFF_V_DIM + ATTN_ONES_ROWS
    nq = seq // tq
    qcol = 2 * N_GMLP_HEADS * GMLP_HEAD_DIM // LANES
    kcol = qcol + N_DIFF_HEADS
    pairs = [(i, j) for i in range(nq) for j in range(i + 1)]
    npair = len(pairs)
    qi_tab = jnp.asarray([p[0] for p in pairs], jnp.int32)
    ki_tab = jnp.asarray([p[1] for p in pairs], jnp.int32)
    nsteps = batch * N_DIFF_HEADS * npair
    cast2d = [w.reshape(-1, w.shape[-1]) for w, _, _ in casts]
    cast_in, cast_out, cast_shape = [], [], []
    for w, i, pieces in casts:
        rows, cols = w.shape[1] * w.shape[2], w.shape[3]
        rb = _cast_rows(rows, nsteps)
        nb = rows // rb

        def blk(b, h, p, nb=nb):
            return jnp.minimum((b * N_DIFF_HEADS + h) * npair + p, nb - 1)
        cast_in.append(pl.BlockSpec(
            (rb, cols), lambda b, h, p, qt, kt, blk=blk, off=i * nb: (off + blk(b, h, p), 0)))
        cast_out.append(pl.BlockSpec(
            (pieces, rb, cols // pieces), lambda b, h, p, qt, kt, blk=blk: (0, blk(b, h, p), 0)))
        cast_shape.append((pieces, rows, cols // pieces))
    grid_spec = pltpu.PrefetchScalarGridSpec(
        num_scalar_prefetch=2,
        grid=(batch, N_DIFF_HEADS, npair),
        in_specs=[
            pl.BlockSpec((tq, LANES), lambda b, h, p, qt, kt: (b * nq + qt[p], qcol + h)),
            pl.BlockSpec((tq, LANES), lambda b, h, p, qt, kt: (b * nq + kt[p], kcol + h)),
            pl.BlockSpec((dve, tq), lambda b, h, p, qt, kt: (h, b * nq + kt[p])),
            pl.BlockSpec((4, DIFF_HEAD_DIM), lambda b, h, p, qt, kt: (0, 0)),
            pl.BlockSpec((1, DIFF_V_DIM), lambda b, h, p, qt, kt: (0, 0)),
        ] + cast_in,
        out_specs=[pl.BlockSpec((tq, LANES), lambda b, h, p, qt, kt: (b * nq + qt[p], h))]
        + cast_out,
        scratch_shapes=(2 * [pltpu.VMEM((tq, LANES), BF16)] + 2 * [pltpu.VMEM((1, tq), F32)]
                        + 2 * [pltpu.VMEM((dve, tq), F32)]),
    )
    outs = pl.pallas_call(
        functools.partial(_attn_body, tq=tq, lam_init=lam_init, ncast=len(casts)),
        grid_spec=grid_spec,
        out_shape=[jax.ShapeDtypeStruct((n, N_DIFF_HEADS * DIFF_V_DIM), BF16)]
        + [jax.ShapeDtypeStruct(s, BF16) for s in cast_shape],
        compiler_params=_params("arbitrary", "arbitrary", "arbitrary"),
        name="diff_attn",
    )(qi_tab, ki_tab, z, z, vt, lam_p, subln_g.reshape(1, DIFF_V_DIM), *cast2d)
    return outs[0], [o.reshape(pieces, w.shape[1], w.shape[2], w.shape[3] // pieces)
                     for o, (w, _, pieces) in zip(outs[1:], casts)]


def _outproj_body(a_ref, b_ref, w_ref, x_ref, mod_ref, o_ref):
    ka = a_ref.shape[1]
    y = jnp.dot(a_ref[...], w_ref[0:ka, :], preferred_element_type=F32)
    y = y + jnp.dot(b_ref[...], w_ref[ka:, :], preferred_element_type=F32)
    o_ref[...] = x_ref[...] + mod_ref[0, G1:G1 + 1, :] * y


def _outproj(out_a, out_b, w, x, mod, seq):
    n, d = x.shape
    tm = 512
    ka, kb = out_a.shape[1], out_b.shape[1]
    return pl.pallas_call(
        _outproj_body,
        grid=(n // tm,),
        in_specs=[pl.BlockSpec((tm, ka), lambda i: (i, 0)),
                  pl.BlockSpec((tm, kb), lambda i: (i, 0)),
                  pl.BlockSpec((ka + kb, d), lambda i: (0, 0)),
                  pl.BlockSpec((tm, d), lambda i: (i, 0)),
                  pl.BlockSpec((1, N_MOD, d), lambda i: (i * tm // seq, 0, 0))],
        out_specs=pl.BlockSpec((tm, d), lambda i: (i, 0)),
        out_shape=jax.ShapeDtypeStruct((n, d), F32),
        compiler_params=_params("parallel"),
        name="outproj",
    )(out_a, out_b, w, x, mod)


def _ffn_body(a_ref, b_ref, wo_ref, x_ref, mod_ref, g_ref, fg_ref, wg_ref, wu_ref, wd_ref, o_ref,
              h_s, acc, *, final):
    j = pl.program_id(1)

    @pl.when(j == 0)
    def _prologue():
        ka = a_ref.shape[1]
        y = jnp.dot(a_ref[...], wo_ref[0:ka, :], preferred_element_type=F32)
        y = y + jnp.dot(b_ref[...], wo_ref[ka:, :], preferred_element_type=F32)
        x = x_ref[...] + mod_ref[0, G1:G1 + 1, :] * y
        o_ref[...] = x
        h = _modnorm(x, g_ref[...], mod_ref[0, SC2:SC2 + 1, :], mod_ref[0, SH2:SH2 + 1, :])
        h_s[...] = h.astype(BF16)
        acc[...] = jnp.zeros(acc.shape, F32)

    h = h_s[...]
    g = jnp.dot(h, wg_ref[0].astype(BF16), preferred_element_type=F32)
    u = jnp.dot(h, wu_ref[0].astype(BF16), preferred_element_type=F32)
    a = (g * jax.nn.sigmoid(g) * u).astype(BF16)
    acc[...] += jnp.dot(a, wd_ref[0].astype(BF16), preferred_element_type=F32)

    @pl.when(j == pl.num_programs(1) - 1)
    def _epilogue():
        y = o_ref[...] + mod_ref[0, G2:G2 + 1, :] * acc[...]
        if final:
            y = y * lax.rsqrt(jnp.mean(y * y, axis=-1, keepdims=True) + EPS) * fg_ref[...]
        o_ref[...] = y


def _ffn(out_a, out_b, w_out, x, mod, g, final_g, w_gate, w_up, w_down, li, seq, final):
    n, d = x.shape
    f = w_gate.shape[2]
    ka, kb = out_a.shape[1], out_b.shape[1]
    tm = 1024
    tf = 256
    return pl.pallas_call(
        functools.partial(_ffn_body, final=final),
        grid=(n // tm, f // tf),
        in_specs=[pl.BlockSpec((tm, ka), lambda i, j: (i, 0)),
                  pl.BlockSpec((tm, kb), lambda i, j: (i, 0)),
                  pl.BlockSpec((ka + kb, d), lambda i, j: (0, 0)),
                  pl.BlockSpec((tm, d), lambda i, j: (i, 0)),
                  pl.BlockSpec((1, N_MOD, d), lambda i, j: (i * tm // seq, 0, 0)),
                  pl.BlockSpec((1, d), lambda i, j: (0, 0)),
                  pl.BlockSpec((1, d), lambda i, j: (0, 0)),
                  pl.BlockSpec((1, d, tf), lambda i, j: (li, 0, j)),
                  pl.BlockSpec((1, d, tf), lambda i, j: (li, 0, j)),
                  pl.BlockSpec((1, tf, d), lambda i, j: (li, j, 0))],
        out_specs=pl.BlockSpec((tm, d), lambda i, j: (i, 0)),
        out_shape=jax.ShapeDtypeStruct((n, d), F32),
        scratch_shapes=[pltpu.VMEM((tm, d), BF16), pltpu.VMEM((tm, d), F32)],
        compiler_params=_params("parallel", "arbitrary"),
        name="ffn_mixer",
    )(out_a, out_b, w_out, x, mod, g, final_g, w_gate, w_up, w_down)


def _top2(logits):
    lane = lax.broadcasted_iota(jnp.int32, logits.shape, 1)
    lg = jnp.where(lane < N_EXPERTS, logits, -jnp.inf)
    m1 = jnp.max(lg, axis=-1, keepdims=True)
    i1 = jnp.min(jnp.where(lg == m1, lane, LANES), axis=-1, keepdims=True)
    lg2 = jnp.where(lane == i1, -jnp.inf, lg)
    m2 = jnp.max(lg2, axis=-1, keepdims=True)
    i2 = jnp.min(jnp.where(lg2 == m2, lane, LANES), axis=-1, keepdims=True)
    e2 = jnp.exp(m2 - m1)
    den = 1.0 + e2
    return i1, i2, 1.0 / den, e2 / den


def _router_body(x_ref, mod_ref, g_ref, wr_ref, h_ref, post_ref, gatet_ref, cnt_ref):
    t = x_ref.shape[0]
    h = _modnorm(x_ref[...], g_ref[...], mod_ref[0, SC2:SC2 + 1, :], mod_ref[0, SH2:SH2 + 1, :])
    h_hi = h.astype(BF16)
    h_ref[...] = h_hi
    h_lo = (h - h_hi.astype(F32)).astype(BF16)
    wr = wr_ref[...]
    w_hi = wr.astype(BF16)
    w_lo = (wr - w_hi.astype(F32)).astype(BF16)
    logits = (jnp.dot(h_hi, w_hi, preferred_element_type=F32)
              + jnp.dot(h_lo, w_hi, preferred_element_type=F32)
              + jnp.dot(h_hi, w_lo, preferred_element_type=F32))
    i1, i2, w1, w2 = _top2(logits)
    lane = lax.broadcasted_iota(jnp.int32, logits.shape, 1)
    sel1 = lane == i1
    sel2 = lane == i2
    gate = jnp.where(sel1, w1, 0.0) + jnp.where(sel2, w2, 0.0)
    gatet_ref[...] = gate.T[0:N_EXPERTS, :]
    picked = jnp.logical_or(sel1, sel2)
    onehot = jnp.where(picked, 1.0, 0.0)
    row = lax.broadcasted_iota(jnp.int32, (t, t), 0)
    col = lax.broadcasted_iota(jnp.int32, (t, t), 1)
    before = jnp.where(col < row, 1.0, 0.0).astype(BF16)
    rank = jnp.dot(before, onehot.astype(BF16), preferred_element_type=F32)
    pos = jnp.where(picked, rank, -1.0)
    post_ref[...] = pos.T[0:N_EXPERTS, :]
    cnt_ref[0] = jnp.broadcast_to(jnp.sum(onehot, axis=0, keepdims=True), cnt_ref.shape[1:])


def _router(x, mod, g, w_router, seq):
    n, d = x.shape
    t = MOE_TILE
    nt = n // t
    wr = jnp.zeros((d, LANES), F32).at[:, :N_EXPERTS].set(w_router)
    return pl.pallas_call(
        _router_body,
        grid=(nt,),
        in_specs=[pl.BlockSpec((t, d), lambda i: (i, 0)),
                  pl.BlockSpec((1, N_MOD, d), lambda i: (i * t // seq, 0, 0)),
                  pl.BlockSpec((1, d), lambda i: (0, 0)),
                  pl.BlockSpec((d, LANES), lambda i: (0, 0))],
        out_specs=[pl.BlockSpec((t, d), lambda i: (i, 0)),
                   pl.BlockSpec((N_EXPERTS, t), lambda i: (0, i)),
                   pl.BlockSpec((N_EXPERTS, t), lambda i: (0, i)),
                   pl.BlockSpec((1, 8, LANES), lambda i: (i, 0, 0))],
        out_shape=[jax.ShapeDtypeStruct((n, d), BF16),
                   jax.ShapeDtypeStruct((N_EXPERTS, n), F32),
                   jax.ShapeDtypeStruct((N_EXPERTS, n), F32),
                   jax.ShapeDtypeStruct((nt, 8, LANES), F32)],
        compiler_params=_params("parallel"),
        name="router",
    )(x, mod, g, wr)


def _moe_body(cnt_ref, h_ref, post_ref, gatet_ref, x_ref, mod_ref, fg_ref,
              wg_ref, wu_ref, wd_ref, o_ref, xg, acc, *, final):
    ti = pl.program_id(0)
    e = pl.program_id(1)
    s = pl.program_id(2)
    ns = pl.num_programs(2)
    t = h_ref.shape[0]
    cnt = cnt_ref[ti, e]

    @pl.when(jnp.logical_and(e == 0, s == 0))
    def _zero():
        o_ref[...] = jnp.zeros(o_ref.shape, F32)

    def block(b, bm):
        if isinstance(b, int):
            rows = slice(b * bm, (b + 1) * bm)
        else:
            rows = pl.ds(pl.multiple_of(b * bm, 16), bm)

        @pl.when(s == 0)
        def _gather():
            tgt = (lax.broadcasted_iota(jnp.int32, (bm, t), 0) + b * bm).astype(F32)
            onehot = jnp.where(post_ref[pl.ds(e, 1), :] == tgt, 1.0, 0.0).astype(BF16)
            xg[rows, :] = jnp.dot(onehot, h_ref[...], preferred_element_type=F32).astype(BF16)

        xb = xg[rows, :]
        g = jnp.dot(xb, wg_ref[0, 0], preferred_element_type=F32)
        u = jnp.dot(xb, wu_ref[0, 0], preferred_element_type=F32)
        a = (g * jax.nn.sigmoid(g) * u).astype(BF16)
        y = jnp.dot(a, wd_ref[0], preferred_element_type=F32)

        @pl.when(s == 0)
        def _first():
            acc[rows, :] = y

        @pl.when(s != 0)
        def _rest():
            acc[rows, :] += y

        @pl.when(s == ns - 1)
        def _scatter():
            tgt = (lax.broadcasted_iota(jnp.int32, (bm, t), 0) + b * bm).astype(F32)
            hit = post_ref[pl.ds(e, 1), :] == tgt
            gate_rows = jnp.sum(jnp.where(hit, gatet_ref[pl.ds(e, 1), :], 0.0), axis=-1,
                                keepdims=True)
            yb = (acc[rows, :] * gate_rows).astype(BF16)
            o_ref[...] += lax.dot_general(jnp.where(hit, 1.0, 0.0).astype(BF16), yb,
                                          (((0,), (0,)), ((), ())), preferred_element_type=F32)

    lo = MOE_LOOP_BLOCK
    for bm in MOE_BLOCKS:
        @pl.when(jnp.logical_and(cnt > lo, cnt <= bm))
        def _single(bm=bm):
            block(0, bm)
        lo = bm

    @pl.when(jnp.logical_or(cnt <= MOE_LOOP_BLOCK, cnt > lo))
    def _multi():
        bm = MOE_LOOP_BLOCK

        def body(b, carry):
            block(b, bm)
            return carry
        lax.fori_loop(0, (cnt + (bm - 1)) // bm, body, 0)

    @pl.when(jnp.logical_and(e == pl.num_programs(1) - 1, s == ns - 1))
    def _epilogue():
        y = x_ref[...] + mod_ref[0, G2:G2 + 1, :] * o_ref[...]
        if final:
            y = y * lax.rsqrt(jnp.mean(y * y, axis=-1, keepdims=True) + EPS) * fg_ref[...]
        o_ref[...] = y


def _moe(x, mod, g, final_g, w_router, w_gate, w_up, w_down, seq, final):
    n, d = x.shape
    n_e, f, _ = w_down.shape
    t = MOE_TILE
    nt = n // t
    fs = f // MOE_FSPLIT
    h, post, gatet, cnt = _router(x, mod, g, w_router, seq)
    counts = cnt[:, 0, :N_EXPERTS].astype(jnp.int32)
    cap = max(pl.cdiv(t, MOE_LOOP_BLOCK) * MOE_LOOP_BLOCK, MOE_BLOCKS[-1])
    grid_spec = pltpu.PrefetchScalarGridSpec(
        num_scalar_prefetch=1,
        grid=(nt, n_e, MOE_FSPLIT),
        in_specs=[pl.BlockSpec((t, d), lambda i, e, s, c: (i, 0)),
                  pl.BlockSpec((N_EXPERTS, t), lambda i, e, s, c: (0, i)),
                  pl.BlockSpec((N_EXPERTS, t), lambda i, e, s, c: (0, i)),
                  pl.BlockSpec((t, d), lambda i, e, s, c: (i, 0)),
                  pl.BlockSpec((1, N_MOD, d), lambda i, e, s, c: (i * t // seq, 0, 0)),
                  pl.BlockSpec((1, d), lambda i, e, s, c: (0, 0)),
                  pl.BlockSpec((1, 1, d, fs), lambda i, e, s, c: (s, e, 0, 0)),
                  pl.BlockSpec((1, 1, d, fs), lambda i, e, s, c: (s, e, 0, 0)),
                  pl.BlockSpec((1, fs, d), lambda i, e, s, c: (e, s, 0))],
        out_specs=pl.BlockSpec((t, d), lambda i, e, s, c: (i, 0)),
        scratch_shapes=[pltpu.VMEM((cap, d), BF16), pltpu.VMEM((cap, d), F32)],
    )
    return pl.pallas_call(
        functools.partial(_moe_body, final=final),
        grid_spec=grid_spec,
        out_shape=jax.ShapeDtypeStruct((n, d), F32),
        compiler_params=_params("parallel", "arbitrary", "arbitrary"),
        name="moe_mixer",
    )(counts, h, post, gatet, x, mod, final_g, w_gate, w_up, w_down)


def kernel(x, c, w_ada, b_ada, norm1_g, norm2_g, w_in, w_out, gmlp_vnorm_g, gmlp_ws, gmlp_bs,
           lam_q1, lam_k1, lam_q2, lam_k2, diff_subln_g, ffn_w_gate, ffn_w_up, ffn_w_down,
           w_router, moe_w_gate, moe_w_up, moe_w_down, final_g):
    batch, seq, d = x.shape
    depth = w_ada.shape[0]
    n = batch * seq
    mods = _ada(c, w_ada, b_ada)
    xf = x.reshape(n, d)
    fg = final_g.reshape(1, d)
    for l in range(depth):
        mod = mods[l]
        z, vt = _inproj(xf, mod, norm1_g[l].reshape(1, d), w_in[l].astype(BF16), seq)
        out_a = _gmlp(z, gmlp_vnorm_g[l], gmlp_ws[l], gmlp_bs[l])
        lam_init = 0.8 - 0.6 * math.exp(-0.3 * l)
        lam_p = jnp.stack([lam_q1[l], lam_k1[l], lam_q2[l], lam_k2[l]])
        i = l // 2
        if l % 2 == 0:
            casts = ([(moe_w_gate, i, MOE_FSPLIT), (moe_w_up, i, MOE_FSPLIT)]
                     if l + 1 < depth else [])
        else:
            casts = [(moe_w_down, i, 1)]
        out_b, cast_out = _attn(z, vt, lam_p, diff_subln_g[l], lam_init, batch, seq, casts)
        final = l == depth - 1
        g2 = norm2_g[l].reshape(1, d)
        if l % 2 == 0:
            xf = _ffn(out_a, out_b, w_out[l].astype(BF16), xf, mod, g2, fg,
                      ffn_w_gate, ffn_w_up, ffn_w_down, i, seq, final)
            moe_gate_up = cast_out
        else:
            xf = _outproj(out_a, out_b, w_out[l].astype(BF16), xf, mod, seq)
            xf = _moe(xf, mod, g2, fg, w_router[i], moe_gate_up[0], moe_gate_up[1], cast_out[0][0],
                      seq, final)
    return xf.reshape(batch, seq, d)
```

```python
import functools
import math

import jax
import jax.numpy as jnp
import numpy as np
from jax import lax
from jax.experimental import pallas as pl
from jax.experimental.pallas import tpu as pltpu

F32 = jnp.float32
BF16 = jnp.bfloat16
EPS = 1e-6
N_MOD = 6
N_GMLP_HEADS = 8
GMLP_HEAD_DIM = 64
CHUNK = 128
N_DIFF_HEADS = 4
DIFF_HEAD_DIM = 64
DIFF_V_DIM = 128
N_EXPERTS = 8
LANES = 128
VMEM_LIMIT = 56 * 1024 * 1024

MOE_TILE = 1024
MOE_BLOCKS = (256, 352, 448)
MOE_LOOP_BLOCK = 160
MOE_FSPLIT = 2

FFN_FSPLIT = 2
FFN_CHUNK = 256

ATTN_BLOCK = 1024
ATTN_ONES_ROWS = 16

SH1, SC1, G1, SH2, SC2, G2 = range(N_MOD)


def _params(*sem):
    return pltpu.CompilerParams(dimension_semantics=sem, vmem_limit_bytes=VMEM_LIMIT)


def _modnorm(x, g, sc, sh):
    ms = jnp.mean(x * x, axis=-1, keepdims=True)
    return (x * lax.rsqrt(ms + EPS)) * g * (1.0 + sc) + sh


def _ada_body(c_ref, w_ref, b_ref, o_ref):
    c = c_ref[...]
    sc = c * jax.nn.sigmoid(c)
    o_ref[0] = jnp.dot(sc, w_ref[0], preferred_element_type=F32,
                       precision=lax.Precision.HIGHEST) + b_ref[0]


def _ada(c, w_ada, b_ada):
    depth, d, m = w_ada.shape
    b = c.shape[0]
    rows = 8
    cp = jnp.zeros((rows, d), F32).at[:b].set(c)
    tn = 3072
    out = pl.pallas_call(
        _ada_body,
        grid=(depth, m // tn),
        in_specs=[pl.BlockSpec((rows, d), lambda l, j: (0, 0)),
                  pl.BlockSpec((1, d, tn), lambda l, j: (l, 0, j)),
                  pl.BlockSpec((1, 1, tn), lambda l, j: (l, 0, j))],
        out_specs=pl.BlockSpec((1, rows, tn), lambda l, j: (l, 0, j)),
        out_shape=jax.ShapeDtypeStruct((depth, rows, m), F32),
        compiler_params=_params("parallel", "parallel"),
        name="ada",
    )(cp, w_ada, b_ada.reshape(depth, 1, m))
    return out[:, :b].reshape(depth, b, N_MOD, d)


def _inproj_body(x_ref, mod_ref, g_ref, w_ref, wvt_ref, z_ref, vt_ref):
    h = _modnorm(x_ref[...], g_ref[...], mod_ref[0, SC1:SC1 + 1, :],
                 mod_ref[0, SH1:SH1 + 1, :]).astype(BF16)
    tm = x_ref.shape[0]
    zw = z_ref.shape[1]
    cw = 512
    for c0 in range(0, zw, cw):
        z_ref[:, c0:c0 + cw] = jnp.dot(
            h, w_ref[:, c0:c0 + cw], preferred_element_type=F32).astype(z_ref.dtype)
    dve = DIFF_V_DIM + ATTN_ONES_ROWS
    vt = lax.dot_general(wvt_ref[...], h, (((1,), (1,)), ((), ())),
                         preferred_element_type=F32).astype(vt_ref.dtype)
    for hd in range(N_DIFF_HEADS):
        vt_ref[hd * dve:hd * dve + DIFF_V_DIM, :] = vt[hd * DIFF_V_DIM:(hd + 1) * DIFF_V_DIM, :]
        vt_ref[hd * dve + DIFF_V_DIM:(hd + 1) * dve, :] = jnp.ones((ATTN_ONES_ROWS, tm), vt_ref.dtype)


def _inproj(x, mod, g, w, seq):
    n, d = x.shape
    vw = N_DIFF_HEADS * DIFF_V_DIM
    zw = w.shape[1] - vw
    dve = DIFF_V_DIM + ATTN_ONES_ROWS
    tm = 512
    return pl.pallas_call(
        _inproj_body,
        grid=(n // tm,),
        in_specs=[pl.BlockSpec((tm, d), lambda i: (i, 0)),
                  pl.BlockSpec((1, N_MOD, d), lambda i: (i * tm // seq, 0, 0)),
                  pl.BlockSpec((1, d), lambda i: (0, 0)),
                  pl.BlockSpec((d, zw), lambda i: (0, 0)),
                  pl.BlockSpec((vw, d), lambda i: (0, 0))],
        out_specs=[pl.BlockSpec((tm, zw), lambda i: (i, 0)),
                   pl.BlockSpec((N_DIFF_HEADS * dve, tm), lambda i: (0, i))],
        out_shape=[jax.ShapeDtypeStruct((n, zw), BF16),
                   jax.ShapeDtypeStruct((N_DIFF_HEADS * dve, n), BF16)],
        compiler_params=_params("parallel"),
        name="inproj",
    )(x, mod, g, w, w[:, zw:].T)


def _gmlp_body(u_ref, v_ref, vg_ref, w_ref, bias_ref, gm_ref, o_ref, *, nck):
    width = u_ref.shape[1]
    npair = width // LANES
    row = lax.broadcasted_iota(jnp.int32, (CHUNK, 2 * CHUNK), 0)
    col = lax.broadcasted_iota(jnp.int32, (CHUNK, 2 * CHUNK), 1)
    causal = jnp.where(col >= CHUNK, col - CHUNK, col) <= row
    lane = lax.broadcasted_iota(jnp.int32, (CHUNK, LANES), 1)
    left = lane < GMLP_HEAD_DIM
    wp = [jnp.where(causal, w_ref[p], jnp.zeros_like(w_ref[p])) for p in range(npair)]
    gm = gm_ref[...]
    for c in range(nck):
        rs = slice(c * CHUNK, (c + 1) * CHUNK)
        gu = jax.nn.gelu(u_ref[rs, :].astype(F32))
        gv = jax.nn.gelu(v_ref[rs, :].astype(F32))
        v2 = gv * gv
        hi = v2.astype(BF16)
        lo = (v2 - hi.astype(F32)).astype(BF16)
        ms = (jnp.dot(hi, gm, preferred_element_type=F32)
              + jnp.dot(lo, gm, preferred_element_type=F32))
        vn = gv * lax.rsqrt(ms + EPS) * vg_ref[...]
        outs = []
        for p in range(npair):
            vp = vn[:, p * LANES:(p + 1) * LANES]
            zero = jnp.zeros_like(vp)
            rhs = jnp.concatenate([jnp.where(left, vp, zero), jnp.where(left, zero, vp)],
                                  axis=0).astype(BF16)
            outs.append(jnp.dot(wp[p], rhs, preferred_element_type=F32))
        vm = jnp.concatenate(outs, axis=1) + bias_ref[...]
        o_ref[rs, :] = (gu * vm).astype(o_ref.dtype)


def _gmlp(z, vnorm_g, w_s, b_s):
    n = z.shape[0]
    width = N_GMLP_HEADS * GMLP_HEAD_DIM
    nck = 4
    rows = nck * CHUNK
    npair = N_GMLP_HEADS // 2
    w_pair = w_s.reshape(npair, 2, CHUNK, CHUNK).transpose(0, 2, 1, 3).reshape(
        npair, CHUNK, 2 * CHUNK).astype(BF16)
    bias = jnp.repeat(b_s.T, GMLP_HEAD_DIM, axis=1)
    grp = np.arange(width) // GMLP_HEAD_DIM
    group_mean = jnp.asarray((grp[:, None] == grp[None, :]) / GMLP_HEAD_DIM, dtype=BF16)
    return pl.pallas_call(
        functools.partial(_gmlp_body, nck=nck),
        grid=(n // rows,),
        in_specs=[pl.BlockSpec((rows, width), lambda i: (i, 0)),
                  pl.BlockSpec((rows, width), lambda i: (i, 1)),
                  pl.BlockSpec((1, width), lambda i: (0, 0)),
                  pl.BlockSpec((npair, CHUNK, 2 * CHUNK), lambda i: (0, 0, 0)),
                  pl.BlockSpec((CHUNK, width), lambda i: (0, 0)),
                  pl.BlockSpec((width, width), lambda i: (0, 0))],
        out_specs=pl.BlockSpec((rows, width), lambda i: (i, 0)),
        out_shape=jax.ShapeDtypeStruct((n, width), BF16),
        compiler_params=_params("parallel"),
        name="gmlp",
    )(z, z, vnorm_g.reshape(1, width), w_pair, bias, group_mean)


def _attn_body(*refs, tq, lam_init, ncast):
    qi_ref, ki_ref, q_ref, k_ref, vt_ref, lam_ref, g_ref = refs[:7]
    cast_src = refs[7:7 + ncast]
    o_ref = refs[7 + ncast]
    cast_dst = refs[8 + ncast:8 + 2 * ncast]
    qs1, qs2, m1, m2, acc1, acc2 = refs[8 + 2 * ncast:]
    for src, dst in zip(cast_src, cast_dst):
        pw = dst.shape[2]
        for j in range(dst.shape[0]):
            dst[j] = src[:, j * pw:(j + 1) * pw].astype(dst.dtype)
    qi = qi_ref[pl.program_id(2)]
    ki = ki_ref[pl.program_id(2)]
    maps = ((qs1, m1, acc1), (qs2, m2, acc2))
    dv = DIFF_V_DIM

    @pl.when(ki == 0)
    def _init():
        qsc = (q_ref[...].astype(F32) * (DIFF_HEAD_DIM ** -0.5 * math.log2(math.e))).astype(BF16)
        lane = lax.broadcasted_iota(jnp.int32, qsc.shape, 1)
        zero = jnp.zeros_like(qsc)
        qs1[...] = jnp.where(lane < DIFF_HEAD_DIM, qsc, zero)
        qs2[...] = jnp.where(lane < DIFF_HEAD_DIM, zero, qsc)
        for _, m_s, acc in maps:
            m_s[...] = jnp.full(m_s.shape, -jnp.inf, F32)
            acc[...] = jnp.zeros(acc.shape, F32)

    def step(masked):
        k = k_ref[...]
        vt = vt_ref[...]
        sts = [lax.dot_general(k, qs[...], (((1,), (1,)), ((), ())), preferred_element_type=F32)
               for qs, _, _ in maps]
        for st, (_, m_s, acc) in zip(sts, maps):
            if masked:
                key = lax.broadcasted_iota(jnp.int32, st.shape, 0)
                qry = lax.broadcasted_iota(jnp.int32, st.shape, 1)
                st = jnp.where(key <= qry, st, -jnp.inf)
            sb = st.astype(BF16)
            m_prev = m_s[...]
            m_new = jnp.maximum(m_prev, jnp.max(sb, axis=0, keepdims=True).astype(F32))
            alpha = jnp.exp2(m_prev - m_new)
            pt = jnp.exp2(sb - m_new.astype(BF16))
            acc[...] = alpha * acc[...] + jnp.dot(vt, pt, preferred_element_type=F32)
            m_s[...] = m_new

    @pl.when(ki < qi)
    def _full():
        step(False)

    @pl.when(ki == qi)
    def _diag():
        step(True)
        lp = lam_ref[...]
        s1 = jnp.sum(lp[0:1, :] * lp[1:2, :], axis=-1, keepdims=True)
        s2 = jnp.sum(lp[2:3, :] * lp[3:4, :], axis=-1, keepdims=True)
        lam = jnp.exp(s1) - jnp.exp(s2) + lam_init
        ot = (acc1[0:dv, :] / acc1[dv:dv + 1, :]
              - lam * (acc2[0:dv, :] / acc2[dv:dv + 1, :]))
        ot = ot * lax.rsqrt(jnp.mean(ot * ot, axis=0, keepdims=True) + EPS)
        o_ref[...] = (ot.T * g_ref[...] * (1.0 - lam_init)).astype(o_ref.dtype)


def _cast_rows(rows, nsteps):
    for rb in range(16, rows + 1, 16):
        if rows % rb == 0 and rows // rb <= nsteps:
            return rb
    raise ValueError(f"no row block for {rows} rows in {nsteps} steps")


def _attn(z, vt, lam_p, subln_g, lam_init, batch, seq, casts=()):
    n = z.shape[0]
    tq = ATTN_BLOCK
    dve = DIFF_V_DIM + ATTN_ONES_ROWS
    nq = seq // tq
    qcol = 2 * N_GMLP_HEADS * GMLP_HEAD_DIM // LANES
    kcol = qcol + N_DIFF_HEADS
    pairs = [(i, j) for i in range(nq) for j in range(i + 1)]
    npair = len(pairs)
    qi_tab = jnp.asarray([p[0] for p in pairs], jnp.int32)
    ki_tab = jnp.asarray([p[1] for p in pairs], jnp.int32)
    nsteps = batch * N_DIFF_HEADS * npair
    cast2d = [w.reshape(-1, w.shape[-1]) for w, _, _ in casts]
    cast_in, cast_out, cast_shape = [], [], []
    for w, i, pieces in casts:
        rows, cols = w.shape[1] * w.shape[2], w.shape[3]
        rb = _cast_rows(rows, nsteps)
        nb = rows // rb

        def blk(b, h, p, nb=nb):
            return jnp.minimum((b * N_DIFF_HEADS + h) * npair + p, nb - 1)
        cast_in.append(pl.BlockSpec(
            (rb, cols), lambda b, h, p, qt, kt, blk=blk, off=i * nb: (off + blk(b, h, p), 0)))
        cast_out.append(pl.BlockSpec(
            (pieces, rb, cols // pieces), lambda b, h, p, qt, kt, blk=blk: (0, blk(b, h, p), 0)))
        cast_shape.append((pieces, rows, cols // pieces))
    grid_spec = pltpu.PrefetchScalarGridSpec(
        num_scalar_prefetch=2,
        grid=(batch, N_DIFF_HEADS, npair),
        in_specs=[
            pl.BlockSpec((tq, LANES), lambda b, h, p, qt, kt: (b * nq + qt[p], qcol + h)),
            pl.BlockSpec((tq, LANES), lambda b, h, p, qt, kt: (b * nq + kt[p], kcol + h)),
            pl.BlockSpec((dve, tq), lambda b, h, p, qt, kt: (h, b * nq + kt[p])),
            pl.BlockSpec((4, DIFF_HEAD_DIM), lambda b, h, p, qt, kt: (0, 0)),
            pl.BlockSpec((1, DIFF_V_DIM), lambda b, h, p, qt, kt: (0, 0)),
        ] + cast_in,
        out_specs=[pl.BlockSpec((tq, LANES), lambda b, h, p, qt, kt: (b * nq + qt[p], h))]
        + cast_out,
        scratch_shapes=(2 * [pltpu.VMEM((tq, LANES), BF16)] + 2 * [pltpu.VMEM((1, tq), F32)]
                        + 2 * [pltpu.VMEM((dve, tq), F32)]),
    )
    outs = pl.pallas_call(
        functools.partial(_attn_body, tq=tq, lam_init=lam_init, ncast=len(casts)),
        grid_spec=grid_spec,
        out_shape=[jax.ShapeDtypeStruct((n, N_DIFF_HEADS * DIFF_V_DIM), BF16)]
        + [jax.ShapeDtypeStruct(s, BF16) for s in cast_shape],
        compiler_params=_params("arbitrary", "arbitrary", "arbitrary"),
        name="diff_attn",
    )(qi_tab, ki_tab, z, z, vt, lam_p, subln_g.reshape(1, DIFF_V_DIM), *cast2d)
    return outs[0], [o.reshape(pieces, w.shape[1], w.shape[2], w.shape[3] // pieces)
                     for o, (w, _, pieces) in zip(outs[1:], casts)]


def _outproj_body(a_ref, b_ref, w_ref, x_ref, mod_ref, o_ref):
    ka = a_ref.shape[1]
    y = jnp.dot(a_ref[...], w_ref[0:ka, :], preferred_element_type=F32)
    y = y + jnp.dot(b_ref[...], w_ref[ka:, :], preferred_element_type=F32)
    o_ref[...] = x_ref[...] + mod_ref[0, G1:G1 + 1, :] * y


def _outproj(out_a, out_b, w, x, mod, seq):
    n, d = x.shape
    tm = 512
    ka, kb = out_a.shape[1], out_b.shape[1]
    return pl.pallas_call(
        _outproj_body,
        grid=(n // tm,),
        in_specs=[pl.BlockSpec((tm, ka), lambda i: (i, 0)),
                  pl.BlockSpec((tm, kb), lambda i: (i, 0)),
                  pl.BlockSpec((ka + kb, d), lambda i: (0, 0)),
                  pl.BlockSpec((tm, d), lambda i: (i, 0)),
                  pl.BlockSpec((1, N_MOD, d), lambda i: (i * tm // seq, 0, 0))],
        out_specs=pl.BlockSpec((tm, d), lambda i: (i, 0)),
        out_shape=jax.ShapeDtypeStruct((n, d), F32),
        compiler_params=_params("parallel"),
        name="outproj",
    )(out_a, out_b, w, x, mod)


def _ffn_body(a_ref, b_ref, wo_ref, x_ref, mod_ref, g_ref, fg_ref, wg_ref, wu_ref, wd_ref, o_ref,
              h_s, acc, *, final):
    j = pl.program_id(1)

    @pl.when(j == 0)
    def _prologue():
        ka = a_ref.shape[1]
        y = jnp.dot(a_ref[...], wo_ref[0:ka, :], preferred_element_type=F32)
        y = y + jnp.dot(b_ref[...], wo_ref[ka:, :], preferred_element_type=F32)
        x = x_ref[...] + mod_ref[0, G1:G1 + 1, :] * y
        o_ref[...] = x
        h = _modnorm(x, g_ref[...], mod_ref[0, SC2:SC2 + 1, :], mod_ref[0, SH2:SH2 + 1, :])
        h_s[...] = h.astype(BF16)
        acc[...] = jnp.zeros(acc.shape, F32)

    h = h_s[...]
    fs = wd_ref.shape[0]
    for c0 in range(0, fs, FFN_CHUNK):
        cs = slice(c0, min(c0 + FFN_CHUNK, fs))
        g = jnp.dot(h, wg_ref[0, :, cs], preferred_element_type=F32)
        u = jnp.dot(h, wu_ref[0, :, cs], preferred_element_type=F32)
        a = (g * jax.nn.sigmoid(g) * u).astype(BF16)
        acc[...] += jnp.dot(a, wd_ref[cs, :], preferred_element_type=F32)

    @pl.when(j == pl.num_programs(1) - 1)
    def _epilogue():
        y = o_ref[...] + mod_ref[0, G2:G2 + 1, :] * acc[...]
        if final:
            y = y * lax.rsqrt(jnp.mean(y * y, axis=-1, keepdims=True) + EPS) * fg_ref[...]
        o_ref[...] = y


def _ffn(out_a, out_b, w_out, x, mod, g, final_g, w_gate, w_up, w_down, seq, final):
    n, d = x.shape
    nsplit, _, fs = w_gate.shape
    ka, kb = out_a.shape[1], out_b.shape[1]
    tm = 1024
    return pl.pallas_call(
        functools.partial(_ffn_body, final=final),
        grid=(n // tm, nsplit),
        in_specs=[pl.BlockSpec((tm, ka), lambda i, j: (i, 0)),
                  pl.BlockSpec((tm, kb), lambda i, j: (i, 0)),
                  pl.BlockSpec((ka + kb, d), lambda i, j: (0, 0)),
                  pl.BlockSpec((tm, d), lambda i, j: (i, 0)),
                  pl.BlockSpec((1, N_MOD, d), lambda i, j: (i * tm // seq, 0, 0)),
                  pl.BlockSpec((1, d), lambda i, j: (0, 0)),
                  pl.BlockSpec((1, d), lambda i, j: (0, 0)),
                  pl.BlockSpec((1, d, fs), lambda i, j: (j, 0, 0)),
                  pl.BlockSpec((1, d, fs), lambda i, j: (j, 0, 0)),
                  pl.BlockSpec((fs, d), lambda i, j: (j, 0))],
        out_specs=pl.BlockSpec((tm, d), lambda i, j: (i, 0)),
        out_shape=jax.ShapeDtypeStruct((n, d), F32),
        scratch_shapes=[pltpu.VMEM((tm, d), BF16), pltpu.VMEM((tm, d), F32)],
        compiler_params=_params("parallel", "arbitrary"),
        name="ffn_mixer",
    )(out_a, out_b, w_out, x, mod, g, final_g, w_gate, w_up, w_down)


def _top2(logits):
    lane = lax.broadcasted_iota(jnp.int32, logits.shape, 1)
    lg = jnp.where(lane < N_EXPERTS, logits, -jnp.inf)
    m1 = jnp.max(lg, axis=-1, keepdims=True)
    i1 = jnp.min(jnp.where(lg == m1, lane, LANES), axis=-1, keepdims=True)
    lg2 = jnp.where(lane == i1, -jnp.inf, lg)
    m2 = jnp.max(lg2, axis=-1, keepdims=True)
    i2 = jnp.min(jnp.where(lg2 == m2, lane, LANES), axis=-1, keepdims=True)
    e2 = jnp.exp(m2 - m1)
    den = 1.0 + e2
    return i1, i2, 1.0 / den, e2 / den


def _router_body(x_ref, mod_ref, g_ref, wr_ref, h_ref, post_ref, gatet_ref, cnt_ref):
    t = x_ref.shape[0]
    h = _modnorm(x_ref[...], g_ref[...], mod_ref[0, SC2:SC2 + 1, :], mod_ref[0, SH2:SH2 + 1, :])
    h_hi = h.astype(BF16)
    h_ref[...] = h_hi
    h_lo = (h - h_hi.astype(F32)).astype(BF16)
    wr = wr_ref[...]
    w_hi = wr.astype(BF16)
    w_lo = (wr - w_hi.astype(F32)).astype(BF16)
    logits = (jnp.dot(h_hi, w_hi, preferred_element_type=F32)
              + jnp.dot(h_lo, w_hi, preferred_element_type=F32)
              + jnp.dot(h_hi, w_lo, preferred_element_type=F32))
    i1, i2, w1, w2 = _top2(logits)
    lane = lax.broadcasted_iota(jnp.int32, logits.shape, 1)
    sel1 = lane == i1
    sel2 = lane == i2
    gate = jnp.where(sel1, w1, 0.0) + jnp.where(sel2, w2, 0.0)
    gatet_ref[...] = gate.T[0:N_EXPERTS, :]
    picked = jnp.logical_or(sel1, sel2)
    onehot = jnp.where(picked, 1.0, 0.0)
    row = lax.broadcasted_iota(jnp.int32, (t, t), 0)
    col = lax.broadcasted_iota(jnp.int32, (t, t), 1)
    before = jnp.where(col < row, 1.0, 0.0).astype(BF16)
    rank = jnp.dot(before, onehot.astype(BF16), preferred_element_type=F32)
    pos = jnp.where(picked, rank, -1.0)
    post_ref[...] = pos.T[0:N_EXPERTS, :]
    cnt_ref[0] = jnp.broadcast_to(jnp.sum(onehot, axis=0, keepdims=True), cnt_ref.shape[1:])


def _router(x, mod, g, w_router, seq):
    n, d = x.shape
    t = MOE_TILE
    nt = n // t
    wr = jnp.zeros((d, LANES), F32).at[:, :N_EXPERTS].set(w_router)
    return pl.pallas_call(
        _router_body,
        grid=(nt,),
        in_specs=[pl.BlockSpec((t, d), lambda i: (i, 0)),
                  pl.BlockSpec((1, N_MOD, d), lambda i: (i * t // seq, 0, 0)),
                  pl.BlockSpec((1, d), lambda i: (0, 0)),
                  pl.BlockSpec((d, LANES), lambda i: (0, 0))],
        out_specs=[pl.BlockSpec((t, d), lambda i: (i, 0)),
                   pl.BlockSpec((N_EXPERTS, t), lambda i: (0, i)),
                   pl.BlockSpec((N_EXPERTS, t), lambda i: (0, i)),
                   pl.BlockSpec((1, 8, LANES), lambda i: (i, 0, 0))],
        out_shape=[jax.ShapeDtypeStruct((n, d), BF16),
                   jax.ShapeDtypeStruct((N_EXPERTS, n), F32),
                   jax.ShapeDtypeStruct((N_EXPERTS, n), F32),
                   jax.ShapeDtypeStruct((nt, 8, LANES), F32)],
        compiler_params=_params("parallel"),
        name="router",
    )(x, mod, g, wr)


def _moe_body(cnt_ref, h_ref, post_ref, gatet_ref, x_ref, mod_ref, fg_ref,
              wg_ref, wu_ref, wd_ref, o_ref, xg, acc, *, final):
    ti = pl.program_id(0)
    e = pl.program_id(1)
    s = pl.program_id(2)
    ns = pl.num_programs(2)
    t = h_ref.shape[0]
    cnt = cnt_ref[ti, e]

    @pl.when(jnp.logical_and(e == 0, s == 0))
    def _zero():
        o_ref[...] = jnp.zeros(o_ref.shape, F32)

    def block(b, bm):
        if isinstance(b, int):
            rows = slice(b * bm, (b + 1) * bm)
        else:
            rows = pl.ds(pl.multiple_of(b * bm, 16), bm)

        @pl.when(s == 0)
        def _gather():
            tgt = (lax.broadcasted_iota(jnp.int32, (bm, t), 0) + b * bm).astype(F32)
            onehot = jnp.where(post_ref[pl.ds(e, 1), :] == tgt, 1.0, 0.0).astype(BF16)
            xg[rows, :] = jnp.dot(onehot, h_ref[...], preferred_element_type=F32).astype(BF16)

        xb = xg[rows, :]
        g = jnp.dot(xb, wg_ref[0, 0], preferred_element_type=F32)
        u = jnp.dot(xb, wu_ref[0, 0], preferred_element_type=F32)
        a = (g * jax.nn.sigmoid(g) * u).astype(BF16)
        y = jnp.dot(a, wd_ref[0], preferred_element_type=F32)

        @pl.when(s == 0)
        def _first():
            acc[rows, :] = y

        @pl.when(s != 0)
        def _rest():
            acc[rows, :] += y

        @pl.when(s == ns - 1)
        def _scatter():
            tgt = (lax.broadcasted_iota(jnp.int32, (bm, t), 0) + b * bm).astype(F32)
            hit = post_ref[pl.ds(e, 1), :] == tgt
            gate_rows = jnp.sum(jnp.where(hit, gatet_ref[pl.ds(e, 1), :], 0.0), axis=-1,
                                keepdims=True)
            yb = (acc[rows, :] * gate_rows).astype(BF16)
            o_ref[...] += lax.dot_general(jnp.where(hit, 1.0, 0.0).astype(BF16), yb,
                                          (((0,), (0,)), ((), ())), preferred_element_type=F32)

    lo = MOE_LOOP_BLOCK
    for bm in MOE_BLOCKS:
        @pl.when(jnp.logical_and(cnt > lo, cnt <= bm))
        def _single(bm=bm):
            block(0, bm)
        lo = bm

    @pl.when(jnp.logical_or(cnt <= MOE_LOOP_BLOCK, cnt > lo))
    def _multi():
        bm = MOE_LOOP_BLOCK

        def body(b, carry):
            block(b, bm)
            return carry
        lax.fori_loop(0, (cnt + (bm - 1)) // bm, body, 0)

    @pl.when(jnp.logical_and(e == pl.num_programs(1) - 1, s == ns - 1))
    def _epilogue():
        y = x_ref[...] + mod_ref[0, G2:G2 + 1, :] * o_ref[...]
        if final:
            y = y * lax.rsqrt(jnp.mean(y * y, axis=-1, keepdims=True) + EPS) * fg_ref[...]
        o_ref[...] = y


def _moe(x, mod, g, final_g, w_router, w_gate, w_up, w_down, seq, final):
    n, d = x.shape
    n_e, f, _ = w_down.shape
    t = MOE_TILE
    nt = n // t
    fs = f // MOE_FSPLIT
    h, post, gatet, cnt = _router(x, mod, g, w_router, seq)
    counts = cnt[:, 0, :N_EXPERTS].astype(jnp.int32)
    cap = max(pl.cdiv(t, MOE_LOOP_BLOCK) * MOE_LOOP_BLOCK, MOE_BLOCKS[-1])
    grid_spec = pltpu.PrefetchScalarGridSpec(
        num_scalar_prefetch=1,
        grid=(nt, n_e, MOE_FSPLIT),
        in_specs=[pl.BlockSpec((t, d), lambda i, e, s, c: (i, 0)),
                  pl.BlockSpec((N_EXPERTS, t), lambda i, e, s, c: (0, i)),
                  pl.BlockSpec((N_EXPERTS, t), lambda i, e, s, c: (0, i)),
                  pl.BlockSpec((t, d), lambda i, e, s, c: (i, 0)),
                  pl.BlockSpec((1, N_MOD, d), lambda i, e, s, c: (i * t // seq, 0, 0)),
                  pl.BlockSpec((1, d), lambda i, e, s, c: (0, 0)),
                  pl.BlockSpec((1, 1, d, fs), lambda i, e, s, c: (s, e, 0, 0)),
                  pl.BlockSpec((1, 1, d, fs), lambda i, e, s, c: (s, e, 0, 0)),
                  pl.BlockSpec((1, fs, d), lambda i, e, s, c: (e, s, 0))],
        out_specs=pl.BlockSpec((t, d), lambda i, e, s, c: (i, 0)),
        scratch_shapes=[pltpu.VMEM((cap, d), BF16), pltpu.VMEM((cap, d), F32)],
    )
    return pl.pallas_call(
        functools.partial(_moe_body, final=final),
        grid_spec=grid_spec,
        out_shape=jax.ShapeDtypeStruct((n, d), F32),
        compiler_params=_params("parallel", "arbitrary", "arbitrary"),
        name="moe_mixer",
    )(counts, h, post, gatet, x, mod, final_g, w_gate, w_up, w_down)


def kernel(x, c, w_ada, b_ada, norm1_g, norm2_g, w_in, w_out, gmlp_vnorm_g, gmlp_ws, gmlp_bs,
           lam_q1, lam_k1, lam_q2, lam_k2, diff_subln_g, ffn_w_gate, ffn_w_up, ffn_w_down,
           w_router, moe_w_gate, moe_w_up, moe_w_down, final_g):
    batch, seq, d = x.shape
    depth = w_ada.shape[0]
    n = batch * seq
    mods = _ada(c, w_ada, b_ada)
    xf = x.reshape(n, d)
    fg = final_g.reshape(1, d)
    for l in range(depth):
        mod = mods[l]
        z, vt = _inproj(xf, mod, norm1_g[l].reshape(1, d), w_in[l].astype(BF16), seq)
        out_a = _gmlp(z, gmlp_vnorm_g[l], gmlp_ws[l], gmlp_bs[l])
        lam_init = 0.8 - 0.6 * math.exp(-0.3 * l)
        lam_p = jnp.stack([lam_q1[l], lam_k1[l], lam_q2[l], lam_k2[l]])
        i = l // 2
        if l % 2 == 0:
            casts = [(w[:, None], i, p) for w, p in
                     ((ffn_w_gate, FFN_FSPLIT), (ffn_w_up, FFN_FSPLIT), (ffn_w_down, 1))]
            if l + 1 < depth:
                casts += [(moe_w_gate, i, MOE_FSPLIT), (moe_w_up, i, MOE_FSPLIT)]
        else:
            casts = [(moe_w_down, i, 1)]
        out_b, cast_out = _attn(z, vt, lam_p, diff_subln_g[l], lam_init, batch, seq, casts)
        final = l == depth - 1
        g2 = norm2_g[l].reshape(1, d)
        if l % 2 == 0:
            xf = _ffn(out_a, out_b, w_out[l].astype(BF16), xf, mod, g2, fg,
                      cast_out[0][:, 0], cast_out[1][:, 0], cast_out[2][0, 0], seq, final)
            moe_gate_up = cast_out[3:]
        else:
            xf = _outproj(out_a, out_b, w_out[l].astype(BF16), xf, mod, seq)
            xf = _moe(xf, mod, g2, fg, w_router[i], moe_gate_up[0], moe_gate_up[1], cast_out[0][0],
                      seq, final)
    return xf.reshape(batch, seq, d)
```

```python
import functools
import math

import jax
import jax.numpy as jnp
import numpy as np
from jax import lax
from jax.experimental import pallas as pl
from jax.experimental.pallas import tpu as pltpu

F32 = jnp.float32
BF16 = jnp.bfloat16
EPS = 1e-6
N_MOD = 6
N_GMLP_HEADS = 8
GMLP_HEAD_DIM = 64
CHUNK = 128
N_DIFF_HEADS = 4
DIFF_HEAD_DIM = 64
DIFF_V_DIM = 128
N_EXPERTS = 8
LANES = 128
VMEM_LIMIT = 56 * 1024 * 1024

MOE_TILE = 1024
MOE_BLOCKS = (256, 352, 448)
MOE_LOOP_BLOCK = 160
MOE_FSPLIT = 2

FFN_FSPLIT = 2
FFN_CHUNK = 256

ATTN_BLOCK = 1024
ATTN_ONES_ROWS = 16

SH1, SC1, G1, SH2, SC2, G2 = range(N_MOD)


def _params(*sem):
    return pltpu.CompilerParams(dimension_semantics=sem, vmem_limit_bytes=VMEM_LIMIT)


def _modnorm(x, g, sc, sh):
    ms = jnp.mean(x * x, axis=-1, keepdims=True)
    return (x * lax.rsqrt(ms + EPS)) * g * (1.0 + sc) + sh


def _ada_body(c_ref, w_ref, b_ref, o_ref):
    c = c_ref[...]
    sc = c * jax.nn.sigmoid(c)
    o_ref[0] = jnp.dot(sc, w_ref[0], preferred_element_type=F32,
                       precision=lax.Precision.HIGHEST) + b_ref[0]


def _ada(c, w_ada, b_ada):
    depth, d, m = w_ada.shape
    b = c.shape[0]
    rows = 8
    cp = jnp.zeros((rows, d), F32).at[:b].set(c)
    tn = 3072
    out = pl.pallas_call(
        _ada_body,
        grid=(depth, m // tn),
        in_specs=[pl.BlockSpec((rows, d), lambda l, j: (0, 0)),
                  pl.BlockSpec((1, d, tn), lambda l, j: (l, 0, j)),
                  pl.BlockSpec((1, 1, tn), lambda l, j: (l, 0, j))],
        out_specs=pl.BlockSpec((1, rows, tn), lambda l, j: (l, 0, j)),
        out_shape=jax.ShapeDtypeStruct((depth, rows, m), F32),
        compiler_params=_params("parallel", "parallel"),
        name="ada",
    )(cp, w_ada, b_ada.reshape(depth, 1, m))
    return out[:, :b].reshape(depth, b, N_MOD, d)


def _gmlp_chunk(u, v, vg, wp, bias, gm):
    lane = lax.broadcasted_iota(jnp.int32, (CHUNK, LANES), 1)
    left = lane < GMLP_HEAD_DIM
    gu = jax.nn.gelu(u)
    gv = jax.nn.gelu(v)
    v2 = gv * gv
    hi = v2.astype(BF16)
    lo = (v2 - hi.astype(F32)).astype(BF16)
    ms = (jnp.dot(hi, gm, preferred_element_type=F32)
          + jnp.dot(lo, gm, preferred_element_type=F32))
    vn = gv * lax.rsqrt(ms + EPS) * vg
    outs = []
    for p in range(len(wp)):
        vp = vn[:, p * LANES:(p + 1) * LANES]
        zero = jnp.zeros_like(vp)
        rhs = jnp.concatenate([jnp.where(left, vp, zero), jnp.where(left, zero, vp)],
                              axis=0).astype(BF16)
        outs.append(jnp.dot(wp[p], rhs, preferred_element_type=F32))
    return gu * (jnp.concatenate(outs, axis=1) + bias)


def _inproj_body(x_ref, mod_ref, g_ref, w_ref, wvt_ref, vg_ref, ws_ref, bias_ref, gm_ref,
                 z_ref, vt_ref, a_ref):
    h = _modnorm(x_ref[...], g_ref[...], mod_ref[0, SC1:SC1 + 1, :],
                 mod_ref[0, SH1:SH1 + 1, :]).astype(BF16)
    tm = x_ref.shape[0]
    gw = a_ref.shape[1]
    u = jnp.dot(h, w_ref[:, 0:gw], preferred_element_type=F32)
    v = jnp.dot(h, w_ref[:, gw:2 * gw], preferred_element_type=F32)
    cw = 512
    for c0 in range(0, z_ref.shape[1], cw):
        z_ref[:, c0:c0 + cw] = jnp.dot(h, w_ref[:, 2 * gw + c0:2 * gw + c0 + cw],
                                       preferred_element_type=F32).astype(z_ref.dtype)
    dve = DIFF_V_DIM + ATTN_ONES_ROWS
    vt = lax.dot_general(wvt_ref[...], h, (((1,), (1,)), ((), ())),
                         preferred_element_type=F32).astype(vt_ref.dtype)
    for hd in range(N_DIFF_HEADS):
        vt_ref[hd * dve:hd * dve + DIFF_V_DIM, :] = vt[hd * DIFF_V_DIM:(hd + 1) * DIFF_V_DIM, :]
        vt_ref[hd * dve + DIFF_V_DIM:(hd + 1) * dve, :] = jnp.ones((ATTN_ONES_ROWS, tm), vt_ref.dtype)
    row = lax.broadcasted_iota(jnp.int32, (CHUNK, 2 * CHUNK), 0)
    col = lax.broadcasted_iota(jnp.int32, (CHUNK, 2 * CHUNK), 1)
    causal = jnp.where(col >= CHUNK, col - CHUNK, col) <= row
    wp = [jnp.where(causal, ws_ref[p], jnp.zeros_like(ws_ref[p])) for p in range(ws_ref.shape[0])]
    for c in range(tm // CHUNK):
        rs = slice(c * CHUNK, (c + 1) * CHUNK)
        a_ref[rs, :] = _gmlp_chunk(u[rs, :], v[rs, :], vg_ref[...], wp, bias_ref[...],
                                   gm_ref[...]).astype(a_ref.dtype)


def _inproj(x, mod, g, w, vnorm_g, w_s, b_s, seq):
    n, d = x.shape
    vw = N_DIFF_HEADS * DIFF_V_DIM
    gw = N_GMLP_HEADS * GMLP_HEAD_DIM
    zw = w.shape[1] - vw - 2 * gw
    dve = DIFF_V_DIM + ATTN_ONES_ROWS
    npair = N_GMLP_HEADS // 2
    tm = 512
    w_pair = w_s.reshape(npair, 2, CHUNK, CHUNK).transpose(0, 2, 1, 3).reshape(
        npair, CHUNK, 2 * CHUNK).astype(BF16)
    bias = jnp.repeat(b_s.T, GMLP_HEAD_DIM, axis=1)
    grp = np.arange(gw) // GMLP_HEAD_DIM
    group_mean = jnp.asarray((grp[:, None] == grp[None, :]) / GMLP_HEAD_DIM, dtype=BF16)
    return pl.pallas_call(
        _inproj_body,
        grid=(n // tm,),
        in_specs=[pl.BlockSpec((tm, d), lambda i: (i, 0)),
                  pl.BlockSpec((1, N_MOD, d), lambda i: (i * tm // seq, 0, 0)),
                  pl.BlockSpec((1, d), lambda i: (0, 0)),
                  pl.BlockSpec((d, 2 * gw + zw), lambda i: (0, 0)),
                  pl.BlockSpec((vw, d), lambda i: (0, 0)),
                  pl.BlockSpec((1, gw), lambda i: (0, 0)),
                  pl.BlockSpec((npair, CHUNK, 2 * CHUNK), lambda i: (0, 0, 0)),
                  pl.BlockSpec((CHUNK, gw), lambda i: (0, 0)),
                  pl.BlockSpec((gw, gw), lambda i: (0, 0))],
        out_specs=[pl.BlockSpec((tm, zw), lambda i: (i, 0)),
                   pl.BlockSpec((N_DIFF_HEADS * dve, tm), lambda i: (0, i)),
                   pl.BlockSpec((tm, gw), lambda i: (i, 0))],
        out_shape=[jax.ShapeDtypeStruct((n, zw), BF16),
                   jax.ShapeDtypeStruct((N_DIFF_HEADS * dve, n), BF16),
                   jax.ShapeDtypeStruct((n, gw), BF16)],
        compiler_params=_params("parallel"),
        name="inproj_gmlp",
    )(x, mod, g, w, w[:, 2 * gw + zw:].T, vnorm_g.reshape(1, gw), w_pair, bias, group_mean)


def _attn_body(*refs, tq, lam_init, ncast):
    qi_ref, ki_ref, q_ref, k_ref, vt_ref, lam_ref, g_ref = refs[:7]
    cast_src = refs[7:7 + ncast]
    o_ref = refs[7 + ncast]
    cast_dst = refs[8 + ncast:8 + 2 * ncast]
    qs1, qs2, m1, m2, acc1, acc2 = refs[8 + 2 * ncast:]
    for src, dst in zip(cast_src, cast_dst):
        pw = dst.shape[2]
        for j in range(dst.shape[0]):
            dst[j] = src[:, j * pw:(j + 1) * pw].astype(dst.dtype)
    qi = qi_ref[pl.program_id(2)]
    ki = ki_ref[pl.program_id(2)]
    maps = ((qs1, m1, acc1), (qs2, m2, acc2))
    dv = DIFF_V_DIM

    @pl.when(ki == 0)
    def _init():
        qsc = (q_ref[...].astype(F32) * (DIFF_HEAD_DIM ** -0.5 * math.log2(math.e))).astype(BF16)
        lane = lax.broadcasted_iota(jnp.int32, qsc.shape, 1)
        zero = jnp.zeros_like(qsc)
        qs1[...] = jnp.where(lane < DIFF_HEAD_DIM, qsc, zero)
        qs2[...] = jnp.where(lane < DIFF_HEAD_DIM, zero, qsc)
        for _, m_s, acc in maps:
            m_s[...] = jnp.full(m_s.shape, -jnp.inf, F32)
            acc[...] = jnp.zeros(acc.shape, F32)

    def step(masked):
        k = k_ref[...]
        vt = vt_ref[...]
        sts = [lax.dot_general(k, qs[...], (((1,), (1,)), ((), ())), preferred_element_type=F32)
               for qs, _, _ in maps]
        for st, (_, m_s, acc) in zip(sts, maps):
            if masked:
                key = lax.broadcasted_iota(jnp.int32, st.shape, 0)
                qry = lax.broadcasted_iota(jnp.int32, st.shape, 1)
                st = jnp.where(key <= qry, st, -jnp.inf)
            sb = st.astype(BF16)
            m_prev = m_s[...]
            m_new = jnp.maximum(m_prev, jnp.max(sb, axis=0, keepdims=True).astype(F32))
            alpha = jnp.exp2(m_prev - m_new)
            pt = jnp.exp2(sb - m_new.astype(BF16))
            acc[...] = alpha * acc[...] + jnp.dot(vt, pt, preferred_element_type=F32)
            m_s[...] = m_new

    @pl.when(ki < qi)
    def _full():
        step(False)

    @pl.when(ki == qi)
    def _diag():
        step(True)
        lp = lam_ref[...]
        s1 = jnp.sum(lp[0:1, :] * lp[1:2, :], axis=-1, keepdims=True)
        s2 = jnp.sum(lp[2:3, :] * lp[3:4, :], axis=-1, keepdims=True)
        lam = jnp.exp(s1) - jnp.exp(s2) + lam_init
        ot = (acc1[0:dv, :] / acc1[dv:dv + 1, :]
              - lam * (acc2[0:dv, :] / acc2[dv:dv + 1, :]))
        ot = ot * lax.rsqrt(jnp.mean(ot * ot, axis=0, keepdims=True) + EPS)
        o_ref[...] = (ot.T * g_ref[...] * (1.0 - lam_init)).astype(o_ref.dtype)


def _cast_rows(rows, nsteps):
    for rb in range(16, rows + 1, 16):
        if rows % rb == 0 and rows // rb <= nsteps:
            return rb
    raise ValueError(f"no row block for {rows} rows in {nsteps} steps")


def _attn(z, vt, lam_p, subln_g, lam_init, batch, seq, casts=()):
    n = z.shape[0]
    tq = ATTN_BLOCK
    dve = DIFF_V_DIM + ATTN_ONES_ROWS
    nq = seq // tq
    qcol = 0
    kcol = qcol + N_DIFF_HEADS
    pairs = [(i, j) for i in range(nq) for j in range(i + 1)]
    npair = len(pairs)
    qi_tab = jnp.asarray([p[0] for p in pairs], jnp.int32)
    ki_tab = jnp.asarray([p[1] for p in pairs], jnp.int32)
    nsteps = batch * N_DIFF_HEADS * npair
    cast2d = [w.reshape(-1, w.shape[-1]) for w, _, _ in casts]
    cast_in, cast_out, cast_shape = [], [], []
    for w, i, pieces in casts:
        rows, cols = w.shape[1] * w.shape[2], w.shape[3]
        rb = _cast_rows(rows, nsteps)
        nb = rows // rb

        def blk(b, h, p, nb=nb):
            return jnp.minimum((b * N_DIFF_HEADS + h) * npair + p, nb - 1)
        cast_in.append(pl.BlockSpec(
            (rb, cols), lambda b, h, p, qt, kt, blk=blk, off=i * nb: (off + blk(b, h, p), 0)))
        cast_out.append(pl.BlockSpec(
            (pieces, rb, cols // pieces), lambda b, h, p, qt, kt, blk=blk: (0, blk(b, h, p), 0)))
        cast_shape.append((pieces, rows, cols // pieces))
    grid_spec = pltpu.PrefetchScalarGridSpec(
        num_scalar_prefetch=2,
        grid=(batch, N_DIFF_HEADS, npair),
        in_specs=[
            pl.BlockSpec((tq, LANES), lambda b, h, p, qt, kt: (b * nq + qt[p], qcol + h)),
            pl.BlockSpec((tq, LANES), lambda b, h, p, qt, kt: (b * nq + kt[p], kcol + h)),
            pl.BlockSpec((dve, tq), lambda b, h, p, qt, kt: (h, b * nq + kt[p])),
            pl.BlockSpec((4, DIFF_HEAD_DIM), lambda b, h, p, qt, kt: (0, 0)),
            pl.BlockSpec((1, DIFF_V_DIM), lambda b, h, p, qt, kt: (0, 0)),
        ] + cast_in,
        out_specs=[pl.BlockSpec((tq, LANES), lambda b, h, p, qt, kt: (b * nq + qt[p], h))]
        + cast_out,
        scratch_shapes=(2 * [pltpu.VMEM((tq, LANES), BF16)] + 2 * [pltpu.VMEM((1, tq), F32)]
                        + 2 * [pltpu.VMEM((dve, tq), F32)]),
    )
    outs = pl.pallas_call(
        functools.partial(_attn_body, tq=tq, lam_init=lam_init, ncast=len(casts)),
        grid_spec=grid_spec,
        out_shape=[jax.ShapeDtypeStruct((n, N_DIFF_HEADS * DIFF_V_DIM), BF16)]
        + [jax.ShapeDtypeStruct(s, BF16) for s in cast_shape],
        compiler_params=_params("arbitrary", "arbitrary", "arbitrary"),
        name="diff_attn",
    )(qi_tab, ki_tab, z, z, vt, lam_p, subln_g.reshape(1, DIFF_V_DIM), *cast2d)
    return outs[0], [o.reshape(pieces, w.shape[1], w.shape[2], w.shape[3] // pieces)
                     for o, (w, _, pieces) in zip(outs[1:], casts)]


def _outproj_body(a_ref, b_ref, w_ref, x_ref, mod_ref, o_ref):
    ka = a_ref.shape[1]
    y = jnp.dot(a_ref[...], w_ref[0:ka, :], preferred_element_type=F32)
    y = y + jnp.dot(b_ref[...], w_ref[ka:, :], preferred_element_type=F32)
    o_ref[...] = x_ref[...] + mod_ref[0, G1:G1 + 1, :] * y


def _outproj(out_a, out_b, w, x, mod, seq):
    n, d = x.shape
    tm = 512
    ka, kb = out_a.shape[1], out_b.shape[1]
    return pl.pallas_call(
        _outproj_body,
        grid=(n // tm,),
        in_specs=[pl.BlockSpec((tm, ka), lambda i: (i, 0)),
                  pl.BlockSpec((tm, kb), lambda i: (i, 0)),
                  pl.BlockSpec((ka + kb, d), lambda i: (0, 0)),
                  pl.BlockSpec((tm, d), lambda i: (i, 0)),
                  pl.BlockSpec((1, N_MOD, d), lambda i: (i * tm // seq, 0, 0))],
        out_specs=pl.BlockSpec((tm, d), lambda i: (i, 0)),
        out_shape=jax.ShapeDtypeStruct((n, d), F32),
        compiler_params=_params("parallel"),
        name="outproj",
    )(out_a, out_b, w, x, mod)


def _ffn_body(a_ref, b_ref, wo_ref, x_ref, mod_ref, g_ref, fg_ref, wg_ref, wu_ref, wd_ref, o_ref,
              h_s, acc, *, final):
    j = pl.program_id(1)

    @pl.when(j == 0)
    def _prologue():
        ka = a_ref.shape[1]
        y = jnp.dot(a_ref[...], wo_ref[0:ka, :], preferred_element_type=F32)
        y = y + jnp.dot(b_ref[...], wo_ref[ka:, :], preferred_element_type=F32)
        x = x_ref[...] + mod_ref[0, G1:G1 + 1, :] * y
        o_ref[...] = x
        h = _modnorm(x, g_ref[...], mod_ref[0, SC2:SC2 + 1, :], mod_ref[0, SH2:SH2 + 1, :])
        h_s[...] = h.astype(BF16)
        acc[...] = jnp.zeros(acc.shape, F32)

    h = h_s[...]
    fs = wd_ref.shape[0]
    for c0 in range(0, fs, FFN_CHUNK):
        cs = slice(c0, min(c0 + FFN_CHUNK, fs))
        g = jnp.dot(h, wg_ref[0, :, cs], preferred_element_type=F32)
        u = jnp.dot(h, wu_ref[0, :, cs], preferred_element_type=F32)
        a = (g * jax.nn.sigmoid(g) * u).astype(BF16)
        acc[...] += jnp.dot(a, wd_ref[cs, :], preferred_element_type=F32)

    @pl.when(j == pl.num_programs(1) - 1)
    def _epilogue():
        y = o_ref[...] + mod_ref[0, G2:G2 + 1, :] * acc[...]
        if final:
            y = y * lax.rsqrt(jnp.mean(y * y, axis=-1, keepdims=True) + EPS) * fg_ref[...]
        o_ref[...] = y


def _ffn(out_a, out_b, w_out, x, mod, g, final_g, w_gate, w_up, w_down, seq, final):
    n, d = x.shape
    nsplit, _, fs = w_gate.shape
    ka, kb = out_a.shape[1], out_b.shape[1]
    tm = 1024
    return pl.pallas_call(
        functools.partial(_ffn_body, final=final),
        grid=(n // tm, nsplit),
        in_specs=[pl.BlockSpec((tm, ka), lambda i, j: (i, 0)),
                  pl.BlockSpec((tm, kb), lambda i, j: (i, 0)),
                  pl.BlockSpec((ka + kb, d), lambda i, j: (0, 0)),
                  pl.BlockSpec((tm, d), lambda i, j: (i, 0)),
                  pl.BlockSpec((1, N_MOD, d), lambda i, j: (i * tm // seq, 0, 0)),
                  pl.BlockSpec((1, d), lambda i, j: (0, 0)),
                  pl.BlockSpec((1, d), lambda i, j: (0, 0)),
                  pl.BlockSpec((1, d, fs), lambda i, j: (j, 0, 0)),
                  pl.BlockSpec((1, d, fs), lambda i, j: (j, 0, 0)),
                  pl.BlockSpec((fs, d), lambda i, j: (j, 0))],
        out_specs=pl.BlockSpec((tm, d), lambda i, j: (i, 0)),
        out_shape=jax.ShapeDtypeStruct((n, d), F32),
        scratch_shapes=[pltpu.VMEM((tm, d), BF16), pltpu.VMEM((tm, d), F32)],
        compiler_params=_params("parallel", "arbitrary"),
        name="ffn_mixer",
    )(out_a, out_b, w_out, x, mod, g, final_g, w_gate, w_up, w_down)


def _top2(logits):
    lane = lax.broadcasted_iota(jnp.int32, logits.shape, 1)
    lg = jnp.where(lane < N_EXPERTS, logits, -jnp.inf)
    m1 = jnp.max(lg, axis=-1, keepdims=True)
    i1 = jnp.min(jnp.where(lg == m1, lane, LANES), axis=-1, keepdims=True)
    lg2 = jnp.where(lane == i1, -jnp.inf, lg)
    m2 = jnp.max(lg2, axis=-1, keepdims=True)
    i2 = jnp.min(jnp.where(lg2 == m2, lane, LANES), axis=-1, keepdims=True)
    e2 = jnp.exp(m2 - m1)
    den = 1.0 + e2
    return i1, i2, 1.0 / den, e2 / den


def _router_body(x_ref, mod_ref, g_ref, wr_ref, h_ref, post_ref, gatet_ref, cnt_ref):
    t = x_ref.shape[0]
    h = _modnorm(x_ref[...], g_ref[...], mod_ref[0, SC2:SC2 + 1, :], mod_ref[0, SH2:SH2 + 1, :])
    h_hi = h.astype(BF16)
    h_ref[...] = h_hi
    h_lo = (h - h_hi.astype(F32)).astype(BF16)
    wr = wr_ref[...]
    w_hi = wr.astype(BF16)
    w_lo = (wr - w_hi.astype(F32)).astype(BF16)
    logits = (jnp.dot(h_hi, w_hi, preferred_element_type=F32)
              + jnp.dot(h_lo, w_hi, preferred_element_type=F32)
              + jnp.dot(h_hi, w_lo, preferred_element_type=F32))
    i1, i2, w1, w2 = _top2(logits)
    lane = lax.broadcasted_iota(jnp.int32, logits.shape, 1)
    sel1 = lane == i1
    sel2 = lane == i2
    gate = jnp.where(sel1, w1, 0.0) + jnp.where(sel2, w2, 0.0)
    gatet_ref[...] = gate.T[0:N_EXPERTS, :]
    picked = jnp.logical_or(sel1, sel2)
    onehot = jnp.where(picked, 1.0, 0.0)
    row = lax.broadcasted_iota(jnp.int32, (t, t), 0)
    col = lax.broadcasted_iota(jnp.int32, (t, t), 1)
    before = jnp.where(col < row, 1.0, 0.0).astype(BF16)
    rank = jnp.dot(before, onehot.astype(BF16), preferred_element_type=F32)
    pos = jnp.where(picked, rank, -1.0)
    post_ref[...] = pos.T[0:N_EXPERTS, :]
    cnt_ref[0] = jnp.broadcast_to(jnp.sum(onehot, axis=0, keepdims=True), cnt_ref.shape[1:])


def _router(x, mod, g, w_router, seq):
    n, d = x.shape
    t = MOE_TILE
    nt = n // t
    wr = jnp.zeros((d, LANES), F32).at[:, :N_EXPERTS].set(w_router)
    return pl.pallas_call(
        _router_body,
        grid=(nt,),
        in_specs=[pl.BlockSpec((t, d), lambda i: (i, 0)),
                  pl.BlockSpec((1, N_MOD, d), lambda i: (i * t // seq, 0, 0)),
                  pl.BlockSpec((1, d), lambda i: (0, 0)),
                  pl.BlockSpec((d, LANES), lambda i: (0, 0))],
        out_specs=[pl.BlockSpec((t, d), lambda i: (i, 0)),
                   pl.BlockSpec((N_EXPERTS, t), lambda i: (0, i)),
                   pl.BlockSpec((N_EXPERTS, t), lambda i: (0, i)),
                   pl.BlockSpec((1, 8, LANES), lambda i: (i, 0, 0))],
        out_shape=[jax.ShapeDtypeStruct((n, d), BF16),
                   jax.ShapeDtypeStruct((N_EXPERTS, n), F32),
                   jax.ShapeDtypeStruct((N_EXPERTS, n), F32),
                   jax.ShapeDtypeStruct((nt, 8, LANES), F32)],
        compiler_params=_params("parallel"),
        name="router",
    )(x, mod, g, wr)


def _moe_body(cnt_ref, h_ref, post_ref, gatet_ref, x_ref, mod_ref, fg_ref,
              wg_ref, wu_ref, wd_ref, o_ref, xg, acc, *, final):
    ti = pl.program_id(0)
    e = pl.program_id(1)
    s = pl.program_id(2)
    ns = pl.num_programs(2)
    t = h_ref.shape[0]
    cnt = cnt_ref[ti, e]

    @pl.when(jnp.logical_and(e == 0, s == 0))
    def _zero():
        o_ref[...] = jnp.zeros(o_ref.shape, F32)

    def block(b, bm):
        if isinstance(b, int):
            rows = slice(b * bm, (b + 1) * bm)
        else:
            rows = pl.ds(pl.multiple_of(b * bm, 16), bm)

        @pl.when(s == 0)
        def _gather():
            tgt = (lax.broadcasted_iota(jnp.int32, (bm, t), 0) + b * bm).astype(F32)
            onehot = jnp.where(post_ref[pl.ds(e, 1), :] == tgt, 1.0, 0.0).astype(BF16)
            xg[rows, :] = jnp.dot(onehot, h_ref[...], preferred_element_type=F32).astype(BF16)

        xb = xg[rows, :]
        g = jnp.dot(xb, wg_ref[0, 0], preferred_element_type=F32)
        u = jnp.dot(xb, wu_ref[0, 0], preferred_element_type=F32)
        a = (g * jax.nn.sigmoid(g) * u).astype(BF16)
        y = jnp.dot(a, wd_ref[0], preferred_element_type=F32)

        @pl.when(s == 0)
        def _first():
            acc[rows, :] = y

        @pl.when(s != 0)
        def _rest():
            acc[rows, :] += y

        @pl.when(s == ns - 1)
        def _scatter():
            tgt = (lax.broadcasted_iota(jnp.int32, (bm, t), 0) + b * bm).astype(F32)
            hit = post_ref[pl.ds(e, 1), :] == tgt
            gate_rows = jnp.sum(jnp.where(hit, gatet_ref[pl.ds(e, 1), :], 0.0), axis=-1,
                                keepdims=True)
            yb = (acc[rows, :] * gate_rows).astype(BF16)
            o_ref[...] += lax.dot_general(jnp.where(hit, 1.0, 0.0).astype(BF16), yb,
                                          (((0,), (0,)), ((), ())), preferred_element_type=F32)

    lo = MOE_LOOP_BLOCK
    for bm in MOE_BLOCKS:
        @pl.when(jnp.logical_and(cnt > lo, cnt <= bm))
        def _single(bm=bm):
            block(0, bm)
        lo = bm

    @pl.when(jnp.logical_or(cnt <= MOE_LOOP_BLOCK, cnt > lo))
    def _multi():
        bm = MOE_LOOP_BLOCK

        def body(b, carry):
            block(b, bm)
            return carry
        lax.fori_loop(0, (cnt + (bm - 1)) // bm, body, 0)

    @pl.when(jnp.logical_and(e == pl.num_programs(1) - 1, s == ns - 1))
    def _epilogue():
        y = x_ref[...] + mod_ref[0, G2:G2 + 1, :] * o_ref[...]
        if final:
            y = y * lax.rsqrt(jnp.mean(y * y, axis=-1, keepdims=True) + EPS) * fg_ref[...]
        o_ref[...] = y


def _moe(x, mod, g, final_g, w_router, w_gate, w_up, w_down, seq, final):
    n, d = x.shape
    n_e, f, _ = w_down.shape
    t = MOE_TILE
    nt = n // t
    fs = f // MOE_FSPLIT
    h, post, gatet, cnt = _router(x, mod, g, w_router, seq)
    counts = cnt[:, 0, :N_EXPERTS].astype(jnp.int32)
    cap = max(pl.cdiv(t, MOE_LOOP_BLOCK) * MOE_LOOP_BLOCK, MOE_BLOCKS[-1])
    grid_spec = pltpu.PrefetchScalarGridSpec(
        num_scalar_prefetch=1,
        grid=(nt, n_e, MOE_FSPLIT),
        in_specs=[pl.BlockSpec((t, d), lambda i, e, s, c: (i, 0)),
                  pl.BlockSpec((N_EXPERTS, t), lambda i, e, s, c: (0, i)),
                  pl.BlockSpec((N_EXPERTS, t), lambda i, e, s, c: (0, i)),
                  pl.BlockSpec((t, d), lambda i, e, s, c: (i, 0)),
                  pl.BlockSpec((1, N_MOD, d), lambda i, e, s, c: (i * t // seq, 0, 0)),
                  pl.BlockSpec((1, d), lambda i, e, s, c: (0, 0)),
                  pl.BlockSpec((1, 1, d, fs), lambda i, e, s, c: (s, e, 0, 0)),
                  pl.BlockSpec((1, 1, d, fs), lambda i, e, s, c: (s, e, 0, 0)),
                  pl.BlockSpec((1, fs, d), lambda i, e, s, c: (e, s, 0))],
        out_specs=pl.BlockSpec((t, d), lambda i, e, s, c: (i, 0)),
        scratch_shapes=[pltpu.VMEM((cap, d), BF16), pltpu.VMEM((cap, d), F32)],
    )
    return pl.pallas_call(
        functools.partial(_moe_body, final=final),
        grid_spec=grid_spec,
        out_shape=jax.ShapeDtypeStruct((n, d), F32),
        compiler_params=_params("parallel", "arbitrary", "arbitrary"),
        name="moe_mixer",
    )(counts, h, post, gatet, x, mod, final_g, w_gate, w_up, w_down)


def kernel(x, c, w_ada, b_ada, norm1_g, norm2_g, w_in, w_out, gmlp_vnorm_g, gmlp_ws, gmlp_bs,
           lam_q1, lam_k1, lam_q2, lam_k2, diff_subln_g, ffn_w_gate, ffn_w_up, ffn_w_down,
           w_router, moe_w_gate, moe_w_up, moe_w_down, final_g):
    batch, seq, d = x.shape
    depth = w_ada.shape[0]
    n = batch * seq
    mods = _ada(c, w_ada, b_ada)
    xf = x.reshape(n, d)
    fg = final_g.reshape(1, d)
    for l in range(depth):
        mod = mods[l]
        z, vt, out_a = _inproj(xf, mod, norm1_g[l].reshape(1, d), w_in[l].astype(BF16),
                               gmlp_vnorm_g[l], gmlp_ws[l], gmlp_bs[l], seq)
        lam_init = 0.8 - 0.6 * math.exp(-0.3 * l)
        lam_p = jnp.stack([lam_q1[l], lam_k1[l], lam_q2[l], lam_k2[l]])
        i = l // 2
        if l % 2 == 0:
            casts = [(w[:, None], i, p) for w, p in
                     ((ffn_w_gate, FFN_FSPLIT), (ffn_w_up, FFN_FSPLIT), (ffn_w_down, 1))]
            if l + 1 < depth:
                casts += [(moe_w_gate, i, MOE_FSPLIT), (moe_w_up, i, MOE_FSPLIT)]
        else:
            casts = [(moe_w_down, i, 1)]
        out_b, cast_out = _attn(z, vt, lam_p, diff_subln_g[l], lam_init, batch, seq, casts)
        final = l == depth - 1
        g2 = norm2_g[l].reshape(1, d)
        if l % 2 == 0:
            xf = _ffn(out_a, out_b, w_out[l].astype(BF16), xf, mod, g2, fg,
                      cast_out[0][:, 0], cast_out[1][:, 0], cast_out[2][0, 0], seq, final)
            moe_gate_up = cast_out[3:]
        else:
            xf = _outproj(out_a, out_b, w_out[l].astype(BF16), xf, mod, seq)
            xf = _moe(xf, mod, g2, fg, w_router[i], moe_gate_up[0], moe_gate_up[1], cast_out[0][0],
                      seq, final)
    return xf.reshape(batch, seq, d)
```

```python
import functools
import math

import jax
import jax.numpy as jnp
import numpy as np
from jax import lax
from jax.experimental import pallas as pl
from jax.experimental.pallas import tpu as pltpu

F32 = jnp.float32
BF16 = jnp.bfloat16
EPS = 1e-6
N_MOD = 6
N_GMLP_HEADS = 8
GMLP_HEAD_DIM = 64
CHUNK = 128
N_DIFF_HEADS = 4
DIFF_HEAD_DIM = 64
DIFF_V_DIM = 128
N_EXPERTS = 8
LANES = 128
VMEM_LIMIT = 56 * 1024 * 1024

MOE_TILE = 1024
MOE_BLOCKS = (256, 352, 448)
MOE_LOOP_BLOCK = 160
MOE_FSPLIT = 2

FFN_FSPLIT = 2
FFN_CHUNK = 256

ATTN_BLOCK = 1024
ATTN_ONES_ROWS = 16

SH1, SC1, G1, SH2, SC2, G2 = range(N_MOD)


def _params(*sem):
    return pltpu.CompilerParams(dimension_semantics=sem, vmem_limit_bytes=VMEM_LIMIT)


def _modnorm(x, g, sc, sh):
    ms = jnp.mean(x * x, axis=-1, keepdims=True)
    return (x * lax.rsqrt(ms + EPS)) * g * (1.0 + sc) + sh


def _ada_body(c_ref, w_ref, b_ref, o_ref):
    c = c_ref[...]
    sc = c * jax.nn.sigmoid(c)
    o_ref[0] = jnp.dot(sc, w_ref[0], preferred_element_type=F32,
                       precision=lax.Precision.HIGHEST) + b_ref[0]


def _ada(c, w_ada, b_ada):
    depth, d, m = w_ada.shape
    b = c.shape[0]
    rows = 8
    cp = jnp.zeros((rows, d), F32).at[:b].set(c)
    tn = 3072
    out = pl.pallas_call(
        _ada_body,
        grid=(depth, m // tn),
        in_specs=[pl.BlockSpec((rows, d), lambda l, j: (0, 0)),
                  pl.BlockSpec((1, d, tn), lambda l, j: (l, 0, j)),
                  pl.BlockSpec((1, 1, tn), lambda l, j: (l, 0, j))],
        out_specs=pl.BlockSpec((1, rows, tn), lambda l, j: (l, 0, j)),
        out_shape=jax.ShapeDtypeStruct((depth, rows, m), F32),
        compiler_params=_params("parallel", "parallel"),
        name="ada",
    )(cp, w_ada, b_ada.reshape(depth, 1, m))
    return out[:, :b].reshape(depth, b, N_MOD, d)


def _gmlp_chunk(u, v, vg, wp, bias, gm):
    lane = lax.broadcasted_iota(jnp.int32, (CHUNK, LANES), 1)
    left = lane < GMLP_HEAD_DIM
    gu = jax.nn.gelu(u)
    gv = jax.nn.gelu(v)
    v2 = gv * gv
    hi = v2.astype(BF16)
    lo = (v2 - hi.astype(F32)).astype(BF16)
    ms = (jnp.dot(hi, gm, preferred_element_type=F32)
          + jnp.dot(lo, gm, preferred_element_type=F32))
    vn = gv * lax.rsqrt(ms + EPS) * vg
    outs = []
    for p in range(len(wp)):
        vp = vn[:, p * LANES:(p + 1) * LANES]
        zero = jnp.zeros_like(vp)
        rhs = jnp.concatenate([jnp.where(left, vp, zero), jnp.where(left, zero, vp)],
                              axis=0).astype(BF16)
        outs.append(jnp.dot(wp[p], rhs, preferred_element_type=F32))
    return gu * (jnp.concatenate(outs, axis=1) + bias)


def _inproj_body(x_ref, mod_ref, g_ref, w_ref, wvt_ref, vg_ref, ws_ref, bias_ref, gm_ref,
                 z_ref, vt_ref, a_ref):
    h = _modnorm(x_ref[...], g_ref[...], mod_ref[0, SC1:SC1 + 1, :],
                 mod_ref[0, SH1:SH1 + 1, :]).astype(BF16)
    tm = x_ref.shape[0]
    gw = a_ref.shape[1]
    u = jnp.dot(h, w_ref[:, 0:gw], preferred_element_type=F32)
    v = jnp.dot(h, w_ref[:, gw:2 * gw], preferred_element_type=F32)
    cw = 512
    for c0 in range(0, z_ref.shape[1], cw):
        z_ref[:, c0:c0 + cw] = jnp.dot(h, w_ref[:, 2 * gw + c0:2 * gw + c0 + cw],
                                       preferred_element_type=F32).astype(z_ref.dtype)
    dve = DIFF_V_DIM + ATTN_ONES_ROWS
    vt = lax.dot_general(wvt_ref[...], h, (((1,), (1,)), ((), ())),
                         preferred_element_type=F32).astype(vt_ref.dtype)
    for hd in range(N_DIFF_HEADS):
        vt_ref[hd * dve:hd * dve + DIFF_V_DIM, :] = vt[hd * DIFF_V_DIM:(hd + 1) * DIFF_V_DIM, :]
        vt_ref[hd * dve + DIFF_V_DIM:(hd + 1) * dve, :] = jnp.ones((ATTN_ONES_ROWS, tm), vt_ref.dtype)
    row = lax.broadcasted_iota(jnp.int32, (CHUNK, 2 * CHUNK), 0)
    col = lax.broadcasted_iota(jnp.int32, (CHUNK, 2 * CHUNK), 1)
    causal = jnp.where(col >= CHUNK, col - CHUNK, col) <= row
    wp = [jnp.where(causal, ws_ref[p], jnp.zeros_like(ws_ref[p])) for p in range(ws_ref.shape[0])]
    for c in range(tm // CHUNK):
        rs = slice(c * CHUNK, (c + 1) * CHUNK)
        a_ref[rs, :] = _gmlp_chunk(u[rs, :], v[rs, :], vg_ref[...], wp, bias_ref[...],
                                   gm_ref[...]).astype(a_ref.dtype)


def _inproj(x, mod, g, w, vnorm_g, w_s, b_s, seq):
    n, d = x.shape
    vw = N_DIFF_HEADS * DIFF_V_DIM
    gw = N_GMLP_HEADS * GMLP_HEAD_DIM
    zw = w.shape[1] - vw - 2 * gw
    dve = DIFF_V_DIM + ATTN_ONES_ROWS
    npair = N_GMLP_HEADS // 2
    tm = 512
    w_pair = w_s.reshape(npair, 2, CHUNK, CHUNK).transpose(0, 2, 1, 3).reshape(
        npair, CHUNK, 2 * CHUNK).astype(BF16)
    bias = jnp.repeat(b_s.T, GMLP_HEAD_DIM, axis=1)
    grp = np.arange(gw) // GMLP_HEAD_DIM
    group_mean = jnp.asarray((grp[:, None] == grp[None, :]) / GMLP_HEAD_DIM, dtype=BF16)
    return pl.pallas_call(
        _inproj_body,
        grid=(n // tm,),
        in_specs=[pl.BlockSpec((tm, d), lambda i: (i, 0)),
                  pl.BlockSpec((1, N_MOD, d), lambda i: (i * tm // seq, 0, 0)),
                  pl.BlockSpec((1, d), lambda i: (0, 0)),
                  pl.BlockSpec((d, 2 * gw + zw), lambda i: (0, 0)),
                  pl.BlockSpec((vw, d), lambda i: (0, 0)),
                  pl.BlockSpec((1, gw), lambda i: (0, 0)),
                  pl.BlockSpec((npair, CHUNK, 2 * CHUNK), lambda i: (0, 0, 0)),
                  pl.BlockSpec((CHUNK, gw), lambda i: (0, 0)),
                  pl.BlockSpec((gw, gw), lambda i: (0, 0))],
        out_specs=[pl.BlockSpec((tm, zw), lambda i: (i, 0)),
                   pl.BlockSpec((N_DIFF_HEADS * dve, tm), lambda i: (0, i)),
                   pl.BlockSpec((tm, gw), lambda i: (i, 0))],
        out_shape=[jax.ShapeDtypeStruct((n, zw), BF16),
                   jax.ShapeDtypeStruct((N_DIFF_HEADS * dve, n), BF16),
                   jax.ShapeDtypeStruct((n, gw), BF16)],
        compiler_params=_params("parallel"),
        name="inproj_gmlp",
    )(x, mod, g, w, w[:, 2 * gw + zw:].T, vnorm_g.reshape(1, gw), w_pair, bias, group_mean)


def _attn_body(*refs, tq, lam_init, ncast):
    qi_ref, ki_ref, q_ref, k_ref, vt_ref, lam_ref, g_ref = refs[:7]
    cast_src = refs[7:7 + ncast]
    o_ref = refs[7 + ncast]
    cast_dst = refs[8 + ncast:8 + 2 * ncast]
    qs1, qs2, m1, m2, acc1, acc2 = refs[8 + 2 * ncast:]
    for src, dst in zip(cast_src, cast_dst):
        pw = dst.shape[2]
        for j in range(dst.shape[0]):
            dst[j] = src[:, j * pw:(j + 1) * pw].astype(dst.dtype)
    qi = qi_ref[pl.program_id(2)]
    ki = ki_ref[pl.program_id(2)]
    maps = ((qs1, m1, acc1), (qs2, m2, acc2))
    dv = DIFF_V_DIM

    @pl.when(ki == 0)
    def _init():
        qsc = (q_ref[...].astype(F32) * (DIFF_HEAD_DIM ** -0.5 * math.log2(math.e))).astype(BF16)
        lane = lax.broadcasted_iota(jnp.int32, qsc.shape, 1)
        zero = jnp.zeros_like(qsc)
        qs1[...] = jnp.where(lane < DIFF_HEAD_DIM, qsc, zero)
        qs2[...] = jnp.where(lane < DIFF_HEAD_DIM, zero, qsc)
        for _, m_s, acc in maps:
            m_s[...] = jnp.full(m_s.shape, -jnp.inf, F32)
            acc[...] = jnp.zeros(acc.shape, F32)

    def step(masked):
        k = k_ref[...]
        vt = vt_ref[...]
        sts = [lax.dot_general(k, qs[...], (((1,), (1,)), ((), ())), preferred_element_type=F32)
               for qs, _, _ in maps]
        for st, (_, m_s, acc) in zip(sts, maps):
            if masked:
                key = lax.broadcasted_iota(jnp.int32, st.shape, 0)
                qry = lax.broadcasted_iota(jnp.int32, st.shape, 1)
                st = jnp.where(key <= qry, st, -jnp.inf)
            sb = st.astype(BF16)
            m_prev = m_s[...]
            m_new = jnp.maximum(m_prev, jnp.max(sb, axis=0, keepdims=True).astype(F32))
            alpha = jnp.exp2(m_prev - m_new)
            pt = jnp.exp2(sb - m_new.astype(BF16))
            acc[...] = alpha * acc[...] + jnp.dot(vt, pt, preferred_element_type=F32)
            m_s[...] = m_new

    @pl.when(ki < qi)
    def _full():
        step(False)

    @pl.when(ki == qi)
    def _diag():
        step(True)
        lp = lam_ref[...]
        s1 = jnp.sum(lp[0:1, :] * lp[1:2, :], axis=-1, keepdims=True)
        s2 = jnp.sum(lp[2:3, :] * lp[3:4, :], axis=-1, keepdims=True)
        lam = jnp.exp(s1) - jnp.exp(s2) + lam_init
        ot = (acc1[0:dv, :] / acc1[dv:dv + 1, :]
              - lam * (acc2[0:dv, :] / acc2[dv:dv + 1, :]))
        ot = ot * lax.rsqrt(jnp.mean(ot * ot, axis=0, keepdims=True) + EPS)
        o_ref[...] = (ot.T * g_ref[...] * (1.0 - lam_init)).astype(o_ref.dtype)


def _cast_rows(rows, nsteps):
    for rb in range(16, rows + 1, 16):
        if rows % rb == 0 and rows // rb <= nsteps:
            return rb
    raise ValueError(f"no row block for {rows} rows in {nsteps} steps")


def _attn(z, vt, lam_p, subln_g, lam_init, batch, seq, casts=()):
    n = z.shape[0]
    tq = ATTN_BLOCK
    dve = DIFF_V_DIM + ATTN_ONES_ROWS
    nq = seq // tq
    qcol = 0
    kcol = qcol + N_DIFF_HEADS
    pairs = [(i, j) for i in range(nq) for j in range(i + 1)]
    npair = len(pairs)
    qi_tab = jnp.asarray([p[0] for p in pairs], jnp.int32)
    ki_tab = jnp.asarray([p[1] for p in pairs], jnp.int32)
    nsteps = batch * N_DIFF_HEADS * npair
    cast2d = [w.reshape(-1, w.shape[-1]) for w, _, _ in casts]
    cast_in, cast_out, cast_shape = [], [], []
    for w, i, pieces in casts:
        rows, cols = w.shape[1] * w.shape[2], w.shape[3]
        rb = _cast_rows(rows, nsteps)
        nb = rows // rb

        def blk(b, h, p, nb=nb):
            return jnp.minimum((b * N_DIFF_HEADS + h) * npair + p, nb - 1)
        cast_in.append(pl.BlockSpec(
            (rb, cols), lambda b, h, p, qt, kt, blk=blk, off=i * nb: (off + blk(b, h, p), 0)))
        cast_out.append(pl.BlockSpec(
            (pieces, rb, cols // pieces), lambda b, h, p, qt, kt, blk=blk: (0, blk(b, h, p), 0)))
        cast_shape.append((pieces, rows, cols // pieces))
    grid_spec = pltpu.PrefetchScalarGridSpec(
        num_scalar_prefetch=2,
        grid=(batch, N_DIFF_HEADS, npair),
        in_specs=[
            pl.BlockSpec((tq, LANES), lambda b, h, p, qt, kt: (b * nq + qt[p], qcol + h)),
            pl.BlockSpec((tq, LANES), lambda b, h, p, qt, kt: (b * nq + kt[p], kcol + h)),
            pl.BlockSpec((dve, tq), lambda b, h, p, qt, kt: (h, b * nq + kt[p])),
            pl.BlockSpec((4, DIFF_HEAD_DIM), lambda b, h, p, qt, kt: (0, 0)),
            pl.BlockSpec((1, DIFF_V_DIM), lambda b, h, p, qt, kt: (0, 0)),
        ] + cast_in,
        out_specs=[pl.BlockSpec((tq, LANES), lambda b, h, p, qt, kt: (b * nq + qt[p], h))]
        + cast_out,
        scratch_shapes=(2 * [pltpu.VMEM((tq, LANES), BF16)] + 2 * [pltpu.VMEM((1, tq), F32)]
                        + 2 * [pltpu.VMEM((dve, tq), F32)]),
    )
    outs = pl.pallas_call(
        functools.partial(_attn_body, tq=tq, lam_init=lam_init, ncast=len(casts)),
        grid_spec=grid_spec,
        out_shape=[jax.ShapeDtypeStruct((n, N_DIFF_HEADS * DIFF_V_DIM), BF16)]
        + [jax.ShapeDtypeStruct(s, BF16) for s in cast_shape],
        compiler_params=_params("arbitrary", "arbitrary", "arbitrary"),
        name="diff_attn",
    )(qi_tab, ki_tab, z, z, vt, lam_p, subln_g.reshape(1, DIFF_V_DIM), *cast2d)
    return outs[0], [o.reshape(pieces, w.shape[1], w.shape[2], w.shape[3] // pieces)
                     for o, (w, _, pieces) in zip(outs[1:], casts)]


def _ffn_body(a_ref, b_ref, wo_ref, x_ref, mod_ref, g_ref, fg_ref, wg_ref, wu_ref, wd_ref, o_ref,
              h_s, acc, *, final):
    j = pl.program_id(1)

    @pl.when(j == 0)
    def _prologue():
        ka = a_ref.shape[1]
        y = jnp.dot(a_ref[...], wo_ref[0:ka, :], preferred_element_type=F32)
        y = y + jnp.dot(b_ref[...], wo_ref[ka:, :], preferred_element_type=F32)
        x = x_ref[...] + mod_ref[0, G1:G1 + 1, :] * y
        o_ref[...] = x
        h = _modnorm(x, g_ref[...], mod_ref[0, SC2:SC2 + 1, :], mod_ref[0, SH2:SH2 + 1, :])
        h_s[...] = h.astype(BF16)
        acc[...] = jnp.zeros(acc.shape, F32)

    h = h_s[...]
    fs = wd_ref.shape[0]
    for c0 in range(0, fs, FFN_CHUNK):
        cs = slice(c0, min(c0 + FFN_CHUNK, fs))
        g = jnp.dot(h, wg_ref[0, :, cs], preferred_element_type=F32)
        u = jnp.dot(h, wu_ref[0, :, cs], preferred_element_type=F32)
        a = (g * jax.nn.sigmoid(g) * u).astype(BF16)
        acc[...] += jnp.dot(a, wd_ref[cs, :], preferred_element_type=F32)

    @pl.when(j == pl.num_programs(1) - 1)
    def _epilogue():
        y = o_ref[...] + mod_ref[0, G2:G2 + 1, :] * acc[...]
        if final:
            y = y * lax.rsqrt(jnp.mean(y * y, axis=-1, keepdims=True) + EPS) * fg_ref[...]
        o_ref[...] = y


def _ffn(out_a, out_b, w_out, x, mod, g, final_g, w_gate, w_up, w_down, seq, final):
    n, d = x.shape
    nsplit, _, fs = w_gate.shape
    ka, kb = out_a.shape[1], out_b.shape[1]
    tm = 1024
    return pl.pallas_call(
        functools.partial(_ffn_body, final=final),
        grid=(n // tm, nsplit),
        in_specs=[pl.BlockSpec((tm, ka), lambda i, j: (i, 0)),
                  pl.BlockSpec((tm, kb), lambda i, j: (i, 0)),
                  pl.BlockSpec((ka + kb, d), lambda i, j: (0, 0)),
                  pl.BlockSpec((tm, d), lambda i, j: (i, 0)),
                  pl.BlockSpec((1, N_MOD, d), lambda i, j: (i * tm // seq, 0, 0)),
                  pl.BlockSpec((1, d), lambda i, j: (0, 0)),
                  pl.BlockSpec((1, d), lambda i, j: (0, 0)),
                  pl.BlockSpec((1, d, fs), lambda i, j: (j, 0, 0)),
                  pl.BlockSpec((1, d, fs), lambda i, j: (j, 0, 0)),
                  pl.BlockSpec((fs, d), lambda i, j: (j, 0))],
        out_specs=pl.BlockSpec((tm, d), lambda i, j: (i, 0)),
        out_shape=jax.ShapeDtypeStruct((n, d), F32),
        scratch_shapes=[pltpu.VMEM((tm, d), BF16), pltpu.VMEM((tm, d), F32)],
        compiler_params=_params("parallel", "arbitrary"),
        name="ffn_mixer",
    )(out_a, out_b, w_out, x, mod, g, final_g, w_gate, w_up, w_down)


def _top2(logits):
    lane = lax.broadcasted_iota(jnp.int32, logits.shape, 1)
    lg = jnp.where(lane < N_EXPERTS, logits, -jnp.inf)
    m1 = jnp.max(lg, axis=-1, keepdims=True)
    i1 = jnp.min(jnp.where(lg == m1, lane, LANES), axis=-1, keepdims=True)
    lg2 = jnp.where(lane == i1, -jnp.inf, lg)
    m2 = jnp.max(lg2, axis=-1, keepdims=True)
    i2 = jnp.min(jnp.where(lg2 == m2, lane, LANES), axis=-1, keepdims=True)
    e2 = jnp.exp(m2 - m1)
    den = 1.0 + e2
    return i1, i2, 1.0 / den, e2 / den


def _router_body(a_ref, b_ref, wo_ref, x_ref, mod_ref, g_ref, wr_ref,
                 xo_ref, h_ref, post_ref, gatet_ref, cnt_ref):
    t = x_ref.shape[0]
    ka = a_ref.shape[1]
    y = jnp.dot(a_ref[...], wo_ref[0:ka, :], preferred_element_type=F32)
    y = y + jnp.dot(b_ref[...], wo_ref[ka:, :], preferred_element_type=F32)
    x = x_ref[...] + mod_ref[0, G1:G1 + 1, :] * y
    xo_ref[...] = x
    h = _modnorm(x, g_ref[...], mod_ref[0, SC2:SC2 + 1, :], mod_ref[0, SH2:SH2 + 1, :])
    h_hi = h.astype(BF16)
    h_ref[...] = h_hi
    h_lo = (h - h_hi.astype(F32)).astype(BF16)
    wr = wr_ref[...]
    w_hi = wr.astype(BF16)
    w_lo = (wr - w_hi.astype(F32)).astype(BF16)
    logits = (jnp.dot(h_hi, w_hi, preferred_element_type=F32)
              + jnp.dot(h_lo, w_hi, preferred_element_type=F32)
              + jnp.dot(h_hi, w_lo, preferred_element_type=F32))
    i1, i2, w1, w2 = _top2(logits)
    lane = lax.broadcasted_iota(jnp.int32, logits.shape, 1)
    sel1 = lane == i1
    sel2 = lane == i2
    gate = jnp.where(sel1, w1, 0.0) + jnp.where(sel2, w2, 0.0)
    gatet_ref[...] = gate.T[0:N_EXPERTS, :]
    picked = jnp.logical_or(sel1, sel2)
    onehot = jnp.where(picked, 1.0, 0.0)
    row = lax.broadcasted_iota(jnp.int32, (t, t), 0)
    col = lax.broadcasted_iota(jnp.int32, (t, t), 1)
    before = jnp.where(col < row, 1.0, 0.0).astype(BF16)
    rank = jnp.dot(before, onehot.astype(BF16), preferred_element_type=F32)
    pos = jnp.where(picked, rank, -1.0)
    post_ref[...] = pos.T[0:N_EXPERTS, :]
    cnt_ref[0] = jnp.broadcast_to(jnp.sum(onehot, axis=0, keepdims=True), cnt_ref.shape[1:])


def _router(out_a, out_b, w_out, x, mod, g, w_router, seq):
    n, d = x.shape
    ka, kb = out_a.shape[1], out_b.shape[1]
    t = MOE_TILE
    nt = n // t
    wr = jnp.zeros((d, LANES), F32).at[:, :N_EXPERTS].set(w_router)
    return pl.pallas_call(
        _router_body,
        grid=(nt,),
        in_specs=[pl.BlockSpec((t, ka), lambda i: (i, 0)),
                  pl.BlockSpec((t, kb), lambda i: (i, 0)),
                  pl.BlockSpec((ka + kb, d), lambda i: (0, 0)),
                  pl.BlockSpec((t, d), lambda i: (i, 0)),
                  pl.BlockSpec((1, N_MOD, d), lambda i: (i * t // seq, 0, 0)),
                  pl.BlockSpec((1, d), lambda i: (0, 0)),
                  pl.BlockSpec((d, LANES), lambda i: (0, 0))],
        out_specs=[pl.BlockSpec((t, d), lambda i: (i, 0)),
                   pl.BlockSpec((t, d), lambda i: (i, 0)),
                   pl.BlockSpec((N_EXPERTS, t), lambda i: (0, i)),
                   pl.BlockSpec((N_EXPERTS, t), lambda i: (0, i)),
                   pl.BlockSpec((1, 8, LANES), lambda i: (i, 0, 0))],
        out_shape=[jax.ShapeDtypeStruct((n, d), F32),
                   jax.ShapeDtypeStruct((n, d), BF16),
                   jax.ShapeDtypeStruct((N_EXPERTS, n), F32),
                   jax.ShapeDtypeStruct((N_EXPERTS, n), F32),
                   jax.ShapeDtypeStruct((nt, 8, LANES), F32)],
        compiler_params=_params("parallel"),
        name="router",
    )(out_a, out_b, w_out, x, mod, g, wr)


def _moe_body(cnt_ref, h_ref, post_ref, gatet_ref, x_ref, mod_ref, fg_ref,
              wg_ref, wu_ref, wd_ref, o_ref, xg, acc, *, final):
    ti = pl.program_id(0)
    e = pl.program_id(1)
    s = pl.program_id(2)
    ns = pl.num_programs(2)
    t = h_ref.shape[0]
    cnt = cnt_ref[ti, e]

    @pl.when(jnp.logical_and(e == 0, s == 0))
    def _zero():
        o_ref[...] = jnp.zeros(o_ref.shape, F32)

    def block(b, bm):
        if isinstance(b, int):
            rows = slice(b * bm, (b + 1) * bm)
        else:
            rows = pl.ds(pl.multiple_of(b * bm, 16), bm)

        @pl.when(s == 0)
        def _gather():
            tgt = (lax.broadcasted_iota(jnp.int32, (bm, t), 0) + b * bm).astype(F32)
            onehot = jnp.where(post_ref[pl.ds(e, 1), :] == tgt, 1.0, 0.0).astype(BF16)
            xg[rows, :] = jnp.dot(onehot, h_ref[...], preferred_element_type=F32).astype(BF16)

        xb = xg[rows, :]
        g = jnp.dot(xb, wg_ref[0, 0], preferred_element_type=F32)
        u = jnp.dot(xb, wu_ref[0, 0], preferred_element_type=F32)
        a = (g * jax.nn.sigmoid(g) * u).astype(BF16)
        y = jnp.dot(a, wd_ref[0], preferred_element_type=F32)

        @pl.when(s == 0)
        def _first():
            acc[rows, :] = y

        @pl.when(s != 0)
        def _rest():
            acc[rows, :] += y

        @pl.when(s == ns - 1)
        def _scatter():
            tgt = (lax.broadcasted_iota(jnp.int32, (bm, t), 0) + b * bm).astype(F32)
            hit = post_ref[pl.ds(e, 1), :] == tgt
            gate_rows = jnp.sum(jnp.where(hit, gatet_ref[pl.ds(e, 1), :], 0.0), axis=-1,
                                keepdims=True)
            yb = (acc[rows, :] * gate_rows).astype(BF16)
            o_ref[...] += lax.dot_general(jnp.where(hit, 1.0, 0.0).astype(BF16), yb,
                                          (((0,), (0,)), ((), ())), preferred_element_type=F32)

    lo = MOE_LOOP_BLOCK
    for bm in MOE_BLOCKS:
        @pl.when(jnp.logical_and(cnt > lo, cnt <= bm))
        def _single(bm=bm):
            block(0, bm)
        lo = bm

    @pl.when(jnp.logical_or(cnt <= MOE_LOOP_BLOCK, cnt > lo))
    def _multi():
        bm = MOE_LOOP_BLOCK

        def body(b, carry):
            block(b, bm)
            return carry
        lax.fori_loop(0, (cnt + (bm - 1)) // bm, body, 0)

    @pl.when(jnp.logical_and(e == pl.num_programs(1) - 1, s == ns - 1))
    def _epilogue():
        y = x_ref[...] + mod_ref[0, G2:G2 + 1, :] * o_ref[...]
        if final:
            y = y * lax.rsqrt(jnp.mean(y * y, axis=-1, keepdims=True) + EPS) * fg_ref[...]
        o_ref[...] = y


def _moe(out_a, out_b, w_out, x, mod, g, final_g, w_router, w_gate, w_up, w_down, seq, final):
    n, d = x.shape
    n_e, f, _ = w_down.shape
    t = MOE_TILE
    nt = n // t
    fs = f // MOE_FSPLIT
    x, h, post, gatet, cnt = _router(out_a, out_b, w_out, x, mod, g, w_router, seq)
    counts = cnt[:, 0, :N_EXPERTS].astype(jnp.int32)
    cap = max(pl.cdiv(t, MOE_LOOP_BLOCK) * MOE_LOOP_BLOCK, MOE_BLOCKS[-1])
    grid_spec = pltpu.PrefetchScalarGridSpec(
        num_scalar_prefetch=1,
        grid=(nt, n_e, MOE_FSPLIT),
        in_specs=[pl.BlockSpec((t, d), lambda i, e, s, c: (i, 0)),
                  pl.BlockSpec((N_EXPERTS, t), lambda i, e, s, c: (0, i)),
                  pl.BlockSpec((N_EXPERTS, t), lambda i, e, s, c: (0, i)),
                  pl.BlockSpec((t, d), lambda i, e, s, c: (i, 0)),
                  pl.BlockSpec((1, N_MOD, d), lambda i, e, s, c: (i * t // seq, 0, 0)),
                  pl.BlockSpec((1, d), lambda i, e, s, c: (0, 0)),
                  pl.BlockSpec((1, 1, d, fs), lambda i, e, s, c: (s, e, 0, 0)),
                  pl.BlockSpec((1, 1, d, fs), lambda i, e, s, c: (s, e, 0, 0)),
                  pl.BlockSpec((1, fs, d), lambda i, e, s, c: (e, s, 0))],
        out_specs=pl.BlockSpec((t, d), lambda i, e, s, c: (i, 0)),
        scratch_shapes=[pltpu.VMEM((cap, d), BF16), pltpu.VMEM((cap, d), F32)],
    )
    return pl.pallas_call(
        functools.partial(_moe_body, final=final),
        grid_spec=grid_spec,
        out_shape=jax.ShapeDtypeStruct((n, d), F32),
        compiler_params=_params("parallel", "arbitrary", "arbitrary"),
        name="moe_mixer",
    )(counts, h, post, gatet, x, mod, final_g, w_gate, w_up, w_down)


def kernel(x, c, w_ada, b_ada, norm1_g, norm2_g, w_in, w_out, gmlp_vnorm_g, gmlp_ws, gmlp_bs,
           lam_q1, lam_k1, lam_q2, lam_k2, diff_subln_g, ffn_w_gate, ffn_w_up, ffn_w_down,
           w_router, moe_w_gate, moe_w_up, moe_w_down, final_g):
    batch, seq, d = x.shape
    depth = w_ada.shape[0]
    n = batch * seq
    mods = _ada(c, w_ada, b_ada)
    xf = x.reshape(n, d)
    fg = final_g.reshape(1, d)
    for l in range(depth):
        mod = mods[l]
        z, vt, out_a = _inproj(xf, mod, norm1_g[l].reshape(1, d), w_in[l].astype(BF16),
                               gmlp_vnorm_g[l], gmlp_ws[l], gmlp_bs[l], seq)
        lam_init = 0.8 - 0.6 * math.exp(-0.3 * l)
        lam_p = jnp.stack([lam_q1[l], lam_k1[l], lam_q2[l], lam_k2[l]])
        i = l // 2
        if l % 2 == 0:
            casts = [(w[:, None], i, p) for w, p in
                     ((ffn_w_gate, FFN_FSPLIT), (ffn_w_up, FFN_FSPLIT), (ffn_w_down, 1))]
            if l + 1 < depth:
                casts += [(moe_w_gate, i, MOE_FSPLIT), (moe_w_up, i, MOE_FSPLIT)]
        else:
            casts = [(moe_w_down, i, 1)]
        out_b, cast_out = _attn(z, vt, lam_p, diff_subln_g[l], lam_init, batch, seq, casts)
        final = l == depth - 1
        g2 = norm2_g[l].reshape(1, d)
        if l % 2 == 0:
            xf = _ffn(out_a, out_b, w_out[l].astype(BF16), xf, mod, g2, fg,
                      cast_out[0][:, 0], cast_out[1][:, 0], cast_out[2][0, 0], seq, final)
            moe_gate_up = cast_out[3:]
        else:
            xf = _moe(out_a, out_b, w_out[l].astype(BF16), xf, mod, g2, fg, w_router[i],
                      moe_gate_up[0], moe_gate_up[1], cast_out[0][0], seq, final)
    return xf.reshape(batch, seq, d)
```

```python
import functools
import math

import jax
import jax.numpy as jnp
import numpy as np
from jax import lax
from jax.experimental import pallas as pl
from jax.experimental.pallas import tpu as pltpu

F32 = jnp.float32
BF16 = jnp.bfloat16
EPS = 1e-6
N_MOD = 6
N_GMLP_HEADS = 8
GMLP_HEAD_DIM = 64
CHUNK = 128
N_DIFF_HEADS = 4
DIFF_HEAD_DIM = 64
DIFF_V_DIM = 128
N_EXPERTS = 8
LANES = 128
VMEM_LIMIT = 56 * 1024 * 1024

MOE_TILE = 1024
MOE_BLOCKS = (256, 352, 448)
MOE_LOOP_BLOCK = 160
MOE_FSPLIT = 2

FFN_FSPLIT = 2
FFN_CHUNK = 256

ATTN_BLOCK = 1024
ATTN_ONES_ROWS = 16

SH1, SC1, G1, SH2, SC2, G2 = range(N_MOD)


def _params(*sem):
    return pltpu.CompilerParams(dimension_semantics=sem, vmem_limit_bytes=VMEM_LIMIT)


def _modnorm(x, g, sc, sh):
    ms = jnp.mean(x * x, axis=-1, keepdims=True)
    return (x * lax.rsqrt(ms + EPS)) * g * (1.0 + sc) + sh


def _ada_body(c_ref, w_ref, b_ref, o_ref):
    c = c_ref[...]
    sc = c * jax.nn.sigmoid(c)
    o_ref[0] = jnp.dot(sc, w_ref[0], preferred_element_type=F32,
                       precision=lax.Precision.HIGHEST) + b_ref[0]


def _ada(c, w_ada, b_ada):
    depth, d, m = w_ada.shape
    b = c.shape[0]
    rows = 8
    cp = jnp.zeros((rows, d), F32).at[:b].set(c)
    tn = 3072
    out = pl.pallas_call(
        _ada_body,
        grid=(depth, m // tn),
        in_specs=[pl.BlockSpec((rows, d), lambda l, j: (0, 0)),
                  pl.BlockSpec((1, d, tn), lambda l, j: (l, 0, j)),
                  pl.BlockSpec((1, 1, tn), lambda l, j: (l, 0, j))],
        out_specs=pl.BlockSpec((1, rows, tn), lambda l, j: (l, 0, j)),
        out_shape=jax.ShapeDtypeStruct((depth, rows, m), F32),
        compiler_params=_params("parallel", "parallel"),
        name="ada",
    )(cp, w_ada, b_ada.reshape(depth, 1, m))
    return out[:, :b].reshape(depth, b, N_MOD, d)


def _gmlp_chunk(u, v, vg, wp, bias, gm):
    lane = lax.broadcasted_iota(jnp.int32, (CHUNK, LANES), 1)
    left = lane < GMLP_HEAD_DIM
    gu = jax.nn.gelu(u)
    gv = jax.nn.gelu(v)
    v2 = gv * gv
    hi = v2.astype(BF16)
    lo = (v2 - hi.astype(F32)).astype(BF16)
    ms = (jnp.dot(hi, gm, preferred_element_type=F32)
          + jnp.dot(lo, gm, preferred_element_type=F32))
    vn = gv * lax.rsqrt(ms + EPS) * vg
    outs = []
    for p in range(len(wp)):
        vp = vn[:, p * LANES:(p + 1) * LANES]
        zero = jnp.zeros_like(vp)
        rhs = jnp.concatenate([jnp.where(left, vp, zero), jnp.where(left, zero, vp)],
                              axis=0).astype(BF16)
        outs.append(jnp.dot(wp[p], rhs, preferred_element_type=F32))
    return gu * (jnp.concatenate(outs, axis=1) + bias)


def _inproj_body(x_ref, mod_ref, g_ref, w_ref, wvt_ref, vg_ref, ws_ref, bias_ref, gm_ref,
                 z_ref, vt_ref, a_ref):
    h = _modnorm(x_ref[...], g_ref[...], mod_ref[0, SC1:SC1 + 1, :],
                 mod_ref[0, SH1:SH1 + 1, :]).astype(BF16)
    tm = x_ref.shape[0]
    gw = a_ref.shape[1]
    u = jnp.dot(h, w_ref[:, 0:gw], preferred_element_type=F32)
    v = jnp.dot(h, w_ref[:, gw:2 * gw], preferred_element_type=F32)
    cw = 512
    for c0 in range(0, z_ref.shape[1], cw):
        z_ref[:, c0:c0 + cw] = jnp.dot(h, w_ref[:, 2 * gw + c0:2 * gw + c0 + cw],
                                       preferred_element_type=F32).astype(z_ref.dtype)
    dve = DIFF_V_DIM + ATTN_ONES_ROWS
    vt = lax.dot_general(wvt_ref[...], h, (((1,), (1,)), ((), ())),
                         preferred_element_type=F32).astype(vt_ref.dtype)
    for hd in range(N_DIFF_HEADS):
        vt_ref[hd * dve:hd * dve + DIFF_V_DIM, :] = vt[hd * DIFF_V_DIM:(hd + 1) * DIFF_V_DIM, :]
        vt_ref[hd * dve + DIFF_V_DIM:(hd + 1) * dve, :] = jnp.ones((ATTN_ONES_ROWS, tm), vt_ref.dtype)
    row = lax.broadcasted_iota(jnp.int32, (CHUNK, 2 * CHUNK), 0)
    col = lax.broadcasted_iota(jnp.int32, (CHUNK, 2 * CHUNK), 1)
    causal = jnp.where(col >= CHUNK, col - CHUNK, col) <= row
    wp = [jnp.where(causal, ws_ref[p], jnp.zeros_like(ws_ref[p])) for p in range(ws_ref.shape[0])]
    for c in range(tm // CHUNK):
        rs = slice(c * CHUNK, (c + 1) * CHUNK)
        a_ref[rs, :] = _gmlp_chunk(u[rs, :], v[rs, :], vg_ref[...], wp, bias_ref[...],
                                   gm_ref[...]).astype(a_ref.dtype)


def _inproj(x, mod, g, w, vnorm_g, w_s, b_s, seq):
    n, d = x.shape
    vw = N_DIFF_HEADS * DIFF_V_DIM
    gw = N_GMLP_HEADS * GMLP_HEAD_DIM
    zw = w.shape[1] - vw - 2 * gw
    dve = DIFF_V_DIM + ATTN_ONES_ROWS
    npair = N_GMLP_HEADS // 2
    tm = 512
    w_pair = w_s.reshape(npair, 2, CHUNK, CHUNK).transpose(0, 2, 1, 3).reshape(
        npair, CHUNK, 2 * CHUNK).astype(BF16)
    bias = jnp.repeat(b_s.T, GMLP_HEAD_DIM, axis=1)
    grp = np.arange(gw) // GMLP_HEAD_DIM
    group_mean = jnp.asarray((grp[:, None] == grp[None, :]) / GMLP_HEAD_DIM, dtype=BF16)
    return pl.pallas_call(
        _inproj_body,
        grid=(n // tm,),
        in_specs=[pl.BlockSpec((tm, d), lambda i: (i, 0)),
                  pl.BlockSpec((1, N_MOD, d), lambda i: (i * tm // seq, 0, 0)),
                  pl.BlockSpec((1, d), lambda i: (0, 0)),
                  pl.BlockSpec((d, 2 * gw + zw), lambda i: (0, 0)),
                  pl.BlockSpec((vw, d), lambda i: (0, 0)),
                  pl.BlockSpec((1, gw), lambda i: (0, 0)),
                  pl.BlockSpec((npair, CHUNK, 2 * CHUNK), lambda i: (0, 0, 0)),
                  pl.BlockSpec((CHUNK, gw), lambda i: (0, 0)),
                  pl.BlockSpec((gw, gw), lambda i: (0, 0))],
        out_specs=[pl.BlockSpec((tm, zw), lambda i: (i, 0)),
                   pl.BlockSpec((N_DIFF_HEADS * dve, tm), lambda i: (0, i)),
                   pl.BlockSpec((tm, gw), lambda i: (i, 0))],
        out_shape=[jax.ShapeDtypeStruct((n, zw), BF16),
                   jax.ShapeDtypeStruct((N_DIFF_HEADS * dve, n), BF16),
                   jax.ShapeDtypeStruct((n, gw), BF16)],
        compiler_params=_params("parallel"),
        name="inproj_gmlp",
    )(x, mod, g, w, w[:, 2 * gw + zw:].T, vnorm_g.reshape(1, gw), w_pair, bias, group_mean)


def _attn_body(*refs, tq, lam_init, ncast):
    qi_ref, ki_ref, q_ref, k_ref, vt_ref, lam_ref, g_ref = refs[:7]
    cast_src = refs[7:7 + ncast]
    o_ref = refs[7 + ncast]
    cast_dst = refs[8 + ncast:8 + 2 * ncast]
    qs1, qs2, m1, m2, acc1, acc2 = refs[8 + 2 * ncast:]
    for src, dst in zip(cast_src, cast_dst):
        pw = dst.shape[2]
        for j in range(dst.shape[0]):
            dst[j] = src[:, j * pw:(j + 1) * pw].astype(dst.dtype)
    qi = qi_ref[pl.program_id(2)]
    ki = ki_ref[pl.program_id(2)]
    maps = ((qs1, m1, acc1), (qs2, m2, acc2))
    dv = DIFF_V_DIM

    @pl.when(ki == 0)
    def _init():
        qsc = (q_ref[...].astype(F32) * (DIFF_HEAD_DIM ** -0.5 * math.log2(math.e))).astype(BF16)
        lane = lax.broadcasted_iota(jnp.int32, qsc.shape, 1)
        zero = jnp.zeros_like(qsc)
        qs1[...] = jnp.where(lane < DIFF_HEAD_DIM, qsc, zero)
        qs2[...] = jnp.where(lane < DIFF_HEAD_DIM, zero, qsc)
        for _, m_s, acc in maps:
            m_s[...] = jnp.full(m_s.shape, -jnp.inf, F32)
            acc[...] = jnp.zeros(acc.shape, F32)

    def step(masked):
        k = k_ref[...]
        vt = vt_ref[...]
        sts = [lax.dot_general(k, qs[...], (((1,), (1,)), ((), ())), preferred_element_type=F32)
               for qs, _, _ in maps]
        for st, (_, m_s, acc) in zip(sts, maps):
            if masked:
                key = lax.broadcasted_iota(jnp.int32, st.shape, 0)
                qry = lax.broadcasted_iota(jnp.int32, st.shape, 1)
                st = jnp.where(key <= qry, st, -jnp.inf)
            sb = st.astype(BF16)
            m_prev = m_s[...]
            m_new = jnp.maximum(m_prev, jnp.max(sb, axis=0, keepdims=True).astype(F32))
            alpha = jnp.exp2(m_prev - m_new)
            pt = jnp.exp2(sb - m_new.astype(BF16))
            acc[...] = alpha * acc[...] + jnp.dot(vt, pt, preferred_element_type=F32)
            m_s[...] = m_new

    @pl.when(ki < qi)
    def _full():
        step(False)

    @pl.when(ki == qi)
    def _diag():
        step(True)
        lp = lam_ref[...]
        s1 = jnp.sum(lp[0:1, :] * lp[1:2, :], axis=-1, keepdims=True)
        s2 = jnp.sum(lp[2:3, :] * lp[3:4, :], axis=-1, keepdims=True)
        lam = jnp.exp(s1) - jnp.exp(s2) + lam_init
        ot = (acc1[0:dv, :] / acc1[dv:dv + 1, :]
              - lam * (acc2[0:dv, :] / acc2[dv:dv + 1, :]))
        ot = ot * lax.rsqrt(jnp.mean(ot * ot, axis=0, keepdims=True) + EPS)
        o_ref[...] = (ot.T * g_ref[...] * (1.0 - lam_init)).astype(o_ref.dtype)


def _cast_rows(rows, nsteps):
    for rb in range(16, rows + 1, 16):
        if rows % rb == 0 and rows // rb <= nsteps:
            return rb
    raise ValueError(f"no row block for {rows} rows in {nsteps} steps")


def _attn(z, vt, lam_p, subln_g, lam_init, batch, seq, casts=()):
    n = z.shape[0]
    tq = ATTN_BLOCK
    dve = DIFF_V_DIM + ATTN_ONES_ROWS
    nq = seq // tq
    qcol = 0
    kcol = qcol + N_DIFF_HEADS
    pairs = [(i, j) for i in range(nq) for j in range(i + 1)]
    npair = len(pairs)
    qi_tab = jnp.asarray([p[0] for p in pairs], jnp.int32)
    ki_tab = jnp.asarray([p[1] for p in pairs], jnp.int32)
    nsteps = batch * N_DIFF_HEADS * npair
    cast2d = [w.reshape(-1, w.shape[-1]) for w, _, _ in casts]
    cast_in, cast_out, cast_shape = [], [], []
    for w, i, pieces in casts:
        rows, cols = w.shape[1] * w.shape[2], w.shape[3]
        rb = _cast_rows(rows, nsteps)
        nb = rows // rb

        def blk(b, h, p, nb=nb):
            return jnp.minimum((b * N_DIFF_HEADS + h) * npair + p, nb - 1)
        cast_in.append(pl.BlockSpec(
            (rb, cols), lambda b, h, p, qt, kt, blk=blk, off=i * nb: (off + blk(b, h, p), 0)))
        cast_out.append(pl.BlockSpec(
            (pieces, rb, cols // pieces), lambda b, h, p, qt, kt, blk=blk: (0, blk(b, h, p), 0)))
        cast_shape.append((pieces, rows, cols // pieces))
    grid_spec = pltpu.PrefetchScalarGridSpec(
        num_scalar_prefetch=2,
        grid=(batch, N_DIFF_HEADS, npair),
        in_specs=[
            pl.BlockSpec((tq, LANES), lambda b, h, p, qt, kt: (b * nq + qt[p], qcol + h)),
            pl.BlockSpec((tq, LANES), lambda b, h, p, qt, kt: (b * nq + kt[p], kcol + h)),
            pl.BlockSpec((dve, tq), lambda b, h, p, qt, kt: (h, b * nq + kt[p])),
            pl.BlockSpec((4, DIFF_HEAD_DIM), lambda b, h, p, qt, kt: (0, 0)),
            pl.BlockSpec((1, DIFF_V_DIM), lambda b, h, p, qt, kt: (0, 0)),
        ] + cast_in,
        out_specs=[pl.BlockSpec((tq, LANES), lambda b, h, p, qt, kt: (b * nq + qt[p], h))]
        + cast_out,
        scratch_shapes=(2 * [pltpu.VMEM((tq, LANES), BF16)] + 2 * [pltpu.VMEM((1, tq), F32)]
                        + 2 * [pltpu.VMEM((dve, tq), F32)]),
    )
    outs = pl.pallas_call(
        functools.partial(_attn_body, tq=tq, lam_init=lam_init, ncast=len(casts)),
        grid_spec=grid_spec,
        out_shape=[jax.ShapeDtypeStruct((n, N_DIFF_HEADS * DIFF_V_DIM), BF16)]
        + [jax.ShapeDtypeStruct(s, BF16) for s in cast_shape],
        compiler_params=_params("arbitrary", "arbitrary", "arbitrary"),
        name="diff_attn",
    )(qi_tab, ki_tab, z, z, vt, lam_p, subln_g.reshape(1, DIFF_V_DIM), *cast2d)
    return outs[0], [o.reshape(pieces, w.shape[1], w.shape[2], w.shape[3] // pieces)
                     for o, (w, _, pieces) in zip(outs[1:], casts)]


def _outproj_body(a_ref, b_ref, w_ref, x_ref, mod_ref, o_ref):
    ka = a_ref.shape[1]
    y = jnp.dot(a_ref[...], w_ref[0:ka, :], preferred_element_type=F32)
    y = y + jnp.dot(b_ref[...], w_ref[ka:, :], preferred_element_type=F32)
    o_ref[...] = x_ref[...] + mod_ref[0, G1:G1 + 1, :] * y


def _outproj(out_a, out_b, w, x, mod, seq):
    n, d = x.shape
    tm = 512
    ka, kb = out_a.shape[1], out_b.shape[1]
    return pl.pallas_call(
        _outproj_body,
        grid=(n // tm,),
        in_specs=[pl.BlockSpec((tm, ka), lambda i: (i, 0)),
                  pl.BlockSpec((tm, kb), lambda i: (i, 0)),
                  pl.BlockSpec((ka + kb, d), lambda i: (0, 0)),
                  pl.BlockSpec((tm, d), lambda i: (i, 0)),
                  pl.BlockSpec((1, N_MOD, d), lambda i: (i * tm // seq, 0, 0))],
        out_specs=pl.BlockSpec((tm, d), lambda i: (i, 0)),
        out_shape=jax.ShapeDtypeStruct((n, d), F32),
        compiler_params=_params("parallel"),
        name="outproj",
    )(out_a, out_b, w, x, mod)


def _ffn_body(a_ref, b_ref, wo_ref, x_ref, mod_ref, g_ref, fg_ref, wg_ref, wu_ref, wd_ref, o_ref,
              h_s, acc, *, final):
    j = pl.program_id(1)

    @pl.when(j == 0)
    def _prologue():
        ka = a_ref.shape[1]
        y = jnp.dot(a_ref[...], wo_ref[0:ka, :], preferred_element_type=F32)
        y = y + jnp.dot(b_ref[...], wo_ref[ka:, :], preferred_element_type=F32)
        x = x_ref[...] + mod_ref[0, G1:G1 + 1, :] * y
        o_ref[...] = x
        h = _modnorm(x, g_ref[...], mod_ref[0, SC2:SC2 + 1, :], mod_ref[0, SH2:SH2 + 1, :])
        h_s[...] = h.astype(BF16)
        acc[...] = jnp.zeros(acc.shape, F32)

    h = h_s[...]
    fs = wd_ref.shape[0]
    for c0 in range(0, fs, FFN_CHUNK):
        cs = slice(c0, min(c0 + FFN_CHUNK, fs))
        g = jnp.dot(h, wg_ref[0, :, cs], preferred_element_type=F32)
        u = jnp.dot(h, wu_ref[0, :, cs], preferred_element_type=F32)
        a = (g * jax.nn.sigmoid(g) * u).astype(BF16)
        acc[...] += jnp.dot(a, wd_ref[cs, :], preferred_element_type=F32)

    @pl.when(j == pl.num_programs(1) - 1)
    def _epilogue():
        y = o_ref[...] + mod_ref[0, G2:G2 + 1, :] * acc[...]
        if final:
            y = y * lax.rsqrt(jnp.mean(y * y, axis=-1, keepdims=True) + EPS) * fg_ref[...]
        o_ref[...] = y


def _ffn(out_a, out_b, w_out, x, mod, g, final_g, w_gate, w_up, w_down, seq, final):
    n, d = x.shape
    nsplit, _, fs = w_gate.shape
    ka, kb = out_a.shape[1], out_b.shape[1]
    tm = 1024
    return pl.pallas_call(
        functools.partial(_ffn_body, final=final),
        grid=(n // tm, nsplit),
        in_specs=[pl.BlockSpec((tm, ka), lambda i, j: (i, 0)),
                  pl.BlockSpec((tm, kb), lambda i, j: (i, 0)),
                  pl.BlockSpec((ka + kb, d), lambda i, j: (0, 0)),
                  pl.BlockSpec((tm, d), lambda i, j: (i, 0)),
                  pl.BlockSpec((1, N_MOD, d), lambda i, j: (i * tm // seq, 0, 0)),
                  pl.BlockSpec((1, d), lambda i, j: (0, 0)),
                  pl.BlockSpec((1, d), lambda i, j: (0, 0)),
                  pl.BlockSpec((1, d, fs), lambda i, j: (j, 0, 0)),
                  pl.BlockSpec((1, d, fs), lambda i, j: (j, 0, 0)),
                  pl.BlockSpec((fs, d), lambda i, j: (j, 0))],
        out_specs=pl.BlockSpec((tm, d), lambda i, j: (i, 0)),
        out_shape=jax.ShapeDtypeStruct((n, d), F32),
        scratch_shapes=[pltpu.VMEM((tm, d), BF16), pltpu.VMEM((tm, d), F32)],
        compiler_params=_params("parallel", "arbitrary"),
        name="ffn_mixer",
    )(out_a, out_b, w_out, x, mod, g, final_g, w_gate, w_up, w_down)


def _top2(logits):
    lane = lax.broadcasted_iota(jnp.int32, logits.shape, 1)
    lg = jnp.where(lane < N_EXPERTS, logits, -jnp.inf)
    m1 = jnp.max(lg, axis=-1, keepdims=True)
    i1 = jnp.min(jnp.where(lg == m1, lane, LANES), axis=-1, keepdims=True)
    lg2 = jnp.where(lane == i1, -jnp.inf, lg)
    m2 = jnp.max(lg2, axis=-1, keepdims=True)
    i2 = jnp.min(jnp.where(lg2 == m2, lane, LANES), axis=-1, keepdims=True)
    e2 = jnp.exp(m2 - m1)
    den = 1.0 + e2
    return i1, i2, 1.0 / den, e2 / den


def _router_body(x_ref, mod_ref, g_ref, wr_ref, before_ref, h_ref, post_ref, gatet_ref, cnt_ref):
    t = x_ref.shape[0]
    h = _modnorm(x_ref[...], g_ref[...], mod_ref[0, SC2:SC2 + 1, :], mod_ref[0, SH2:SH2 + 1, :])
    h_hi = h.astype(BF16)
    h_ref[...] = h_hi
    h_lo = (h - h_hi.astype(F32)).astype(BF16)
    wr = wr_ref[...]
    w_hi = wr.astype(BF16)
    w_lo = (wr - w_hi.astype(F32)).astype(BF16)
    logits = (jnp.dot(h_hi, w_hi, preferred_element_type=F32)
              + jnp.dot(h_lo, w_hi, preferred_element_type=F32)
              + jnp.dot(h_hi, w_lo, preferred_element_type=F32))
    i1, i2, w1, w2 = _top2(logits)
    lane = lax.broadcasted_iota(jnp.int32, logits.shape, 1)
    sel1 = lane == i1
    sel2 = lane == i2
    gate = jnp.where(sel1, w1, 0.0) + jnp.where(sel2, w2, 0.0)
    gatet_ref[...] = gate.T[0:N_EXPERTS, :]
    picked = jnp.logical_or(sel1, sel2)
    onehot = jnp.where(picked, 1.0, 0.0)
    rank = jnp.dot(before_ref[...], onehot.astype(BF16), preferred_element_type=F32)
    pos = jnp.where(picked, rank, -1.0)
    post_ref[...] = pos.T[0:N_EXPERTS, :]
    cnt_ref[0] = jnp.broadcast_to(jnp.sum(onehot, axis=0, keepdims=True), cnt_ref.shape[1:])


def _router(x, mod, g, w_router, seq):
    n, d = x.shape
    t = MOE_TILE
    nt = n // t
    wr = jnp.zeros((d, LANES), F32).at[:, :N_EXPERTS].set(w_router)
    before = jnp.asarray(np.tril(np.ones((t, t), np.float32), -1), dtype=BF16)
    return pl.pallas_call(
        _router_body,
        grid=(nt,),
        in_specs=[pl.BlockSpec((t, d), lambda i: (i, 0)),
                  pl.BlockSpec((1, N_MOD, d), lambda i: (i * t // seq, 0, 0)),
                  pl.BlockSpec((1, d), lambda i: (0, 0)),
                  pl.BlockSpec((d, LANES), lambda i: (0, 0)),
                  pl.BlockSpec((t, t), lambda i: (0, 0))],
        out_specs=[pl.BlockSpec((t, d), lambda i: (i, 0)),
                   pl.BlockSpec((N_EXPERTS, t), lambda i: (0, i)),
                   pl.BlockSpec((N_EXPERTS, t), lambda i: (0, i)),
                   pl.BlockSpec((1, 8, LANES), lambda i: (i, 0, 0))],
        out_shape=[jax.ShapeDtypeStruct((n, d), BF16),
                   jax.ShapeDtypeStruct((N_EXPERTS, n), F32),
                   jax.ShapeDtypeStruct((N_EXPERTS, n), F32),
                   jax.ShapeDtypeStruct((nt, 8, LANES), F32)],
        compiler_params=_params("parallel"),
        name="router",
    )(x, mod, g, wr, before)


def _moe_body(cnt_ref, h_ref, post_ref, gatet_ref, x_ref, mod_ref, fg_ref,
              wg_ref, wu_ref, wd_ref, o_ref, xg, acc, *, final):
    ti = pl.program_id(0)
    e = pl.program_id(1)
    s = pl.program_id(2)
    ns = pl.num_programs(2)
    t = h_ref.shape[0]
    cnt = cnt_ref[ti, e]

    @pl.when(jnp.logical_and(e == 0, s == 0))
    def _zero():
        o_ref[...] = jnp.zeros(o_ref.shape, F32)

    def block(b, bm):
        if isinstance(b, int):
            rows = slice(b * bm, (b + 1) * bm)
        else:
            rows = pl.ds(pl.multiple_of(b * bm, 16), bm)

        @pl.when(s == 0)
        def _gather():
            tgt = (lax.broadcasted_iota(jnp.int32, (bm, t), 0) + b * bm).astype(F32)
            onehot = jnp.where(post_ref[pl.ds(e, 1), :] == tgt, 1.0, 0.0).astype(BF16)
            xg[rows, :] = jnp.dot(onehot, h_ref[...], preferred_element_type=F32).astype(BF16)

        xb = xg[rows, :]
        g = jnp.dot(xb, wg_ref[0, 0], preferred_element_type=F32)
        u = jnp.dot(xb, wu_ref[0, 0], preferred_element_type=F32)
        a = (g * jax.nn.sigmoid(g) * u).astype(BF16)
        y = jnp.dot(a, wd_ref[0], preferred_element_type=F32)

        @pl.when(s == 0)
        def _first():
            acc[rows, :] = y

        @pl.when(s != 0)
        def _rest():
            acc[rows, :] += y

        @pl.when(s == ns - 1)
        def _scatter():
            tgt = (lax.broadcasted_iota(jnp.int32, (bm, t), 0) + b * bm).astype(F32)
            hit = post_ref[pl.ds(e, 1), :] == tgt
            gate_rows = jnp.sum(jnp.where(hit, gatet_ref[pl.ds(e, 1), :], 0.0), axis=-1,
                                keepdims=True)
            yb = (acc[rows, :] * gate_rows).astype(BF16)
            o_ref[...] += lax.dot_general(jnp.where(hit, 1.0, 0.0).astype(BF16), yb,
                                          (((0,), (0,)), ((), ())), preferred_element_type=F32)

    lo = MOE_LOOP_BLOCK
    for bm in MOE_BLOCKS:
        @pl.when(jnp.logical_and(cnt > lo, cnt <= bm))
        def _single(bm=bm):
            block(0, bm)
        lo = bm

    @pl.when(jnp.logical_or(cnt <= MOE_LOOP_BLOCK, cnt > lo))
    def _multi():
        bm = MOE_LOOP_BLOCK

        def body(b, carry):
            block(b, bm)
            return carry
        lax.fori_loop(0, (cnt + (bm - 1)) // bm, body, 0)

    @pl.when(jnp.logical_and(e == pl.num_programs(1) - 1, s == ns - 1))
    def _epilogue():
        y = x_ref[...] + mod_ref[0, G2:G2 + 1, :] * o_ref[...]
        if final:
            y = y * lax.rsqrt(jnp.mean(y * y, axis=-1, keepdims=True) + EPS) * fg_ref[...]
        o_ref[...] = y


def _moe(x, mod, g, final_g, w_router, w_gate, w_up, w_down, seq, final):
    n, d = x.shape
    n_e, f, _ = w_down.shape
    t = MOE_TILE
    nt = n // t
    fs = f // MOE_FSPLIT
    h, post, gatet, cnt = _router(x, mod, g, w_router, seq)
    counts = cnt[:, 0, :N_EXPERTS].astype(jnp.int32)
    cap = max(pl.cdiv(t, MOE_LOOP_BLOCK) * MOE_LOOP_BLOCK, MOE_BLOCKS[-1])
    grid_spec = pltpu.PrefetchScalarGridSpec(
        num_scalar_prefetch=1,
        grid=(nt, n_e, MOE_FSPLIT),
        in_specs=[pl.BlockSpec((t, d), lambda i, e, s, c: (i, 0)),
                  pl.BlockSpec((N_EXPERTS, t), lambda i, e, s, c: (0, i)),
                  pl.BlockSpec((N_EXPERTS, t), lambda i, e, s, c: (0, i)),
                  pl.BlockSpec((t, d), lambda i, e, s, c: (i, 0)),
                  pl.BlockSpec((1, N_MOD, d), lambda i, e, s, c: (i * t // seq, 0, 0)),
                  pl.BlockSpec((1, d), lambda i, e, s, c: (0, 0)),
                  pl.BlockSpec((1, 1, d, fs), lambda i, e, s, c: (s, e, 0, 0)),
                  pl.BlockSpec((1, 1, d, fs), lambda i, e, s, c: (s, e, 0, 0)),
                  pl.BlockSpec((1, fs, d), lambda i, e, s, c: (e, s, 0))],
        out_specs=pl.BlockSpec((t, d), lambda i, e, s, c: (i, 0)),
        scratch_shapes=[pltpu.VMEM((cap, d), BF16), pltpu.VMEM((cap, d), F32)],
    )
    return pl.pallas_call(
        functools.partial(_moe_body, final=final),
        grid_spec=grid_spec,
        out_shape=jax.ShapeDtypeStruct((n, d), F32),
        compiler_params=_params("parallel", "arbitrary", "arbitrary"),
        name="moe_mixer",
    )(counts, h, post, gatet, x, mod, final_g, w_gate, w_up, w_down)


def kernel(x, c, w_ada, b_ada, norm1_g, norm2_g, w_in, w_out, gmlp_vnorm_g, gmlp_ws, gmlp_bs,
           lam_q1, lam_k1, lam_q2, lam_k2, diff_subln_g, ffn_w_gate, ffn_w_up, ffn_w_down,
           w_router, moe_w_gate, moe_w_up, moe_w_down, final_g):
    batch, seq, d = x.shape
    depth = w_ada.shape[0]
    n = batch * seq
    mods = _ada(c, w_ada, b_ada)
    xf = x.reshape(n, d)
    fg = final_g.reshape(1, d)
    for l in range(depth):
        mod = mods[l]
        z, vt, out_a = _inproj(xf, mod, norm1_g[l].reshape(1, d), w_in[l].astype(BF16),
                               gmlp_vnorm_g[l], gmlp_ws[l], gmlp_bs[l], seq)
        lam_init = 0.8 - 0.6 * math.exp(-0.3 * l)
        lam_p = jnp.stack([lam_q1[l], lam_k1[l], lam_q2[l], lam_k2[l]])
        i = l // 2
        dense_casts = lambda j: [(w[:, None], j, p) for w, p in
                                 ((ffn_w_gate, FFN_FSPLIT), (ffn_w_up, FFN_FSPLIT), (ffn_w_down, 1))]
        if l % 2 == 0:
            casts = dense_casts(i) if l == 0 else []
            if l + 1 < depth:
                casts += [(moe_w_gate, i, MOE_FSPLIT), (moe_w_up, i, MOE_FSPLIT)]
        else:
            casts = [(moe_w_down, i, 1)] + (dense_casts(i + 1) if l + 1 < depth else [])
        out_b, cast_out = _attn(z, vt, lam_p, diff_subln_g[l], lam_init, batch, seq, casts)
        final = l == depth - 1
        g2 = norm2_g[l].reshape(1, d)
        if l % 2 == 0:
            if l == 0:
                dense_w, cast_out = cast_out[:3], cast_out[3:]
            xf = _ffn(out_a, out_b, w_out[l].astype(BF16), xf, mod, g2, fg,
                      dense_w[0][:, 0], dense_w[1][:, 0], dense_w[2][0, 0], seq, final)
            moe_gate_up = cast_out
        else:
            dense_w = cast_out[1:]
            xf = _outproj(out_a, out_b, w_out[l].astype(BF16), xf, mod, seq)
            xf = _moe(xf, mod, g2, fg, w_router[i], moe_gate_up[0], moe_gate_up[1], cast_out[0][0],
                      seq, final)
    return xf.reshape(batch, seq, d)
```

```python
import functools
import math

import jax
import jax.numpy as jnp
import numpy as np
from jax import lax
from jax.experimental import pallas as pl
from jax.experimental.pallas import tpu as pltpu

F32 = jnp.float32
BF16 = jnp.bfloat16
EPS = 1e-6
N_MOD = 6
N_GMLP_HEADS = 8
GMLP_HEAD_DIM = 64
CHUNK = 128
N_DIFF_HEADS = 4
DIFF_HEAD_DIM = 64
DIFF_V_DIM = 128
N_EXPERTS = 8
LANES = 128
VMEM_LIMIT = 56 * 1024 * 1024

MOE_TILE = 1024
MOE_BLOCKS = (192, 256, 352, 448)
MOE_LOOP_BLOCK = 128
MOE_FSPLIT = 2

FFN_FSPLIT = 2
FFN_CHUNK = 256

ATTN_BLOCK = 1024
ATTN_ONES_ROWS = 16

SH1, SC1, G1, SH2, SC2, G2 = range(N_MOD)


def _params(*sem):
    return pltpu.CompilerParams(dimension_semantics=sem, vmem_limit_bytes=VMEM_LIMIT)


def _modnorm(x, g, sc, sh):
    ms = jnp.mean(x * x, axis=-1, keepdims=True)
    return (x * lax.rsqrt(ms + EPS)) * g * (1.0 + sc) + sh


def _ada_body(c_ref, w_ref, b_ref, o_ref):
    c = c_ref[...]
    sc = c * jax.nn.sigmoid(c)
    o_ref[0] = jnp.dot(sc, w_ref[0], preferred_element_type=F32,
                       precision=lax.Precision.HIGHEST) + b_ref[0]


def _ada(c, w_ada, b_ada):
    depth, d, m = w_ada.shape
    b = c.shape[0]
    rows = 8
    cp = jnp.zeros((rows, d), F32).at[:b].set(c)
    tn = 3072
    out = pl.pallas_call(
        _ada_body,
        grid=(depth, m // tn),
        in_specs=[pl.BlockSpec((rows, d), lambda l, j: (0, 0)),
                  pl.BlockSpec((1, d, tn), lambda l, j: (l, 0, j)),
                  pl.BlockSpec((1, 1, tn), lambda l, j: (l, 0, j))],
        out_specs=pl.BlockSpec((1, rows, tn), lambda l, j: (l, 0, j)),
        out_shape=jax.ShapeDtypeStruct((depth, rows, m), F32),
        compiler_params=_params("parallel", "parallel"),
        name="ada",
    )(cp, w_ada, b_ada.reshape(depth, 1, m))
    return out[:, :b].reshape(depth, b, N_MOD, d)


def _gmlp_chunk(u, v, vg, wp, bias, gm):
    lane = lax.broadcasted_iota(jnp.int32, (CHUNK, LANES), 1)
    left = lane < GMLP_HEAD_DIM
    gu = jax.nn.gelu(u)
    gv = jax.nn.gelu(v)
    v2 = gv * gv
    hi = v2.astype(BF16)
    lo = (v2 - hi.astype(F32)).astype(BF16)
    ms = (jnp.dot(hi, gm, preferred_element_type=F32)
          + jnp.dot(lo, gm, preferred_element_type=F32))
    vn = gv * lax.rsqrt(ms + EPS) * vg
    outs = []
    for p in range(len(wp)):
        vp = vn[:, p * LANES:(p + 1) * LANES]
        zero = jnp.zeros_like(vp)
        rhs = jnp.concatenate([jnp.where(left, vp, zero), jnp.where(left, zero, vp)],
                              axis=0).astype(BF16)
        outs.append(jnp.dot(wp[p], rhs, preferred_element_type=F32))
    return gu * (jnp.concatenate(outs, axis=1) + bias)


def _inproj_body(x_ref, mod_ref, g_ref, w_ref, wvt_ref, vg_ref, ws_ref, bias_ref, gm_ref,
                 z_ref, vt_ref, a_ref):
    h = _modnorm(x_ref[...], g_ref[...], mod_ref[0, SC1:SC1 + 1, :],
                 mod_ref[0, SH1:SH1 + 1, :]).astype(BF16)
    tm = x_ref.shape[0]
    gw = a_ref.shape[1]
    u = jnp.dot(h, w_ref[:, 0:gw], preferred_element_type=F32)
    v = jnp.dot(h, w_ref[:, gw:2 * gw], preferred_element_type=F32)
    cw = 512
    for c0 in range(0, z_ref.shape[1], cw):
        z_ref[:, c0:c0 + cw] = jnp.dot(h, w_ref[:, 2 * gw + c0:2 * gw + c0 + cw],
                                       preferred_element_type=F32).astype(z_ref.dtype)
    dve = DIFF_V_DIM + ATTN_ONES_ROWS
    vt = lax.dot_general(wvt_ref[...], h, (((1,), (1,)), ((), ())),
                         preferred_element_type=F32).astype(vt_ref.dtype)
    for hd in range(N_DIFF_HEADS):
        vt_ref[hd * dve:hd * dve + DIFF_V_DIM, :] = vt[hd * DIFF_V_DIM:(hd + 1) * DIFF_V_DIM, :]
        vt_ref[hd * dve + DIFF_V_DIM:(hd + 1) * dve, :] = jnp.ones((ATTN_ONES_ROWS, tm), vt_ref.dtype)
    row = lax.broadcasted_iota(jnp.int32, (CHUNK, 2 * CHUNK), 0)
    col = lax.broadcasted_iota(jnp.int32, (CHUNK, 2 * CHUNK), 1)
    causal = jnp.where(col >= CHUNK, col - CHUNK, col) <= row
    wp = [jnp.where(causal, ws_ref[p], jnp.zeros_like(ws_ref[p])) for p in range(ws_ref.shape[0])]
    for c in range(tm // CHUNK):
        rs = slice(c * CHUNK, (c + 1) * CHUNK)
        a_ref[rs, :] = _gmlp_chunk(u[rs, :], v[rs, :], vg_ref[...], wp, bias_ref[...],
                                   gm_ref[...]).astype(a_ref.dtype)


def _inproj(x, mod, g, w, vnorm_g, w_s, b_s, seq):
    n, d = x.shape
    vw = N_DIFF_HEADS * DIFF_V_DIM
    gw = N_GMLP_HEADS * GMLP_HEAD_DIM
    zw = w.shape[1] - vw - 2 * gw
    dve = DIFF_V_DIM + ATTN_ONES_ROWS
    npair = N_GMLP_HEADS // 2
    tm = 512
    w_pair = w_s.reshape(npair, 2, CHUNK, CHUNK).transpose(0, 2, 1, 3).reshape(
        npair, CHUNK, 2 * CHUNK).astype(BF16)
    bias = jnp.repeat(b_s.T, GMLP_HEAD_DIM, axis=1)
    grp = np.arange(gw) // GMLP_HEAD_DIM
    group_mean = jnp.asarray((grp[:, None] == grp[None, :]) / GMLP_HEAD_DIM, dtype=BF16)
    return pl.pallas_call(
        _inproj_body,
        grid=(n // tm,),
        in_specs=[pl.BlockSpec((tm, d), lambda i: (i, 0)),
                  pl.BlockSpec((1, N_MOD, d), lambda i: (i * tm // seq, 0, 0)),
                  pl.BlockSpec((1, d), lambda i: (0, 0)),
                  pl.BlockSpec((d, 2 * gw + zw), lambda i: (0, 0)),
                  pl.BlockSpec((vw, d), lambda i: (0, 0)),
                  pl.BlockSpec((1, gw), lambda i: (0, 0)),
                  pl.BlockSpec((npair, CHUNK, 2 * CHUNK), lambda i: (0, 0, 0)),
                  pl.BlockSpec((CHUNK, gw), lambda i: (0, 0)),
                  pl.BlockSpec((gw, gw), lambda i: (0, 0))],
        out_specs=[pl.BlockSpec((tm, zw), lambda i: (i, 0)),
                   pl.BlockSpec((N_DIFF_HEADS * dve, tm), lambda i: (0, i)),
                   pl.BlockSpec((tm, gw), lambda i: (i, 0))],
        out_shape=[jax.ShapeDtypeStruct((n, zw), BF16),
                   jax.ShapeDtypeStruct((N_DIFF_HEADS * dve, n), BF16),
                   jax.ShapeDtypeStruct((n, gw), BF16)],
        compiler_params=_params("parallel"),
        name="inproj_gmlp",
    )(x, mod, g, w, w[:, 2 * gw + zw:].T, vnorm_g.reshape(1, gw), w_pair, bias, group_mean)


def _attn_body(*refs, tq, lam_init, ncast):
    qi_ref, ki_ref, q_ref, k_ref, vt_ref, lam_ref, g_ref = refs[:7]
    cast_src = refs[7:7 + ncast]
    o_ref = refs[7 + ncast]
    cast_dst = refs[8 + ncast:8 + 2 * ncast]
    qs1, qs2, m1, m2, acc1, acc2 = refs[8 + 2 * ncast:]
    for src, dst in zip(cast_src, cast_dst):
        pw = dst.shape[2]
        for j in range(dst.shape[0]):
            dst[j] = src[:, j * pw:(j + 1) * pw].astype(dst.dtype)
    qi = qi_ref[pl.program_id(2)]
    ki = ki_ref[pl.program_id(2)]
    maps = ((qs1, m1, acc1), (qs2, m2, acc2))
    dv = DIFF_V_DIM

    @pl.when(ki == 0)
    def _init():
        qsc = (q_ref[...].astype(F32) * (DIFF_HEAD_DIM ** -0.5 * math.log2(math.e))).astype(BF16)
        lane = lax.broadcasted_iota(jnp.int32, qsc.shape, 1)
        zero = jnp.zeros_like(qsc)
        qs1[...] = jnp.where(lane < DIFF_HEAD_DIM, qsc, zero)
        qs2[...] = jnp.where(lane < DIFF_HEAD_DIM, zero, qsc)
        for _, m_s, acc in maps:
            m_s[...] = jnp.full(m_s.shape, -jnp.inf, F32)
            acc[...] = jnp.zeros(acc.shape, F32)

    def step(masked):
        k = k_ref[...]
        vt = vt_ref[...]
        sts = [lax.dot_general(k, qs[...], (((1,), (1,)), ((), ())), preferred_element_type=F32)
               for qs, _, _ in maps]
        for st, (_, m_s, acc) in zip(sts, maps):
            if masked:
                key = lax.broadcasted_iota(jnp.int32, st.shape, 0)
                qry = lax.broadcasted_iota(jnp.int32, st.shape, 1)
                st = jnp.where(key <= qry, st, -jnp.inf)
            sb = st.astype(BF16)
            m_prev = m_s[...]
            m_new = jnp.maximum(m_prev, jnp.max(sb, axis=0, keepdims=True).astype(F32))
            alpha = jnp.exp2(m_prev - m_new)
            pt = jnp.exp2(sb - m_new.astype(BF16))
            acc[...] = alpha * acc[...] + jnp.dot(vt, pt, preferred_element_type=F32)
            m_s[...] = m_new

    @pl.when(ki < qi)
    def _full():
        step(False)

    @pl.when(ki == qi)
    def _diag():
        step(True)
        lp = lam_ref[...]
        s1 = jnp.sum(lp[0:1, :] * lp[1:2, :], axis=-1, keepdims=True)
        s2 = jnp.sum(lp[2:3, :] * lp[3:4, :], axis=-1, keepdims=True)
        lam = jnp.exp(s1) - jnp.exp(s2) + lam_init
        ot = (acc1[0:dv, :] / acc1[dv:dv + 1, :]
              - lam * (acc2[0:dv, :] / acc2[dv:dv + 1, :]))
        ot = ot * lax.rsqrt(jnp.mean(ot * ot, axis=0, keepdims=True) + EPS)
        o_ref[...] = (ot.T * g_ref[...] * (1.0 - lam_init)).astype(o_ref.dtype)


def _cast_rows(rows, nsteps):
    for rb in range(16, rows + 1, 16):
        if rows % rb == 0 and rows // rb <= nsteps:
            return rb
    raise ValueError(f"no row block for {rows} rows in {nsteps} steps")


def _attn(z, vt, lam_p, subln_g, lam_init, batch, seq, casts=()):
    n = z.shape[0]
    tq = ATTN_BLOCK
    dve = DIFF_V_DIM + ATTN_ONES_ROWS
    nq = seq // tq
    qcol = 0
    kcol = qcol + N_DIFF_HEADS
    pairs = [(i, j) for i in range(nq) for j in range(i + 1)]
    npair = len(pairs)
    qi_tab = jnp.asarray([p[0] for p in pairs], jnp.int32)
    ki_tab = jnp.asarray([p[1] for p in pairs], jnp.int32)
    nsteps = batch * N_DIFF_HEADS * npair
    cast2d = [w.reshape(-1, w.shape[-1]) for w, _, _ in casts]
    cast_in, cast_out, cast_shape = [], [], []
    for w, i, pieces in casts:
        rows, cols = w.shape[1] * w.shape[2], w.shape[3]
        rb = _cast_rows(rows, nsteps)
        nb = rows // rb

        def blk(b, h, p, nb=nb):
            return jnp.minimum((b * N_DIFF_HEADS + h) * npair + p, nb - 1)
        cast_in.append(pl.BlockSpec(
            (rb, cols), lambda b, h, p, qt, kt, blk=blk, off=i * nb: (off + blk(b, h, p), 0)))
        cast_out.append(pl.BlockSpec(
            (pieces, rb, cols // pieces), lambda b, h, p, qt, kt, blk=blk: (0, blk(b, h, p), 0)))
        cast_shape.append((pieces, rows, cols // pieces))
    grid_spec = pltpu.PrefetchScalarGridSpec(
        num_scalar_prefetch=2,
        grid=(batch, N_DIFF_HEADS, npair),
        in_specs=[
            pl.BlockSpec((tq, LANES), lambda b, h, p, qt, kt: (b * nq + qt[p], qcol + h)),
            pl.BlockSpec((tq, LANES), lambda b, h, p, qt, kt: (b * nq + kt[p], kcol + h)),
            pl.BlockSpec((dve, tq), lambda b, h, p, qt, kt: (h, b * nq + kt[p])),
            pl.BlockSpec((4, DIFF_HEAD_DIM), lambda b, h, p, qt, kt: (0, 0)),
            pl.BlockSpec((1, DIFF_V_DIM), lambda b, h, p, qt, kt: (0, 0)),
        ] + cast_in,
        out_specs=[pl.BlockSpec((tq, LANES), lambda b, h, p, qt, kt: (b * nq + qt[p], h))]
        + cast_out,
        scratch_shapes=(2 * [pltpu.VMEM((tq, LANES), BF16)] + 2 * [pltpu.VMEM((1, tq), F32)]
                        + 2 * [pltpu.VMEM((dve, tq), F32)]),
    )
    outs = pl.pallas_call(
        functools.partial(_attn_body, tq=tq, lam_init=lam_init, ncast=len(casts)),
        grid_spec=grid_spec,
        out_shape=[jax.ShapeDtypeStruct((n, N_DIFF_HEADS * DIFF_V_DIM), BF16)]
        + [jax.ShapeDtypeStruct(s, BF16) for s in cast_shape],
        compiler_params=_params("arbitrary", "arbitrary", "arbitrary"),
        name="diff_attn",
    )(qi_tab, ki_tab, z, z, vt, lam_p, subln_g.reshape(1, DIFF_V_DIM), *cast2d)
    return outs[0], [o.reshape(pieces, w.shape[1], w.shape[2], w.shape[3] // pieces)
                     for o, (w, _, pieces) in zip(outs[1:], casts)]


def _outproj_body(a_ref, b_ref, w_ref, x_ref, mod_ref, o_ref):
    ka = a_ref.shape[1]
    y = jnp.dot(a_ref[...], w_ref[0:ka, :], preferred_element_type=F32)
    y = y + jnp.dot(b_ref[...], w_ref[ka:, :], preferred_element_type=F32)
    o_ref[...] = x_ref[...] + mod_ref[0, G1:G1 + 1, :] * y


def _outproj(out_a, out_b, w, x, mod, seq):
    n, d = x.shape
    tm = 512
    ka, kb = out_a.shape[1], out_b.shape[1]
    return pl.pallas_call(
        _outproj_body,
        grid=(n // tm,),
        in_specs=[pl.BlockSpec((tm, ka), lambda i: (i, 0)),
                  pl.BlockSpec((tm, kb), lambda i: (i, 0)),
                  pl.BlockSpec((ka + kb, d), lambda i: (0, 0)),
                  pl.BlockSpec((tm, d), lambda i: (i, 0)),
                  pl.BlockSpec((1, N_MOD, d), lambda i: (i * tm // seq, 0, 0))],
        out_specs=pl.BlockSpec((tm, d), lambda i: (i, 0)),
        out_shape=jax.ShapeDtypeStruct((n, d), F32),
        compiler_params=_params("parallel"),
        name="outproj",
    )(out_a, out_b, w, x, mod)


def _ffn_body(a_ref, b_ref, wo_ref, x_ref, mod_ref, g_ref, fg_ref, wg_ref, wu_ref, wd_ref, o_ref,
              h_s, acc, *, final):
    j = pl.program_id(1)

    @pl.when(j == 0)
    def _prologue():
        ka = a_ref.shape[1]
        y = jnp.dot(a_ref[...], wo_ref[0:ka, :], preferred_element_type=F32)
        y = y + jnp.dot(b_ref[...], wo_ref[ka:, :], preferred_element_type=F32)
        x = x_ref[...] + mod_ref[0, G1:G1 + 1, :] * y
        o_ref[...] = x
        h = _modnorm(x, g_ref[...], mod_ref[0, SC2:SC2 + 1, :], mod_ref[0, SH2:SH2 + 1, :])
        h_s[...] = h.astype(BF16)
        acc[...] = jnp.zeros(acc.shape, F32)

    h = h_s[...]
    fs = wd_ref.shape[0]
    for c0 in range(0, fs, FFN_CHUNK):
        cs = slice(c0, min(c0 + FFN_CHUNK, fs))
        g = jnp.dot(h, wg_ref[0, :, cs], preferred_element_type=F32)
        u = jnp.dot(h, wu_ref[0, :, cs], preferred_element_type=F32)
        a = (g * jax.nn.sigmoid(g) * u).astype(BF16)
        acc[...] += jnp.dot(a, wd_ref[cs, :], preferred_element_type=F32)

    @pl.when(j == pl.num_programs(1) - 1)
    def _epilogue():
        y = o_ref[...] + mod_ref[0, G2:G2 + 1, :] * acc[...]
        if final:
            y = y * lax.rsqrt(jnp.mean(y * y, axis=-1, keepdims=True) + EPS) * fg_ref[...]
        o_ref[...] = y


def _ffn(out_a, out_b, w_out, x, mod, g, final_g, w_gate, w_up, w_down, seq, final):
    n, d = x.shape
    nsplit, _, fs = w_gate.shape
    ka, kb = out_a.shape[1], out_b.shape[1]
    tm = 1024
    return pl.pallas_call(
        functools.partial(_ffn_body, final=final),
        grid=(n // tm, nsplit),
        in_specs=[pl.BlockSpec((tm, ka), lambda i, j: (i, 0)),
                  pl.BlockSpec((tm, kb), lambda i, j: (i, 0)),
                  pl.BlockSpec((ka + kb, d), lambda i, j: (0, 0)),
                  pl.BlockSpec((tm, d), lambda i, j: (i, 0)),
                  pl.BlockSpec((1, N_MOD, d), lambda i, j: (i * tm // seq, 0, 0)),
                  pl.BlockSpec((1, d), lambda i, j: (0, 0)),
                  pl.BlockSpec((1, d), lambda i, j: (0, 0)),
                  pl.BlockSpec((1, d, fs), lambda i, j: (j, 0, 0)),
                  pl.BlockSpec((1, d, fs), lambda i, j: (j, 0, 0)),
                  pl.BlockSpec((fs, d), lambda i, j: (j, 0))],
        out_specs=pl.BlockSpec((tm, d), lambda i, j: (i, 0)),
        out_shape=jax.ShapeDtypeStruct((n, d), F32),
        scratch_shapes=[pltpu.VMEM((tm, d), BF16), pltpu.VMEM((tm, d), F32)],
        compiler_params=_params("parallel", "arbitrary"),
        name="ffn_mixer",
    )(out_a, out_b, w_out, x, mod, g, final_g, w_gate, w_up, w_down)


def _top2(logits):
    lane = lax.broadcasted_iota(jnp.int32, logits.shape, 1)
    lg = jnp.where(lane < N_EXPERTS, logits, -jnp.inf)
    m1 = jnp.max(lg, axis=-1, keepdims=True)
    i1 = jnp.min(jnp.where(lg == m1, lane, LANES), axis=-1, keepdims=True)
    lg2 = jnp.where(lane == i1, -jnp.inf, lg)
    m2 = jnp.max(lg2, axis=-1, keepdims=True)
    i2 = jnp.min(jnp.where(lg2 == m2, lane, LANES), axis=-1, keepdims=True)
    e2 = jnp.exp(m2 - m1)
    den = 1.0 + e2
    return i1, i2, 1.0 / den, e2 / den


def _router_body(x_ref, mod_ref, g_ref, wr_ref, before_ref, h_ref, post_ref, gatet_ref, cnt_ref):
    t = x_ref.shape[0]
    h = _modnorm(x_ref[...], g_ref[...], mod_ref[0, SC2:SC2 + 1, :], mod_ref[0, SH2:SH2 + 1, :])
    h_hi = h.astype(BF16)
    h_ref[...] = h_hi
    h_lo = (h - h_hi.astype(F32)).astype(BF16)
    wr = wr_ref[...]
    w_hi = wr.astype(BF16)
    w_lo = (wr - w_hi.astype(F32)).astype(BF16)
    logits = (jnp.dot(h_hi, w_hi, preferred_element_type=F32)
              + jnp.dot(h_lo, w_hi, preferred_element_type=F32)
              + jnp.dot(h_hi, w_lo, preferred_element_type=F32))
    i1, i2, w1, w2 = _top2(logits)
    lane = lax.broadcasted_iota(jnp.int32, logits.shape, 1)
    sel1 = lane == i1
    sel2 = lane == i2
    gate = jnp.where(sel1, w1, 0.0) + jnp.where(sel2, w2, 0.0)
    gatet_ref[...] = gate.T[0:N_EXPERTS, :]
    picked = jnp.logical_or(sel1, sel2)
    onehot = jnp.where(picked, 1.0, 0.0)
    rank = jnp.dot(before_ref[...], onehot.astype(BF16), preferred_element_type=F32)
    pos = jnp.where(picked, rank, -1.0)
    post_ref[...] = pos.T[0:N_EXPERTS, :]
    cnt_ref[0] = jnp.broadcast_to(jnp.sum(onehot, axis=0, keepdims=True), cnt_ref.shape[1:])


def _router(x, mod, g, w_router, seq):
    n, d = x.shape
    t = MOE_TILE
    nt = n // t
    wr = jnp.zeros((d, LANES), F32).at[:, :N_EXPERTS].set(w_router)
    before = jnp.asarray(np.tril(np.ones((t, t), np.float32), -1), dtype=BF16)
    return pl.pallas_call(
        _router_body,
        grid=(nt,),
        in_specs=[pl.BlockSpec((t, d), lambda i: (i, 0)),
                  pl.BlockSpec((1, N_MOD, d), lambda i: (i * t // seq, 0, 0)),
                  pl.BlockSpec((1, d), lambda i: (0, 0)),
                  pl.BlockSpec((d, LANES), lambda i: (0, 0)),
                  pl.BlockSpec((t, t), lambda i: (0, 0))],
        out_specs=[pl.BlockSpec((t, d), lambda i: (i, 0)),
                   pl.BlockSpec((N_EXPERTS, t), lambda i: (0, i)),
                   pl.BlockSpec((N_EXPERTS, t), lambda i: (0, i)),
                   pl.BlockSpec((1, 8, LANES), lambda i: (i, 0, 0))],
        out_shape=[jax.ShapeDtypeStruct((n, d), BF16),
                   jax.ShapeDtypeStruct((N_EXPERTS, n), F32),
                   jax.ShapeDtypeStruct((N_EXPERTS, n), F32),
                   jax.ShapeDtypeStruct((nt, 8, LANES), F32)],
        compiler_params=_params("parallel"),
        name="router",
    )(x, mod, g, wr, before)


def _moe_body(cnt_ref, h_ref, post_ref, gatet_ref, x_ref, mod_ref, fg_ref,
              wg_ref, wu_ref, wd_ref, o_ref, xg, acc, *, final):
    ti = pl.program_id(0)
    e = pl.program_id(1)
    s = pl.program_id(2)
    ns = pl.num_programs(2)
    t = h_ref.shape[0]
    cnt = cnt_ref[ti, e]

    @pl.when(jnp.logical_and(e == 0, s == 0))
    def _zero():
        o_ref[...] = jnp.zeros(o_ref.shape, F32)

    def block(b, bm):
        if isinstance(b, int):
            rows = slice(b * bm, (b + 1) * bm)
        else:
            rows = pl.ds(pl.multiple_of(b * bm, 16), bm)

        @pl.when(s == 0)
        def _gather():
            tgt = (lax.broadcasted_iota(jnp.int32, (bm, t), 0) + b * bm).astype(F32)
            onehot = jnp.where(post_ref[pl.ds(e, 1), :] == tgt, 1.0, 0.0).astype(BF16)
            xg[rows, :] = jnp.dot(onehot, h_ref[...], preferred_element_type=F32).astype(BF16)

        xb = xg[rows, :]
        g = jnp.dot(xb, wg_ref[0, 0], preferred_element_type=F32)
        u = jnp.dot(xb, wu_ref[0, 0], preferred_element_type=F32)
        a = (g * jax.nn.sigmoid(g) * u).astype(BF16)
        y = jnp.dot(a, wd_ref[0], preferred_element_type=F32)

        @pl.when(s == 0)
        def _first():
            acc[rows, :] = y

        @pl.when(s != 0)
        def _rest():
            acc[rows, :] += y

        @pl.when(s == ns - 1)
        def _scatter():
            tgt = (lax.broadcasted_iota(jnp.int32, (bm, t), 0) + b * bm).astype(F32)
            hit = post_ref[pl.ds(e, 1), :] == tgt
            gate_rows = jnp.sum(jnp.where(hit, gatet_ref[pl.ds(e, 1), :], 0.0), axis=-1,
                                keepdims=True)
            yb = (acc[rows, :] * gate_rows).astype(BF16)
            o_ref[...] += lax.dot_general(jnp.where(hit, 1.0, 0.0).astype(BF16), yb,
                                          (((0,), (0,)), ((), ())), preferred_element_type=F32)

    lo = MOE_LOOP_BLOCK
    for bm in MOE_BLOCKS:
        @pl.when(jnp.logical_and(cnt > lo, cnt <= bm))
        def _single(bm=bm):
            block(0, bm)
        lo = bm

    @pl.when(jnp.logical_or(cnt <= MOE_LOOP_BLOCK, cnt > lo))
    def _multi():
        bm = MOE_LOOP_BLOCK

        def body(b, carry):
            block(b, bm)
            return carry
        lax.fori_loop(0, (cnt + (bm - 1)) // bm, body, 0)

    @pl.when(jnp.logical_and(e == pl.num_programs(1) - 1, s == ns - 1))
    def _epilogue():
        y = x_ref[...] + mod_ref[0, G2:G2 + 1, :] * o_ref[...]
        if final:
            y = y * lax.rsqrt(jnp.mean(y * y, axis=-1, keepdims=True) + EPS) * fg_ref[...]
        o_ref[...] = y


def _moe(x, mod, g, final_g, w_router, w_gate, w_up, w_down, seq, final):
    n, d = x.shape
    n_e, f, _ = w_down.shape
    t = MOE_TILE
    nt = n // t
    fs = f // MOE_FSPLIT
    h, post, gatet, cnt = _router(x, mod, g, w_router, seq)
    counts = cnt[:, 0, :N_EXPERTS].astype(jnp.int32)
    cap = max(pl.cdiv(t, MOE_LOOP_BLOCK) * MOE_LOOP_BLOCK, MOE_BLOCKS[-1])
    grid_spec = pltpu.PrefetchScalarGridSpec(
        num_scalar_prefetch=1,
        grid=(nt, n_e, MOE_FSPLIT),
        in_specs=[pl.BlockSpec((t, d), lambda i, e, s, c: (i, 0)),
                  pl.BlockSpec((N_EXPERTS, t), lambda i, e, s, c: (0, i)),
                  pl.BlockSpec((N_EXPERTS, t), lambda i, e, s, c: (0, i)),
                  pl.BlockSpec((t, d), lambda i, e, s, c: (i, 0)),
                  pl.BlockSpec((1, N_MOD, d), lambda i, e, s, c: (i * t // seq, 0, 0)),
                  pl.BlockSpec((1, d), lambda i, e, s, c: (0, 0)),
                  pl.BlockSpec((1, 1, d, fs), lambda i, e, s, c: (s, e, 0, 0)),
                  pl.BlockSpec((1, 1, d, fs), lambda i, e, s, c: (s, e, 0, 0)),
                  pl.BlockSpec((1, fs, d), lambda i, e, s, c: (e, s, 0))],
        out_specs=pl.BlockSpec((t, d), lambda i, e, s, c: (i, 0)),
        scratch_shapes=[pltpu.VMEM((cap, d), BF16), pltpu.VMEM((cap, d), F32)],
    )
    return pl.pallas_call(
        functools.partial(_moe_body, final=final),
        grid_spec=grid_spec,
        out_shape=jax.ShapeDtypeStruct((n, d), F32),
        compiler_params=_params("parallel", "arbitrary", "arbitrary"),
        name="moe_mixer",
    )(counts, h, post, gatet, x, mod, final_g, w_gate, w_up, w_down)


def kernel(x, c, w_ada, b_ada, norm1_g, norm2_g, w_in, w_out, gmlp_vnorm_g, gmlp_ws, gmlp_bs,
           lam_q1, lam_k1, lam_q2, lam_k2, diff_subln_g, ffn_w_gate, ffn_w_up, ffn_w_down,
           w_router, moe_w_gate, moe_w_up, moe_w_down, final_g):
    batch, seq, d = x.shape
    depth = w_ada.shape[0]
    n = batch * seq
    mods = _ada(c, w_ada, b_ada)
    xf = x.reshape(n, d)
    fg = final_g.reshape(1, d)
    for l in range(depth):
        mod = mods[l]
        z, vt, out_a = _inproj(xf, mod, norm1_g[l].reshape(1, d), w_in[l].astype(BF16),
                               gmlp_vnorm_g[l], gmlp_ws[l], gmlp_bs[l], seq)
        lam_init = 0.8 - 0.6 * math.exp(-0.3 * l)
        lam_p = jnp.stack([lam_q1[l], lam_k1[l], lam_q2[l], lam_k2[l]])
        i = l // 2
        dense_casts = lambda j: [(w[:, None], j, p) for w, p in
                                 ((ffn_w_gate, FFN_FSPLIT), (ffn_w_up, FFN_FSPLIT), (ffn_w_down, 1))]
        if l % 2 == 0:
            casts = dense_casts(i) if l == 0 else []
            if l + 1 < depth:
                casts += [(moe_w_gate, i, MOE_FSPLIT), (moe_w_up, i, MOE_FSPLIT)]
        else:
            casts = [(moe_w_down, i, 1)] + (dense_casts(i + 1) if l + 1 < depth else [])
        out_b, cast_out = _attn(z, vt, lam_p, diff_subln_g[l], lam_init, batch, seq, casts)
        final = l == depth - 1
        g2 = norm2_g[l].reshape(1, d)
        if l % 2 == 0:
            if l == 0:
                dense_w, cast_out = cast_out[:3], cast_out[3:]
            xf = _ffn(out_a, out_b, w_out[l].astype(BF16), xf, mod, g2, fg,
                      dense_w[0][:, 0], dense_w[1][:, 0], dense_w[2][0, 0], seq, final)
            moe_gate_up = cast_out
        else:
            dense_w = cast_out[1:]
            xf = _outproj(out_a, out_b, w_out[l].astype(BF16), xf, mod, seq)
            xf = _moe(xf, mod, g2, fg, w_router[i], moe_gate_up[0], moe_gate_up[1], cast_out[0][0],
                      seq, final)
    return xf.reshape(batch, seq, d)
```

```python
import functools
import math

import jax
import jax.numpy as jnp
import numpy as np
from jax import lax
from jax.experimental import pallas as pl
from jax.experimental.pallas import tpu as pltpu

F32 = jnp.float32
BF16 = jnp.bfloat16
EPS = 1e-6
N_MOD = 6
N_GMLP_HEADS = 8
GMLP_HEAD_DIM = 64
CHUNK = 128
N_DIFF_HEADS = 4
DIFF_HEAD_DIM = 64
DIFF_V_DIM = 128
N_EXPERTS = 8
LANES = 128
VMEM_LIMIT = 56 * 1024 * 1024

MOE_TILE = 1024
MOE_BLOCKS = (256, 352, 448)
MOE_LOOP_BLOCK = 160
MOE_FSPLIT = 2

FFN_FSPLIT = 2
FFN_CHUNK = 256

ATTN_BLOCK = 1024
ATTN_ONES_ROWS = 16

SH1, SC1, G1, SH2, SC2, G2 = range(N_MOD)


def _params(*sem):
    return pltpu.CompilerParams(dimension_semantics=sem, vmem_limit_bytes=VMEM_LIMIT)


def _modnorm(x, g, sc, sh):
    ms = jnp.mean(x * x, axis=-1, keepdims=True)
    return (x * lax.rsqrt(ms + EPS)) * g * (1.0 + sc) + sh


def _ada_body(c_ref, w_ref, b_ref, o_ref):
    c = c_ref[...]
    sc = c * jax.nn.sigmoid(c)
    o_ref[0] = jnp.dot(sc, w_ref[0], preferred_element_type=F32,
                       precision=lax.Precision.HIGHEST) + b_ref[0]


def _ada(c, w_ada, b_ada):
    depth, d, m = w_ada.shape
    b = c.shape[0]
    rows = 8
    cp = jnp.zeros((rows, d), F32).at[:b].set(c)
    tn = 3072
    out = pl.pallas_call(
        _ada_body,
        grid=(depth, m // tn),
        in_specs=[pl.BlockSpec((rows, d), lambda l, j: (0, 0)),
                  pl.BlockSpec((1, d, tn), lambda l, j: (l, 0, j)),
                  pl.BlockSpec((1, 1, tn), lambda l, j: (l, 0, j))],
        out_specs=pl.BlockSpec((1, rows, tn), lambda l, j: (l, 0, j)),
        out_shape=jax.ShapeDtypeStruct((depth, rows, m), F32),
        compiler_params=_params("parallel", "parallel"),
        name="ada",
    )(cp, w_ada, b_ada.reshape(depth, 1, m))
    return out[:, :b].reshape(depth, b, N_MOD, d)


def _gmlp_chunk(u, v, vg, wp, bias, gm):
    lane = lax.broadcasted_iota(jnp.int32, (CHUNK, LANES), 1)
    left = lane < GMLP_HEAD_DIM
    gu = jax.nn.gelu(u)
    gv = jax.nn.gelu(v)
    v2 = gv * gv
    hi = v2.astype(BF16)
    lo = (v2 - hi.astype(F32)).astype(BF16)
    ms = (jnp.dot(hi, gm, preferred_element_type=F32)
          + jnp.dot(lo, gm, preferred_element_type=F32))
    vn = gv * lax.rsqrt(ms + EPS) * vg
    outs = []
    for p in range(len(wp)):
        vp = vn[:, p * LANES:(p + 1) * LANES]
        zero = jnp.zeros_like(vp)
        rhs = jnp.concatenate([jnp.where(left, vp, zero), jnp.where(left, zero, vp)],
                              axis=0).astype(BF16)
        outs.append(jnp.dot(wp[p], rhs, preferred_element_type=F32))
    return gu * (jnp.concatenate(outs, axis=1) + bias)


def _inproj_body(x_ref, mod_ref, g_ref, w_ref, wvt_ref, vg_ref, ws_ref, bias_ref, gm_ref,
                 z_ref, vt_ref, a_ref):
    h = _modnorm(x_ref[...], g_ref[...], mod_ref[0, SC1:SC1 + 1, :],
                 mod_ref[0, SH1:SH1 + 1, :]).astype(BF16)
    tm = x_ref.shape[0]
    gw = a_ref.shape[1]
    u = jnp.dot(h, w_ref[:, 0:gw], preferred_element_type=F32)
    v = jnp.dot(h, w_ref[:, gw:2 * gw], preferred_element_type=F32)
    cw = 512
    for c0 in range(0, z_ref.shape[1], cw):
        z_ref[:, c0:c0 + cw] = jnp.dot(h, w_ref[:, 2 * gw + c0:2 * gw + c0 + cw],
                                       preferred_element_type=F32).astype(z_ref.dtype)
    dve = DIFF_V_DIM + ATTN_ONES_ROWS
    vt = lax.dot_general(wvt_ref[...], h, (((1,), (1,)), ((), ())),
                         preferred_element_type=F32).astype(vt_ref.dtype)
    for hd in range(N_DIFF_HEADS):
        vt_ref[hd * dve:hd * dve + DIFF_V_DIM, :] = vt[hd * DIFF_V_DIM:(hd + 1) * DIFF_V_DIM, :]
        vt_ref[hd * dve + DIFF_V_DIM:(hd + 1) * dve, :] = jnp.ones((ATTN_ONES_ROWS, tm), vt_ref.dtype)
    row = lax.broadcasted_iota(jnp.int32, (CHUNK, 2 * CHUNK), 0)
    col = lax.broadcasted_iota(jnp.int32, (CHUNK, 2 * CHUNK), 1)
    causal = jnp.where(col >= CHUNK, col - CHUNK, col) <= row
    wp = [jnp.where(causal, ws_ref[p], jnp.zeros_like(ws_ref[p])) for p in range(ws_ref.shape[0])]
    for c in range(tm // CHUNK):
        rs = slice(c * CHUNK, (c + 1) * CHUNK)
        a_ref[rs, :] = _gmlp_chunk(u[rs, :], v[rs, :], vg_ref[...], wp, bias_ref[...],
                                   gm_ref[...]).astype(a_ref.dtype)


def _inproj(x, mod, g, w, vnorm_g, w_s, b_s, seq):
    n, d = x.shape
    vw = N_DIFF_HEADS * DIFF_V_DIM
    gw = N_GMLP_HEADS * GMLP_HEAD_DIM
    zw = w.shape[1] - vw - 2 * gw
    dve = DIFF_V_DIM + ATTN_ONES_ROWS
    npair = N_GMLP_HEADS // 2
    tm = 512
    w_pair = w_s.reshape(npair, 2, CHUNK, CHUNK).transpose(0, 2, 1, 3).reshape(
        npair, CHUNK, 2 * CHUNK).astype(BF16)
    bias = jnp.repeat(b_s.T, GMLP_HEAD_DIM, axis=1)
    grp = np.arange(gw) // GMLP_HEAD_DIM
    group_mean = jnp.asarray((grp[:, None] == grp[None, :]) / GMLP_HEAD_DIM, dtype=BF16)
    return pl.pallas_call(
        _inproj_body,
        grid=(n // tm,),
        in_specs=[pl.BlockSpec((tm, d), lambda i: (i, 0)),
                  pl.BlockSpec((1, N_MOD, d), lambda i: (i * tm // seq, 0, 0)),
                  pl.BlockSpec((1, d), lambda i: (0, 0)),
                  pl.BlockSpec((d, 2 * gw + zw), lambda i: (0, 0)),
                  pl.BlockSpec((vw, d), lambda i: (0, 0)),
                  pl.BlockSpec((1, gw), lambda i: (0, 0)),
                  pl.BlockSpec((npair, CHUNK, 2 * CHUNK), lambda i: (0, 0, 0)),
                  pl.BlockSpec((CHUNK, gw), lambda i: (0, 0)),
                  pl.BlockSpec((gw, gw), lambda i: (0, 0))],
        out_specs=[pl.BlockSpec((tm, zw), lambda i: (i, 0)),
                   pl.BlockSpec((N_DIFF_HEADS * dve, tm), lambda i: (0, i)),
                   pl.BlockSpec((tm, gw), lambda i: (i, 0))],
        out_shape=[jax.ShapeDtypeStruct((n, zw), BF16),
                   jax.ShapeDtypeStruct((N_DIFF_HEADS * dve, n), BF16),
                   jax.ShapeDtypeStruct((n, gw), BF16)],
        compiler_params=_params("parallel"),
        name="inproj_gmlp",
    )(x, mod, g, w, w[:, 2 * gw + zw:].T, vnorm_g.reshape(1, gw), w_pair, bias, group_mean)


def _attn_body(*refs, tq, lam_init, ncast):
    qi_ref, ki_ref, q_ref, k_ref, vt_ref, lam_ref, g_ref = refs[:7]
    cast_src = refs[7:7 + ncast]
    o_ref = refs[7 + ncast]
    cast_dst = refs[8 + ncast:8 + 2 * ncast]
    qs1, qs2, m1, m2, acc1, acc2 = refs[8 + 2 * ncast:]
    for src, dst in zip(cast_src, cast_dst):
        pw = dst.shape[2]
        for j in range(dst.shape[0]):
            dst[j] = src[:, j * pw:(j + 1) * pw].astype(dst.dtype)
    qi = qi_ref[pl.program_id(2)]
    ki = ki_ref[pl.program_id(2)]
    maps = ((qs1, m1, acc1), (qs2, m2, acc2))
    dv = DIFF_V_DIM

    @pl.when(ki == 0)
    def _init():
        qsc = (q_ref[...].astype(F32) * (DIFF_HEAD_DIM ** -0.5 * math.log2(math.e))).astype(BF16)
        lane = lax.broadcasted_iota(jnp.int32, qsc.shape, 1)
        zero = jnp.zeros_like(qsc)
        qs1[...] = jnp.where(lane < DIFF_HEAD_DIM, qsc, zero)
        qs2[...] = jnp.where(lane < DIFF_HEAD_DIM, zero, qsc)
        for _, m_s, acc in maps:
            m_s[...] = jnp.full(m_s.shape, -jnp.inf, F32)
            acc[...] = jnp.zeros(acc.shape, F32)

    def step(masked):
        k = k_ref[...]
        vt = vt_ref[...]
        sts = [lax.dot_general(k, qs[...], (((1,), (1,)), ((), ())), preferred_element_type=F32)
               for qs, _, _ in maps]
        for st, (_, m_s, acc) in zip(sts, maps):
            if masked:
                key = lax.broadcasted_iota(jnp.int32, st.shape, 0)
                qry = lax.broadcasted_iota(jnp.int32, st.shape, 1)
                st = jnp.where(key <= qry, st, -jnp.inf)
            sb = st.astype(BF16)
            m_prev = m_s[...]
            m_new = jnp.maximum(m_prev, jnp.max(sb, axis=0, keepdims=True).astype(F32))
            alpha = jnp.exp2(m_prev - m_new)
            pt = jnp.exp2(sb - m_new.astype(BF16))
            acc[...] = alpha * acc[...] + jnp.dot(vt, pt, preferred_element_type=F32)
            m_s[...] = m_new

    @pl.when(ki < qi)
    def _full():
        step(False)

    @pl.when(ki == qi)
    def _diag():
        step(True)
        lp = lam_ref[...]
        s1 = jnp.sum(lp[0:1, :] * lp[1:2, :], axis=-1, keepdims=True)
        s2 = jnp.sum(lp[2:3, :] * lp[3:4, :], axis=-1, keepdims=True)
        lam = jnp.exp(s1) - jnp.exp(s2) + lam_init
        ot = (acc1[0:dv, :] / acc1[dv:dv + 1, :]
              - lam * (acc2[0:dv, :] / acc2[dv:dv + 1, :]))
        ot = ot * lax.rsqrt(jnp.mean(ot * ot, axis=0, keepdims=True) + EPS)
        o_ref[...] = (ot.T * g_ref[...] * (1.0 - lam_init)).astype(o_ref.dtype)


def _cast_rows(rows, nsteps):
    for rb in range(16, rows + 1, 16):
        if rows % rb == 0 and rows // rb <= nsteps:
            return rb
    raise ValueError(f"no row block for {rows} rows in {nsteps} steps")


def _attn(z, vt, lam_p, subln_g, lam_init, batch, seq, casts=()):
    n = z.shape[0]
    tq = ATTN_BLOCK
    dve = DIFF_V_DIM + ATTN_ONES_ROWS
    nq = seq // tq
    qcol = 0
    kcol = qcol + N_DIFF_HEADS
    pairs = [(i, j) for i in range(nq) for j in range(i + 1)]
    npair = len(pairs)
    qi_tab = jnp.asarray([p[0] for p in pairs], jnp.int32)
    ki_tab = jnp.asarray([p[1] for p in pairs], jnp.int32)
    nsteps = batch * N_DIFF_HEADS * npair
    cast2d = [w.reshape(-1, w.shape[-1]) for w, _, _ in casts]
    cast_in, cast_out, cast_shape = [], [], []
    for w, i, pieces in casts:
        rows, cols = w.shape[1] * w.shape[2], w.shape[3]
        rb = _cast_rows(rows, nsteps)
        nb = rows // rb

        def blk(b, h, p, nb=nb):
            return jnp.minimum((b * N_DIFF_HEADS + h) * npair + p, nb - 1)
        cast_in.append(pl.BlockSpec(
            (rb, cols), lambda b, h, p, qt, kt, blk=blk, off=i * nb: (off + blk(b, h, p), 0)))
        cast_out.append(pl.BlockSpec(
            (pieces, rb, cols // pieces), lambda b, h, p, qt, kt, blk=blk: (0, blk(b, h, p), 0)))
        cast_shape.append((pieces, rows, cols // pieces))
    grid_spec = pltpu.PrefetchScalarGridSpec(
        num_scalar_prefetch=2,
        grid=(batch, N_DIFF_HEADS, npair),
        in_specs=[
            pl.BlockSpec((tq, LANES), lambda b, h, p, qt, kt: (b * nq + qt[p], qcol + h)),
            pl.BlockSpec((tq, LANES), lambda b, h, p, qt, kt: (b * nq + kt[p], kcol + h)),
            pl.BlockSpec((dve, tq), lambda b, h, p, qt, kt: (h, b * nq + kt[p])),
            pl.BlockSpec((4, DIFF_HEAD_DIM), lambda b, h, p, qt, kt: (0, 0)),
            pl.BlockSpec((1, DIFF_V_DIM), lambda b, h, p, qt, kt: (0, 0)),
        ] + cast_in,
        out_specs=[pl.BlockSpec((tq, LANES), lambda b, h, p, qt, kt: (b * nq + qt[p], h))]
        + cast_out,
        scratch_shapes=(2 * [pltpu.VMEM((tq, LANES), BF16)] + 2 * [pltpu.VMEM((1, tq), F32)]
                        + 2 * [pltpu.VMEM((dve, tq), F32)]),
    )
    outs = pl.pallas_call(
        functools.partial(_attn_body, tq=tq, lam_init=lam_init, ncast=len(casts)),
        grid_spec=grid_spec,
        out_shape=[jax.ShapeDtypeStruct((n, N_DIFF_HEADS * DIFF_V_DIM), BF16)]
        + [jax.ShapeDtypeStruct(s, BF16) for s in cast_shape],
        compiler_params=_params("arbitrary", "arbitrary", "arbitrary"),
        name="diff_attn",
    )(qi_tab, ki_tab, z, z, vt, lam_p, subln_g.reshape(1, DIFF_V_DIM), *cast2d)
    return outs[0], [o.reshape(pieces, w.shape[1], w.shape[2], w.shape[3] // pieces)
                     for o, (w, _, pieces) in zip(outs[1:], casts)]


def _outproj_body(a_ref, b_ref, w_ref, x_ref, mod_ref, o_ref):
    ka = a_ref.shape[1]
    y = jnp.dot(a_ref[...], w_ref[0:ka, :], preferred_element_type=F32)
    y = y + jnp.dot(b_ref[...], w_ref[ka:, :], preferred_element_type=F32)
    o_ref[...] = x_ref[...] + mod_ref[0, G1:G1 + 1, :] * y


def _outproj(out_a, out_b, w, x, mod, seq):
    n, d = x.shape
    tm = 512
    ka, kb = out_a.shape[1], out_b.shape[1]
    return pl.pallas_call(
        _outproj_body,
        grid=(n // tm,),
        in_specs=[pl.BlockSpec((tm, ka), lambda i: (i, 0)),
                  pl.BlockSpec((tm, kb), lambda i: (i, 0)),
                  pl.BlockSpec((ka + kb, d), lambda i: (0, 0)),
                  pl.BlockSpec((tm, d), lambda i: (i, 0)),
                  pl.BlockSpec((1, N_MOD, d), lambda i: (i * tm // seq, 0, 0))],
        out_specs=pl.BlockSpec((tm, d), lambda i: (i, 0)),
        out_shape=jax.ShapeDtypeStruct((n, d), F32),
        compiler_params=_params("parallel"),
        name="outproj",
    )(out_a, out_b, w, x, mod)


def _ffn_body(a_ref, b_ref, wo_ref, x_ref, mod_ref, g_ref, fg_ref, wg_ref, wu_ref, wd_ref, o_ref,
              h_s, acc, *, final):
    j = pl.program_id(1)

    @pl.when(j == 0)
    def _prologue():
        ka = a_ref.shape[1]
        y = jnp.dot(a_ref[...], wo_ref[0:ka, :], preferred_element_type=F32)
        y = y + jnp.dot(b_ref[...], wo_ref[ka:, :], preferred_element_type=F32)
        x = x_ref[...] + mod_ref[0, G1:G1 + 1, :] * y
        o_ref[...] = x
        h = _modnorm(x, g_ref[...], mod_ref[0, SC2:SC2 + 1, :], mod_ref[0, SH2:SH2 + 1, :])
        h_s[...] = h.astype(BF16)
        acc[...] = jnp.zeros(acc.shape, F32)

    h = h_s[...]
    fs = wd_ref.shape[0]
    for c0 in range(0, fs, FFN_CHUNK):
        cs = slice(c0, min(c0 + FFN_CHUNK, fs))
        g = jnp.dot(h, wg_ref[0, :, cs], preferred_element_type=F32)
        u = jnp.dot(h, wu_ref[0, :, cs], preferred_element_type=F32)
        a = (g * jax.nn.sigmoid(g) * u).astype(BF16)
        acc[...] += jnp.dot(a, wd_ref[cs, :], preferred_element_type=F32)

    @pl.when(j == pl.num_programs(1) - 1)
    def _epilogue():
        y = o_ref[...] + mod_ref[0, G2:G2 + 1, :] * acc[...]
        if final:
            y = y * lax.rsqrt(jnp.mean(y * y, axis=-1, keepdims=True) + EPS) * fg_ref[...]
        o_ref[...] = y


def _ffn(out_a, out_b, w_out, x, mod, g, final_g, w_gate, w_up, w_down, seq, final):
    n, d = x.shape
    nsplit, _, fs = w_gate.shape
    ka, kb = out_a.shape[1], out_b.shape[1]
    tm = 1024
    return pl.pallas_call(
        functools.partial(_ffn_body, final=final),
        grid=(n // tm, nsplit),
        in_specs=[pl.BlockSpec((tm, ka), lambda i, j: (i, 0)),
                  pl.BlockSpec((tm, kb), lambda i, j: (i, 0)),
                  pl.BlockSpec((ka + kb, d), lambda i, j: (0, 0)),
                  pl.BlockSpec((tm, d), lambda i, j: (i, 0)),
                  pl.BlockSpec((1, N_MOD, d), lambda i, j: (i * tm // seq, 0, 0)),
                  pl.BlockSpec((1, d), lambda i, j: (0, 0)),
                  pl.BlockSpec((1, d), lambda i, j: (0, 0)),
                  pl.BlockSpec((1, d, fs), lambda i, j: (j, 0, 0)),
                  pl.BlockSpec((1, d, fs), lambda i, j: (j, 0, 0)),
                  pl.BlockSpec((fs, d), lambda i, j: (j, 0))],
        out_specs=pl.BlockSpec((tm, d), lambda i, j: (i, 0)),
        out_shape=jax.ShapeDtypeStruct((n, d), F32),
        scratch_shapes=[pltpu.VMEM((tm, d), BF16), pltpu.VMEM((tm, d), F32)],
        compiler_params=_params("parallel", "arbitrary"),
        name="ffn_mixer",
    )(out_a, out_b, w_out, x, mod, g, final_g, w_gate, w_up, w_down)


def _top2(logits):
    lane = lax.broadcasted_iota(jnp.int32, logits.shape, 1)
    lg = jnp.where(lane < N_EXPERTS, logits, -jnp.inf)
    m1 = jnp.max(lg, axis=-1, keepdims=True)
    i1 = jnp.min(jnp.where(lg == m1, lane, LANES), axis=-1, keepdims=True)
    lg2 = jnp.where(lane == i1, -jnp.inf, lg)
    m2 = jnp.max(lg2, axis=-1, keepdims=True)
    i2 = jnp.min(jnp.where(lg2 == m2, lane, LANES), axis=-1, keepdims=True)
    e2 = jnp.exp(m2 - m1)
    den = 1.0 + e2
    return i1, i2, 1.0 / den, e2 / den


def _router_body(x_ref, mod_ref, g_ref, wr_ref, before_ref, h_ref, post_ref, gatet_ref, cnt_ref):
    t = x_ref.shape[0]
    h = _modnorm(x_ref[...], g_ref[...], mod_ref[0, SC2:SC2 + 1, :], mod_ref[0, SH2:SH2 + 1, :])
    h_hi = h.astype(BF16)
    h_ref[...] = h_hi
    h_lo = (h - h_hi.astype(F32)).astype(BF16)
    wr = wr_ref[...]
    w_hi = wr.astype(BF16)
    w_lo = (wr - w_hi.astype(F32)).astype(BF16)
    logits = (jnp.dot(h_hi, w_hi, preferred_element_type=F32)
              + jnp.dot(h_lo, w_hi, preferred_element_type=F32)
              + jnp.dot(h_hi, w_lo, preferred_element_type=F32))
    i1, i2, w1, w2 = _top2(logits)
    lane = lax.broadcasted_iota(jnp.int32, logits.shape, 1)
    sel1 = lane == i1
    sel2 = lane == i2
    gate = jnp.where(sel1, w1, 0.0) + jnp.where(sel2, w2, 0.0)
    gatet_ref[...] = gate.T[0:N_EXPERTS, :]
    picked = jnp.logical_or(sel1, sel2)
    onehot = jnp.where(picked, 1.0, 0.0)
    rank = jnp.dot(before_ref[...], onehot.astype(BF16), preferred_element_type=F32)
    pos = jnp.where(picked, rank, -1.0)
    post_ref[...] = pos.T[0:N_EXPERTS, :]
    cnt_ref[0] = jnp.broadcast_to(jnp.sum(onehot, axis=0, keepdims=True), cnt_ref.shape[1:])


def _router(x, mod, g, w_router, seq):
    n, d = x.shape
    t = MOE_TILE
    nt = n // t
    wr = jnp.zeros((d, LANES), F32).at[:, :N_EXPERTS].set(w_router)
    before = jnp.asarray(np.tril(np.ones((t, t), np.float32), -1), dtype=BF16)
    return pl.pallas_call(
        _router_body,
        grid=(nt,),
        in_specs=[pl.BlockSpec((t, d), lambda i: (i, 0)),
                  pl.BlockSpec((1, N_MOD, d), lambda i: (i * t // seq, 0, 0)),
                  pl.BlockSpec((1, d), lambda i: (0, 0)),
                  pl.BlockSpec((d, LANES), lambda i: (0, 0)),
                  pl.BlockSpec((t, t), lambda i: (0, 0))],
        out_specs=[pl.BlockSpec((t, d), lambda i: (i, 0)),
                   pl.BlockSpec((N_EXPERTS, t), lambda i: (0, i)),
                   pl.BlockSpec((N_EXPERTS, t), lambda i: (0, i)),
                   pl.BlockSpec((1, 8, LANES), lambda i: (i, 0, 0))],
        out_shape=[jax.ShapeDtypeStruct((n, d), BF16),
                   jax.ShapeDtypeStruct((N_EXPERTS, n), F32),
                   jax.ShapeDtypeStruct((N_EXPERTS, n), F32),
                   jax.ShapeDtypeStruct((nt, 8, LANES), F32)],
        compiler_params=_params("parallel"),
        name="router",
    )(x, mod, g, wr, before)


def _moe_body(cnt_ref, h_ref, post_ref, gatet_ref, x_ref, mod_ref, fg_ref,
              wg_ref, wu_ref, wd_ref, o_ref, xg, acc, *, final):
    ti = pl.program_id(0)
    e = pl.program_id(1)
    s = pl.program_id(2)
    ns = pl.num_programs(2)
    t = h_ref.shape[0]
    cnt = cnt_ref[ti, e]

    @pl.when(jnp.logical_and(e == 0, s == 0))
    def _zero():
        o_ref[...] = jnp.zeros(o_ref.shape, F32)

    def block(b, bm):
        if isinstance(b, int):
            rows = slice(b * bm, (b + 1) * bm)
        else:
            rows = pl.ds(pl.multiple_of(b * bm, 16), bm)

        @pl.when(s == 0)
        def _gather():
            tgt = (lax.broadcasted_iota(jnp.int32, (bm, t), 0) + b * bm).astype(F32)
            onehot = jnp.where(post_ref[pl.ds(e, 1), :] == tgt, 1.0, 0.0).astype(BF16)
            xg[rows, :] = jnp.dot(onehot, h_ref[...], preferred_element_type=F32).astype(BF16)

        xb = xg[rows, :]
        g = jnp.dot(xb, wg_ref[0, 0], preferred_element_type=F32)
        u = jnp.dot(xb, wu_ref[0, 0], preferred_element_type=F32)
        a = (g * jax.nn.sigmoid(g) * u).astype(BF16)
        y = jnp.dot(a, wd_ref[0], preferred_element_type=F32)

        @pl.when(s == 0)
        def _first():
            acc[rows, :] = y

        @pl.when(s != 0)
        def _rest():
            acc[rows, :] += y

        @pl.when(s == ns - 1)
        def _scatter():
            tgt = (lax.broadcasted_iota(jnp.int32, (bm, t), 0) + b * bm).astype(F32)
            hit = post_ref[pl.ds(e, 1), :] == tgt
            gate_rows = jnp.sum(jnp.where(hit, gatet_ref[pl.ds(e, 1), :], 0.0), axis=-1,
                                keepdims=True)
            yb = (acc[rows, :] * gate_rows).astype(BF16)
            o_ref[...] += lax.dot_general(jnp.where(hit, 1.0, 0.0).astype(BF16), yb,
                                          (((0,), (0,)), ((), ())), preferred_element_type=F32)

    lo = MOE_LOOP_BLOCK
    for bm in MOE_BLOCKS:
        @pl.when(jnp.logical_and(cnt > lo, cnt <= bm))
        def _single(bm=bm):
            block(0, bm)
        lo = bm

    @pl.when(jnp.logical_or(cnt <= MOE_LOOP_BLOCK, cnt > lo))
    def _multi():
        bm = MOE_LOOP_BLOCK

        def body(b, carry):
            block(b, bm)
            return carry
        lax.fori_loop(0, (cnt + (bm - 1)) // bm, body, 0)

    @pl.when(jnp.logical_and(e == pl.num_programs(1) - 1, s == ns - 1))
    def _epilogue():
        y = x_ref[...] + mod_ref[0, G2:G2 + 1, :] * o_ref[...]
        if final:
            y = y * lax.rsqrt(jnp.mean(y * y, axis=-1, keepdims=True) + EPS) * fg_ref[...]
        o_ref[...] = y


def _moe(x, mod, g, final_g, w_router, w_gate, w_up, w_down, seq, final):
    n, d = x.shape
    n_e, f, _ = w_down.shape
    t = MOE_TILE
    nt = n // t
    fs = f // MOE_FSPLIT
    h, post, gatet, cnt = _router(x, mod, g, w_router, seq)
    counts = cnt[:, 0, :N_EXPERTS].astype(jnp.int32)
    cap = max(pl.cdiv(t, MOE_LOOP_BLOCK) * MOE_LOOP_BLOCK, MOE_BLOCKS[-1])
    grid_spec = pltpu.PrefetchScalarGridSpec(
        num_scalar_prefetch=1,
        grid=(nt, n_e, MOE_FSPLIT),
        in_specs=[pl.BlockSpec((t, d), lambda i, e, s, c: (i, 0)),
                  pl.BlockSpec((N_EXPERTS, t), lambda i, e, s, c: (0, i)),
                  pl.BlockSpec((N_EXPERTS, t), lambda i, e, s, c: (0, i)),
                  pl.BlockSpec((t, d), lambda i, e, s, c: (i, 0)),
                  pl.BlockSpec((1, N_MOD, d), lambda i, e, s, c: (i * t // seq, 0, 0)),
                  pl.BlockSpec((1, d), lambda i, e, s, c: (0, 0)),
                  pl.BlockSpec((1, 1, d, fs), lambda i, e, s, c: (s, e, 0, 0)),
                  pl.BlockSpec((1, 1, d, fs), lambda i, e, s, c: (s, e, 0, 0)),
                  pl.BlockSpec((1, fs, d), lambda i, e, s, c: (e, s, 0))],
        out_specs=pl.BlockSpec((t, d), lambda i, e, s, c: (i, 0)),
        scratch_shapes=[pltpu.VMEM((cap, d), BF16), pltpu.VMEM((cap, d), F32)],
    )
    return pl.pallas_call(
        functools.partial(_moe_body, final=final),
        grid_spec=grid_spec,
        out_shape=jax.ShapeDtypeStruct((n, d), F32),
        compiler_params=_params("parallel", "arbitrary", "arbitrary"),
        name="moe_mixer",
    )(counts, h, post, gatet, x, mod, final_g, w_gate, w_up, w_down)


def kernel(x, c, w_ada, b_ada, norm1_g, norm2_g, w_in, w_out, gmlp_vnorm_g, gmlp_ws, gmlp_bs,
           lam_q1, lam_k1, lam_q2, lam_k2, diff_subln_g, ffn_w_gate, ffn_w_up, ffn_w_down,
           w_router, moe_w_gate, moe_w_up, moe_w_down, final_g):
    batch, seq, d = x.shape
    depth = w_ada.shape[0]
    n = batch * seq
    mods = _ada(c, w_ada, b_ada)
    xf = x.reshape(n, d)
    fg = final_g.reshape(1, d)
    for l in range(depth):
        mod = mods[l]
        z, vt, out_a = _inproj(xf, mod, norm1_g[l].reshape(1, d), w_in[l].astype(BF16),
                               gmlp_vnorm_g[l], gmlp_ws[l], gmlp_bs[l], seq)
        lam_init = 0.8 - 0.6 * math.exp(-0.3 * l)
        lam_p = jnp.stack([lam_q1[l], lam_k1[l], lam_q2[l], lam_k2[l]])
        i = l // 2
        dense_casts = lambda j: [(w[:, None], j, p) for w, p in
                                 ((ffn_w_gate, FFN_FSPLIT), (ffn_w_up, FFN_FSPLIT), (ffn_w_down, 1))]
        if l % 2 == 0:
            casts = dense_casts(i) if l == 0 else []
            if l + 1 < depth:
                casts += [(moe_w_gate, i, MOE_FSPLIT), (moe_w_up, i, MOE_FSPLIT)]
        else:
            casts = [(moe_w_down, i, 1)] + (dense_casts(i + 1) if l + 1 < depth else [])
        out_b, cast_out = _attn(z, vt, lam_p, diff_subln_g[l], lam_init, batch, seq, casts)
        final = l == depth - 1
        g2 = norm2_g[l].reshape(1, d)
        if l % 2 == 0:
            if l == 0:
                dense_w, cast_out = cast_out[:3], cast_out[3:]
            xf = _ffn(out_a, out_b, w_out[l].astype(BF16), xf, mod, g2, fg,
                      dense_w[0][:, 0], dense_w[1][:, 0], dense_w[2][0, 0], seq, final)
            moe_gate_up = cast_out
        else:
            dense_w = cast_out[1:]
            xf = _outproj(out_a, out_b, w_out[l].astype(BF16), xf, mod, seq)
            xf = _moe(xf, mod, g2, fg, w_router[i], moe_gate_up[0], moe_gate_up[1], cast_out[0][0],
                      seq, final)
    return xf.reshape(batch, seq, d)
```

```python
import functools
import math

import jax
import jax.numpy as jnp
import numpy as np
from jax import lax
from jax.experimental import pallas as pl
from jax.experimental.pallas import tpu as pltpu

F32 = jnp.float32
BF16 = jnp.bfloat16
EPS = 1e-6
N_MOD = 6
N_GMLP_HEADS = 8
GMLP_HEAD_DIM = 64
CHUNK = 128
N_DIFF_HEADS = 4
DIFF_HEAD_DIM = 64
DIFF_V_DIM = 128
N_EXPERTS = 8
LANES = 128
VMEM_LIMIT = 56 * 1024 * 1024

MOE_TILE = 1024
MOE_BLOCKS = (256, 352, 448)
MOE_LOOP_BLOCK = 160
MOE_FSPLIT = 2
MOE_WSLOTS = 3
MOE_VMEM_LIMIT = 62 * 1024 * 1024

FFN_FSPLIT = 2
FFN_CHUNK = 256

ATTN_BLOCK = 1024
ATTN_ONES_ROWS = 16

SH1, SC1, G1, SH2, SC2, G2 = range(N_MOD)


def _params(*sem):
    return pltpu.CompilerParams(dimension_semantics=sem, vmem_limit_bytes=VMEM_LIMIT)


def _modnorm(x, g, sc, sh):
    ms = jnp.mean(x * x, axis=-1, keepdims=True)
    return (x * lax.rsqrt(ms + EPS)) * g * (1.0 + sc) + sh


def _ada_body(c_ref, w_ref, b_ref, o_ref):
    c = c_ref[...]
    sc = c * jax.nn.sigmoid(c)
    o_ref[0] = jnp.dot(sc, w_ref[0], preferred_element_type=F32,
                       precision=lax.Precision.HIGHEST) + b_ref[0]


def _ada(c, w_ada, b_ada):
    depth, d, m = w_ada.shape
    b = c.shape[0]
    rows = 8
    cp = jnp.zeros((rows, d), F32).at[:b].set(c)
    tn = 3072
    out = pl.pallas_call(
        _ada_body,
        grid=(depth, m // tn),
        in_specs=[pl.BlockSpec((rows, d), lambda l, j: (0, 0)),
                  pl.BlockSpec((1, d, tn), lambda l, j: (l, 0, j)),
                  pl.BlockSpec((1, 1, tn), lambda l, j: (l, 0, j))],
        out_specs=pl.BlockSpec((1, rows, tn), lambda l, j: (l, 0, j)),
        out_shape=jax.ShapeDtypeStruct((depth, rows, m), F32),
        compiler_params=_params("parallel", "parallel"),
        name="ada",
    )(cp, w_ada, b_ada.reshape(depth, 1, m))
    return out[:, :b].reshape(depth, b, N_MOD, d)


def _gmlp_chunk(u, v, vg, wp, bias, gm):
    lane = lax.broadcasted_iota(jnp.int32, (CHUNK, LANES), 1)
    left = lane < GMLP_HEAD_DIM
    gu = jax.nn.gelu(u)
    gv = jax.nn.gelu(v)
    v2 = gv * gv
    hi = v2.astype(BF16)
    lo = (v2 - hi.astype(F32)).astype(BF16)
    ms = (jnp.dot(hi, gm, preferred_element_type=F32)
          + jnp.dot(lo, gm, preferred_element_type=F32))
    vn = gv * lax.rsqrt(ms + EPS) * vg
    outs = []
    for p in range(len(wp)):
        vp = vn[:, p * LANES:(p + 1) * LANES]
        zero = jnp.zeros_like(vp)
        rhs = jnp.concatenate([jnp.where(left, vp, zero), jnp.where(left, zero, vp)],
                              axis=0).astype(BF16)
        outs.append(jnp.dot(wp[p], rhs, preferred_element_type=F32))
    return gu * (jnp.concatenate(outs, axis=1) + bias)


def _inproj_body(x_ref, mod_ref, g_ref, w_ref, wvt_ref, vg_ref, ws_ref, bias_ref, gm_ref,
                 z_ref, vt_ref, a_ref):
    h = _modnorm(x_ref[...], g_ref[...], mod_ref[0, SC1:SC1 + 1, :],
                 mod_ref[0, SH1:SH1 + 1, :]).astype(BF16)
    tm = x_ref.shape[0]
    gw = a_ref.shape[1]
    u = jnp.dot(h, w_ref[:, 0:gw], preferred_element_type=F32)
    v = jnp.dot(h, w_ref[:, gw:2 * gw], preferred_element_type=F32)
    cw = 512
    for c0 in range(0, z_ref.shape[1], cw):
        z_ref[:, c0:c0 + cw] = jnp.dot(h, w_ref[:, 2 * gw + c0:2 * gw + c0 + cw],
                                       preferred_element_type=F32).astype(z_ref.dtype)
    dve = DIFF_V_DIM + ATTN_ONES_ROWS
    vt = lax.dot_general(wvt_ref[...], h, (((1,), (1,)), ((), ())),
                         preferred_element_type=F32).astype(vt_ref.dtype)
    for hd in range(N_DIFF_HEADS):
        vt_ref[hd * dve:hd * dve + DIFF_V_DIM, :] = vt[hd * DIFF_V_DIM:(hd + 1) * DIFF_V_DIM, :]
        vt_ref[hd * dve + DIFF_V_DIM:(hd + 1) * dve, :] = jnp.ones((ATTN_ONES_ROWS, tm), vt_ref.dtype)
    row = lax.broadcasted_iota(jnp.int32, (CHUNK, 2 * CHUNK), 0)
    col = lax.broadcasted_iota(jnp.int32, (CHUNK, 2 * CHUNK), 1)
    causal = jnp.where(col >= CHUNK, col - CHUNK, col) <= row
    wp = [jnp.where(causal, ws_ref[p], jnp.zeros_like(ws_ref[p])) for p in range(ws_ref.shape[0])]
    for c in range(tm // CHUNK):
        rs = slice(c * CHUNK, (c + 1) * CHUNK)
        a_ref[rs, :] = _gmlp_chunk(u[rs, :], v[rs, :], vg_ref[...], wp, bias_ref[...],
                                   gm_ref[...]).astype(a_ref.dtype)


def _inproj(x, mod, g, w, vnorm_g, w_s, b_s, seq):
    n, d = x.shape
    vw = N_DIFF_HEADS * DIFF_V_DIM
    gw = N_GMLP_HEADS * GMLP_HEAD_DIM
    zw = w.shape[1] - vw - 2 * gw
    dve = DIFF_V_DIM + ATTN_ONES_ROWS
    npair = N_GMLP_HEADS // 2
    tm = 512
    w_pair = w_s.reshape(npair, 2, CHUNK, CHUNK).transpose(0, 2, 1, 3).reshape(
        npair, CHUNK, 2 * CHUNK).astype(BF16)
    bias = jnp.repeat(b_s.T, GMLP_HEAD_DIM, axis=1)
    grp = np.arange(gw) // GMLP_HEAD_DIM
    group_mean = jnp.asarray((grp[:, None] == grp[None, :]) / GMLP_HEAD_DIM, dtype=BF16)
    return pl.pallas_call(
        _inproj_body,
        grid=(n // tm,),
        in_specs=[pl.BlockSpec((tm, d), lambda i: (i, 0)),
                  pl.BlockSpec((1, N_MOD, d), lambda i: (i * tm // seq, 0, 0)),
                  pl.BlockSpec((1, d), lambda i: (0, 0)),
                  pl.BlockSpec((d, 2 * gw + zw), lambda i: (0, 0)),
                  pl.BlockSpec((vw, d), lambda i: (0, 0)),
                  pl.BlockSpec((1, gw), lambda i: (0, 0)),
                  pl.BlockSpec((npair, CHUNK, 2 * CHUNK), lambda i: (0, 0, 0)),
                  pl.BlockSpec((CHUNK, gw), lambda i: (0, 0)),
                  pl.BlockSpec((gw, gw), lambda i: (0, 0))],
        out_specs=[pl.BlockSpec((tm, zw), lambda i: (i, 0)),
                   pl.BlockSpec((N_DIFF_HEADS * dve, tm), lambda i: (0, i)),
                   pl.BlockSpec((tm, gw), lambda i: (i, 0))],
        out_shape=[jax.ShapeDtypeStruct((n, zw), BF16),
                   jax.ShapeDtypeStruct((N_DIFF_HEADS * dve, n), BF16),
                   jax.ShapeDtypeStruct((n, gw), BF16)],
        compiler_params=_params("parallel"),
        name="inproj_gmlp",
    )(x, mod, g, w, w[:, 2 * gw + zw:].T, vnorm_g.reshape(1, gw), w_pair, bias, group_mean)


def _attn_body(*refs, tq, lam_init, ncast):
    qi_ref, ki_ref, q_ref, k_ref, vt_ref, lam_ref, g_ref = refs[:7]
    cast_src = refs[7:7 + ncast]
    o_ref = refs[7 + ncast]
    cast_dst = refs[8 + ncast:8 + 2 * ncast]
    qs1, qs2, m1, m2, acc1, acc2 = refs[8 + 2 * ncast:]
    for src, dst in zip(cast_src, cast_dst):
        pw = dst.shape[2]
        for j in range(dst.shape[0]):
            dst[j] = src[:, j * pw:(j + 1) * pw].astype(dst.dtype)
    qi = qi_ref[pl.program_id(2)]
    ki = ki_ref[pl.program_id(2)]
    maps = ((qs1, m1, acc1), (qs2, m2, acc2))
    dv = DIFF_V_DIM

    @pl.when(ki == 0)
    def _init():
        qsc = (q_ref[...].astype(F32) * (DIFF_HEAD_DIM ** -0.5 * math.log2(math.e))).astype(BF16)
        lane = lax.broadcasted_iota(jnp.int32, qsc.shape, 1)
        zero = jnp.zeros_like(qsc)
        qs1[...] = jnp.where(lane < DIFF_HEAD_DIM, qsc, zero)
        qs2[...] = jnp.where(lane < DIFF_HEAD_DIM, zero, qsc)
        for _, m_s, acc in maps:
            m_s[...] = jnp.full(m_s.shape, -jnp.inf, F32)
            acc[...] = jnp.zeros(acc.shape, F32)

    def step(masked):
        k = k_ref[...]
        vt = vt_ref[...]
        sts = [lax.dot_general(k, qs[...], (((1,), (1,)), ((), ())), preferred_element_type=F32)
               for qs, _, _ in maps]
        for st, (_, m_s, acc) in zip(sts, maps):
            if masked:
                key = lax.broadcasted_iota(jnp.int32, st.shape, 0)
                qry = lax.broadcasted_iota(jnp.int32, st.shape, 1)
                st = jnp.where(key <= qry, st, -jnp.inf)
            sb = st.astype(BF16)
            m_prev = m_s[...]
            m_new = jnp.maximum(m_prev, jnp.max(sb, axis=0, keepdims=True).astype(F32))
            alpha = jnp.exp2(m_prev - m_new)
            pt = jnp.exp2(sb - m_new.astype(BF16))
            acc[...] = alpha * acc[...] + jnp.dot(vt, pt, preferred_element_type=F32)
            m_s[...] = m_new

    @pl.when(ki < qi)
    def _full():
        step(False)

    @pl.when(ki == qi)
    def _diag():
        step(True)
        lp = lam_ref[...]
        s1 = jnp.sum(lp[0:1, :] * lp[1:2, :], axis=-1, keepdims=True)
        s2 = jnp.sum(lp[2:3, :] * lp[3:4, :], axis=-1, keepdims=True)
        lam = jnp.exp(s1) - jnp.exp(s2) + lam_init
        ot = (acc1[0:dv, :] / acc1[dv:dv + 1, :]
              - lam * (acc2[0:dv, :] / acc2[dv:dv + 1, :]))
        ot = ot * lax.rsqrt(jnp.mean(ot * ot, axis=0, keepdims=True) + EPS)
        o_ref[...] = (ot.T * g_ref[...] * (1.0 - lam_init)).astype(o_ref.dtype)


def _cast_rows(rows, nsteps):
    for rb in range(16, rows + 1, 16):
        if rows % rb == 0 and rows // rb <= nsteps:
            return rb
    raise ValueError(f"no row block for {rows} rows in {nsteps} steps")


def _attn(z, vt, lam_p, subln_g, lam_init, batch, seq, casts=()):
    n = z.shape[0]
    tq = ATTN_BLOCK
    dve = DIFF_V_DIM + ATTN_ONES_ROWS
    nq = seq // tq
    qcol = 0
    kcol = qcol + N_DIFF_HEADS
    pairs = [(i, j) for i in range(nq) for j in range(i + 1)]
    npair = len(pairs)
    qi_tab = jnp.asarray([p[0] for p in pairs], jnp.int32)
    ki_tab = jnp.asarray([p[1] for p in pairs], jnp.int32)
    nsteps = batch * N_DIFF_HEADS * npair
    cast2d = [w.reshape(-1, w.shape[-1]) for w, _, _ in casts]
    cast_in, cast_out, cast_shape = [], [], []
    for w, i, pieces in casts:
        rows, cols = w.shape[1] * w.shape[2], w.shape[3]
        rb = _cast_rows(rows, nsteps)
        nb = rows // rb

        def blk(b, h, p, nb=nb):
            return jnp.minimum((b * N_DIFF_HEADS + h) * npair + p, nb - 1)
        cast_in.append(pl.BlockSpec(
            (rb, cols), lambda b, h, p, qt, kt, blk=blk, off=i * nb: (off + blk(b, h, p), 0)))
        cast_out.append(pl.BlockSpec(
            (pieces, rb, cols // pieces), lambda b, h, p, qt, kt, blk=blk: (0, blk(b, h, p), 0)))
        cast_shape.append((pieces, rows, cols // pieces))
    grid_spec = pltpu.PrefetchScalarGridSpec(
        num_scalar_prefetch=2,
        grid=(batch, N_DIFF_HEADS, npair),
        in_specs=[
            pl.BlockSpec((tq, LANES), lambda b, h, p, qt, kt: (b * nq + qt[p], qcol + h)),
            pl.BlockSpec((tq, LANES), lambda b, h, p, qt, kt: (b * nq + kt[p], kcol + h)),
            pl.BlockSpec((dve, tq), lambda b, h, p, qt, kt: (h, b * nq + kt[p])),
            pl.BlockSpec((4, DIFF_HEAD_DIM), lambda b, h, p, qt, kt: (0, 0)),
            pl.BlockSpec((1, DIFF_V_DIM), lambda b, h, p, qt, kt: (0, 0)),
        ] + cast_in,
        out_specs=[pl.BlockSpec((tq, LANES), lambda b, h, p, qt, kt: (b * nq + qt[p], h))]
        + cast_out,
        scratch_shapes=(2 * [pltpu.VMEM((tq, LANES), BF16)] + 2 * [pltpu.VMEM((1, tq), F32)]
                        + 2 * [pltpu.VMEM((dve, tq), F32)]),
    )
    outs = pl.pallas_call(
        functools.partial(_attn_body, tq=tq, lam_init=lam_init, ncast=len(casts)),
        grid_spec=grid_spec,
        out_shape=[jax.ShapeDtypeStruct((n, N_DIFF_HEADS * DIFF_V_DIM), BF16)]
        + [jax.ShapeDtypeStruct(s, BF16) for s in cast_shape],
        compiler_params=_params("arbitrary", "arbitrary", "arbitrary"),
        name="diff_attn",
    )(qi_tab, ki_tab, z, z, vt, lam_p, subln_g.reshape(1, DIFF_V_DIM), *cast2d)
    return outs[0], [o.reshape(pieces, w.shape[1], w.shape[2], w.shape[3] // pieces)
                     for o, (w, _, pieces) in zip(outs[1:], casts)]


def _outproj_body(a_ref, b_ref, w_ref, x_ref, mod_ref, o_ref):
    ka = a_ref.shape[1]
    y = jnp.dot(a_ref[...], w_ref[0:ka, :], preferred_element_type=F32)
    y = y + jnp.dot(b_ref[...], w_ref[ka:, :], preferred_element_type=F32)
    o_ref[...] = x_ref[...] + mod_ref[0, G1:G1 + 1, :] * y


def _outproj(out_a, out_b, w, x, mod, seq):
    n, d = x.shape
    tm = 512
    ka, kb = out_a.shape[1], out_b.shape[1]
    return pl.pallas_call(
        _outproj_body,
        grid=(n // tm,),
        in_specs=[pl.BlockSpec((tm, ka), lambda i: (i, 0)),
                  pl.BlockSpec((tm, kb), lambda i: (i, 0)),
                  pl.BlockSpec((ka + kb, d), lambda i: (0, 0)),
                  pl.BlockSpec((tm, d), lambda i: (i, 0)),
                  pl.BlockSpec((1, N_MOD, d), lambda i: (i * tm // seq, 0, 0))],
        out_specs=pl.BlockSpec((tm, d), lambda i: (i, 0)),
        out_shape=jax.ShapeDtypeStruct((n, d), F32),
        compiler_params=_params("parallel"),
        name="outproj",
    )(out_a, out_b, w, x, mod)


def _ffn_body(a_ref, b_ref, wo_ref, x_ref, mod_ref, g_ref, fg_ref, wg_ref, wu_ref, wd_ref, o_ref,
              h_s, acc, *, final):
    j = pl.program_id(1)

    @pl.when(j == 0)
    def _prologue():
        ka = a_ref.shape[1]
        y = jnp.dot(a_ref[...], wo_ref[0:ka, :], preferred_element_type=F32)
        y = y + jnp.dot(b_ref[...], wo_ref[ka:, :], preferred_element_type=F32)
        x = x_ref[...] + mod_ref[0, G1:G1 + 1, :] * y
        o_ref[...] = x
        h = _modnorm(x, g_ref[...], mod_ref[0, SC2:SC2 + 1, :], mod_ref[0, SH2:SH2 + 1, :])
        h_s[...] = h.astype(BF16)
        acc[...] = jnp.zeros(acc.shape, F32)

    h = h_s[...]
    fs = wd_ref.shape[0]
    for c0 in range(0, fs, FFN_CHUNK):
        cs = slice(c0, min(c0 + FFN_CHUNK, fs))
        g = jnp.dot(h, wg_ref[0, :, cs], preferred_element_type=F32)
        u = jnp.dot(h, wu_ref[0, :, cs], preferred_element_type=F32)
        a = (g * jax.nn.sigmoid(g) * u).astype(BF16)
        acc[...] += jnp.dot(a, wd_ref[cs, :], preferred_element_type=F32)

    @pl.when(j == pl.num_programs(1) - 1)
    def _epilogue():
        y = o_ref[...] + mod_ref[0, G2:G2 + 1, :] * acc[...]
        if final:
            y = y * lax.rsqrt(jnp.mean(y * y, axis=-1, keepdims=True) + EPS) * fg_ref[...]
        o_ref[...] = y


def _ffn(out_a, out_b, w_out, x, mod, g, final_g, w_gate, w_up, w_down, seq, final):
    n, d = x.shape
    nsplit, _, fs = w_gate.shape
    ka, kb = out_a.shape[1], out_b.shape[1]
    tm = 1024
    return pl.pallas_call(
        functools.partial(_ffn_body, final=final),
        grid=(n // tm, nsplit),
        in_specs=[pl.BlockSpec((tm, ka), lambda i, j: (i, 0)),
                  pl.BlockSpec((tm, kb), lambda i, j: (i, 0)),
                  pl.BlockSpec((ka + kb, d), lambda i, j: (0, 0)),
                  pl.BlockSpec((tm, d), lambda i, j: (i, 0)),
                  pl.BlockSpec((1, N_MOD, d), lambda i, j: (i * tm // seq, 0, 0)),
                  pl.BlockSpec((1, d), lambda i, j: (0, 0)),
                  pl.BlockSpec((1, d), lambda i, j: (0, 0)),
                  pl.BlockSpec((1, d, fs), lambda i, j: (j, 0, 0)),
                  pl.BlockSpec((1, d, fs), lambda i, j: (j, 0, 0)),
                  pl.BlockSpec((fs, d), lambda i, j: (j, 0))],
        out_specs=pl.BlockSpec((tm, d), lambda i, j: (i, 0)),
        out_shape=jax.ShapeDtypeStruct((n, d), F32),
        scratch_shapes=[pltpu.VMEM((tm, d), BF16), pltpu.VMEM((tm, d), F32)],
        compiler_params=_params("parallel", "arbitrary"),
        name="ffn_mixer",
    )(out_a, out_b, w_out, x, mod, g, final_g, w_gate, w_up, w_down)


def _top2(logits):
    lane = lax.broadcasted_iota(jnp.int32, logits.shape, 1)
    lg = jnp.where(lane < N_EXPERTS, logits, -jnp.inf)
    m1 = jnp.max(lg, axis=-1, keepdims=True)
    i1 = jnp.min(jnp.where(lg == m1, lane, LANES), axis=-1, keepdims=True)
    lg2 = jnp.where(lane == i1, -jnp.inf, lg)
    m2 = jnp.max(lg2, axis=-1, keepdims=True)
    i2 = jnp.min(jnp.where(lg2 == m2, lane, LANES), axis=-1, keepdims=True)
    e2 = jnp.exp(m2 - m1)
    den = 1.0 + e2
    return i1, i2, 1.0 / den, e2 / den


def _router_body(x_ref, mod_ref, g_ref, wr_ref, before_ref, h_ref, post_ref, gatet_ref, cnt_ref):
    t = x_ref.shape[0]
    h = _modnorm(x_ref[...], g_ref[...], mod_ref[0, SC2:SC2 + 1, :], mod_ref[0, SH2:SH2 + 1, :])
    h_hi = h.astype(BF16)
    h_ref[...] = h_hi
    h_lo = (h - h_hi.astype(F32)).astype(BF16)
    wr = wr_ref[...]
    w_hi = wr.astype(BF16)
    w_lo = (wr - w_hi.astype(F32)).astype(BF16)
    logits = (jnp.dot(h_hi, w_hi, preferred_element_type=F32)
              + jnp.dot(h_lo, w_hi, preferred_element_type=F32)
              + jnp.dot(h_hi, w_lo, preferred_element_type=F32))
    i1, i2, w1, w2 = _top2(logits)
    lane = lax.broadcasted_iota(jnp.int32, logits.shape, 1)
    sel1 = lane == i1
    sel2 = lane == i2
    gate = jnp.where(sel1, w1, 0.0) + jnp.where(sel2, w2, 0.0)
    gatet_ref[...] = gate.T[0:N_EXPERTS, :]
    picked = jnp.logical_or(sel1, sel2)
    onehot = jnp.where(picked, 1.0, 0.0)
    rank = jnp.dot(before_ref[...], onehot.astype(BF16), preferred_element_type=F32)
    pos = jnp.where(picked, rank, -1.0)
    post_ref[...] = pos.T[0:N_EXPERTS, :]
    cnt_ref[0] = jnp.broadcast_to(jnp.sum(onehot, axis=0, keepdims=True), cnt_ref.shape[1:])


def _router(x, mod, g, w_router, seq):
    n, d = x.shape
    t = MOE_TILE
    nt = n // t
    wr = jnp.zeros((d, LANES), F32).at[:, :N_EXPERTS].set(w_router)
    before = jnp.asarray(np.tril(np.ones((t, t), np.float32), -1), dtype=BF16)
    return pl.pallas_call(
        _router_body,
        grid=(nt,),
        in_specs=[pl.BlockSpec((t, d), lambda i: (i, 0)),
                  pl.BlockSpec((1, N_MOD, d), lambda i: (i * t // seq, 0, 0)),
                  pl.BlockSpec((1, d), lambda i: (0, 0)),
                  pl.BlockSpec((d, LANES), lambda i: (0, 0)),
                  pl.BlockSpec((t, t), lambda i: (0, 0))],
        out_specs=[pl.BlockSpec((t, d), lambda i: (i, 0)),
                   pl.BlockSpec((N_EXPERTS, t), lambda i: (0, i)),
                   pl.BlockSpec((N_EXPERTS, t), lambda i: (0, i)),
                   pl.BlockSpec((1, 8, LANES), lambda i: (i, 0, 0))],
        out_shape=[jax.ShapeDtypeStruct((n, d), BF16),
                   jax.ShapeDtypeStruct((N_EXPERTS, n), F32),
                   jax.ShapeDtypeStruct((N_EXPERTS, n), F32),
                   jax.ShapeDtypeStruct((nt, 8, LANES), F32)],
        compiler_params=_params("parallel"),
        name="router",
    )(x, mod, g, wr, before)


def _moe_body(cnt_ref, h_ref, post_ref, gatet_ref, x_ref, mod_ref, fg_ref,
              wg_hbm, wu_hbm, wd_hbm, o_ref, xg, acc, wgb, wub, wdb, sem, *, final):
    ti = pl.program_id(0)
    e = pl.program_id(1)
    s = pl.program_id(2)
    ns = pl.num_programs(2)
    t = h_ref.shape[0]
    cnt = cnt_ref[ti, e]
    n_e = pl.num_programs(1)
    fs = wdb.shape[1]
    total = pl.num_programs(0) * n_e * ns
    k = (ti * n_e + e) * ns + s

    def copies(kk, slot):
        ss = kk % ns
        ee = (kk // ns) % n_e
        return (pltpu.make_async_copy(wg_hbm.at[ss, ee], wgb.at[slot], sem.at[0, slot]),
                pltpu.make_async_copy(wu_hbm.at[ss, ee], wub.at[slot], sem.at[1, slot]),
                pltpu.make_async_copy(wd_hbm.at[ee, pl.ds(pl.multiple_of(ss * fs, 16), fs)],
                                      wdb.at[slot], sem.at[2, slot]))

    @pl.when(k == 0)
    def _prime():
        for kk in (0, 1):
            for cp in copies(kk, kk):
                cp.start()

    @pl.when(k + 2 < total)
    def _prefetch():
        for cp in copies(k + 2, (k + 2) % MOE_WSLOTS):
            cp.start()

    slot = k % MOE_WSLOTS
    for cp in copies(k, slot):
        cp.wait()

    @pl.when(jnp.logical_and(e == 0, s == 0))
    def _zero():
        o_ref[...] = jnp.zeros(o_ref.shape, F32)

    def block(b, bm):
        if isinstance(b, int):
            rows = slice(b * bm, (b + 1) * bm)
        else:
            rows = pl.ds(pl.multiple_of(b * bm, 16), bm)

        @pl.when(s == 0)
        def _gather():
            tgt = (lax.broadcasted_iota(jnp.int32, (bm, t), 0) + b * bm).astype(F32)
            onehot = jnp.where(post_ref[pl.ds(e, 1), :] == tgt, 1.0, 0.0).astype(BF16)
            xg[rows, :] = jnp.dot(onehot, h_ref[...], preferred_element_type=F32).astype(BF16)

        xb = xg[rows, :]
        g = jnp.dot(xb, wgb[slot], preferred_element_type=F32)
        u = jnp.dot(xb, wub[slot], preferred_element_type=F32)
        a = (g * jax.nn.sigmoid(g) * u).astype(BF16)
        y = jnp.dot(a, wdb[slot], preferred_element_type=F32)

        @pl.when(s == 0)
        def _first():
            acc[rows, :] = y

        @pl.when(s != 0)
        def _rest():
            acc[rows, :] += y

        @pl.when(s == ns - 1)
        def _scatter():
            tgt = (lax.broadcasted_iota(jnp.int32, (bm, t), 0) + b * bm).astype(F32)
            hit = post_ref[pl.ds(e, 1), :] == tgt
            gate_rows = jnp.sum(jnp.where(hit, gatet_ref[pl.ds(e, 1), :], 0.0), axis=-1,
                                keepdims=True)
            yb = (acc[rows, :] * gate_rows).astype(BF16)
            o_ref[...] += lax.dot_general(jnp.where(hit, 1.0, 0.0).astype(BF16), yb,
                                          (((0,), (0,)), ((), ())), preferred_element_type=F32)

    lo = MOE_LOOP_BLOCK
    for bm in MOE_BLOCKS:
        @pl.when(jnp.logical_and(cnt > lo, cnt <= bm))
        def _single(bm=bm):
            block(0, bm)
        lo = bm

    @pl.when(jnp.logical_or(cnt <= MOE_LOOP_BLOCK, cnt > lo))
    def _multi():
        bm = MOE_LOOP_BLOCK

        def body(b, carry):
            block(b, bm)
            return carry
        lax.fori_loop(0, (cnt + (bm - 1)) // bm, body, 0)

    @pl.when(jnp.logical_and(e == pl.num_programs(1) - 1, s == ns - 1))
    def _epilogue():
        y = x_ref[...] + mod_ref[0, G2:G2 + 1, :] * o_ref[...]
        if final:
            y = y * lax.rsqrt(jnp.mean(y * y, axis=-1, keepdims=True) + EPS) * fg_ref[...]
        o_ref[...] = y


def _moe(x, mod, g, final_g, w_router, w_gate, w_up, w_down, seq, final):
    n, d = x.shape
    n_e, f, _ = w_down.shape
    t = MOE_TILE
    nt = n // t
    fs = f // MOE_FSPLIT
    h, post, gatet, cnt = _router(x, mod, g, w_router, seq)
    counts = cnt[:, 0, :N_EXPERTS].astype(jnp.int32)
    cap = max(pl.cdiv(t, MOE_LOOP_BLOCK) * MOE_LOOP_BLOCK, MOE_BLOCKS[-1])
    grid_spec = pltpu.PrefetchScalarGridSpec(
        num_scalar_prefetch=1,
        grid=(nt, n_e, MOE_FSPLIT),
        in_specs=[pl.BlockSpec((t, d), lambda i, e, s, c: (i, 0)),
                  pl.BlockSpec((N_EXPERTS, t), lambda i, e, s, c: (0, i)),
                  pl.BlockSpec((N_EXPERTS, t), lambda i, e, s, c: (0, i)),
                  pl.BlockSpec((t, d), lambda i, e, s, c: (i, 0)),
                  pl.BlockSpec((1, N_MOD, d), lambda i, e, s, c: (i * t // seq, 0, 0)),
                  pl.BlockSpec((1, d), lambda i, e, s, c: (0, 0)),
                  pl.BlockSpec(memory_space=pl.ANY),
                  pl.BlockSpec(memory_space=pl.ANY),
                  pl.BlockSpec(memory_space=pl.ANY)],
        out_specs=pl.BlockSpec((t, d), lambda i, e, s, c: (i, 0)),
        scratch_shapes=[pltpu.VMEM((cap, d), BF16), pltpu.VMEM((cap, d), F32),
                        pltpu.VMEM((MOE_WSLOTS, d, fs), BF16), pltpu.VMEM((MOE_WSLOTS, d, fs), BF16),
                        pltpu.VMEM((MOE_WSLOTS, fs, d), BF16),
                        pltpu.SemaphoreType.DMA((3, MOE_WSLOTS))],
    )
    return pl.pallas_call(
        functools.partial(_moe_body, final=final),
        grid_spec=grid_spec,
        out_shape=jax.ShapeDtypeStruct((n, d), F32),
        compiler_params=pltpu.CompilerParams(
            dimension_semantics=("arbitrary", "arbitrary", "arbitrary"),
            vmem_limit_bytes=MOE_VMEM_LIMIT),
        name="moe_mixer",
    )(counts, h, post, gatet, x, mod, final_g, w_gate, w_up, w_down)


def kernel(x, c, w_ada, b_ada, norm1_g, norm2_g, w_in, w_out, gmlp_vnorm_g, gmlp_ws, gmlp_bs,
           lam_q1, lam_k1, lam_q2, lam_k2, diff_subln_g, ffn_w_gate, ffn_w_up, ffn_w_down,
           w_router, moe_w_gate, moe_w_up, moe_w_down, final_g):
    batch, seq, d = x.shape
    depth = w_ada.shape[0]
    n = batch * seq
    mods = _ada(c, w_ada, b_ada)
    xf = x.reshape(n, d)
    fg = final_g.reshape(1, d)
    for l in range(depth):
        mod = mods[l]
        z, vt, out_a = _inproj(xf, mod, norm1_g[l].reshape(1, d), w_in[l].astype(BF16),
                               gmlp_vnorm_g[l], gmlp_ws[l], gmlp_bs[l], seq)
        lam_init = 0.8 - 0.6 * math.exp(-0.3 * l)
        lam_p = jnp.stack([lam_q1[l], lam_k1[l], lam_q2[l], lam_k2[l]])
        i = l // 2
        dense_casts = lambda j: [(w[:, None], j, p) for w, p in
                                 ((ffn_w_gate, FFN_FSPLIT), (ffn_w_up, FFN_FSPLIT), (ffn_w_down, 1))]
        if l % 2 == 0:
            casts = dense_casts(i) if l == 0 else []
            if l + 1 < depth:
                casts += [(moe_w_gate, i, MOE_FSPLIT), (moe_w_up, i, MOE_FSPLIT)]
        else:
            casts = [(moe_w_down, i, 1)] + (dense_casts(i + 1) if l + 1 < depth else [])
        out_b, cast_out = _attn(z, vt, lam_p, diff_subln_g[l], lam_init, batch, seq, casts)
        final = l == depth - 1
        g2 = norm2_g[l].reshape(1, d)
        if l % 2 == 0:
            if l == 0:
                dense_w, cast_out = cast_out[:3], cast_out[3:]
            xf = _ffn(out_a, out_b, w_out[l].astype(BF16), xf, mod, g2, fg,
                      dense_w[0][:, 0], dense_w[1][:, 0], dense_w[2][0, 0], seq, final)
            moe_gate_up = cast_out
        else:
            dense_w = cast_out[1:]
            xf = _outproj(out_a, out_b, w_out[l].astype(BF16), xf, mod, seq)
            xf = _moe(xf, mod, g2, fg, w_router[i], moe_gate_up[0], moe_gate_up[1], cast_out[0][0],
                      seq, final)
    return xf.reshape(batch, seq, d)
```
